```python
import math, functools
import jax, jax.numpy as jnp
from jax import lax
import numpy as np

D_MODEL = 1024
BATCH = 16
SEQ = 2048
DEPTH = 1

CTX_LEN = 256
GRID_W = 64
ROPE_THETA = 10000.0
NORM_EPS = 1e-6
Q_BLOCK = 128

A_HEADS = 4
A_HEAD_DIM = 64
A_WIDTH = A_HEADS * 2 * A_HEAD_DIM
B_HEADS = 8
B_NOPE = 64
B_ROPE = 32
B_VDIM = 64
B_Q_RANK = 256
B_KV_RANK = 128
B_WIDTH = B_HEADS * B_VDIM
MIX_WIDTH = A_WIDTH + B_WIDTH
IN_WIDTH = 3 * A_WIDTH + B_Q_RANK + B_KV_RANK + B_ROPE
SPLIT_POINTS = (A_WIDTH, 2 * A_WIDTH, 3 * A_WIDTH, 3 * A_WIDTH + B_Q_RANK,
                3 * A_WIDTH + B_Q_RANK + B_KV_RANK)
FFN_HIDDEN = -(-8 * D_MODEL // (3 * 256)) * 256

kernel_name = "hybrid_diffattn_mla_dit_layer"


def _rmsnorm(x, gain=None):
    x32 = x.astype(jnp.float32)
    y = x32 * lax.rsqrt(jnp.mean(x32 * x32, axis=-1, keepdims=True) + NORM_EPS)
    if gain is not None:
        y = y * gain.astype(jnp.float32)
    return y.astype(x.dtype)


def _modulate(h, shift, scale):
    return h * (1 + scale) + shift


def _axial_rope(x, row, col):
    half = x.shape[-1] // 2
    quarter = half // 2
    inv = ROPE_THETA ** (-jnp.arange(quarter, dtype=jnp.float32) / quarter)

    def rot(xa, pos):
        ang = pos.astype(jnp.float32)[:, None] * inv[None, :]
        cos = jnp.concatenate([jnp.cos(ang)] * 2, -1)[None, :, None, :]
        sin = jnp.concatenate([jnp.sin(ang)] * 2, -1)[None, :, None, :]
        xa32 = xa.astype(jnp.float32)
        rotated = jnp.concatenate([-xa32[..., quarter:], xa32[..., :quarter]], -1)
        return (xa32 * cos + rotated * sin).astype(x.dtype)

    return jnp.concatenate([rot(x[..., :half], row), rot(x[..., half:], col)], -1)


def _project(h, w_in, q_a_norm, kv_a_norm, w_q_up, w_kv_up, pos):
    nb, s, _ = h.shape
    qa, ka, va, cq, ckv, krope = jnp.split(h @ w_in, SPLIT_POINTS, axis=-1)
    qa = qa.reshape(nb, s, A_HEADS, 2, A_HEAD_DIM)
    ka = ka.reshape(nb, s, A_HEADS, 2, A_HEAD_DIM)
    q1, q2 = qa[..., 0, :], qa[..., 1, :]
    k1, k2 = ka[..., 0, :], ka[..., 1, :]
    va = va.reshape(nb, s, A_HEADS, 2 * A_HEAD_DIM)
    qb = (_rmsnorm(cq, q_a_norm) @ w_q_up).reshape(nb, s, B_HEADS, B_NOPE + B_ROPE)
    kvb = (_rmsnorm(ckv, kv_a_norm) @ w_kv_up).reshape(nb, s, B_HEADS, B_NOPE + B_VDIM)
    q_nope, q_rope = qb[..., :B_NOPE], qb[..., B_NOPE:]
    k_nope, vb = kvb[..., :B_NOPE], kvb[..., B_NOPE:]
    krope = krope[:, :, None, :]
    if pos is not None:
        row, col = pos
        q1, q2, k1, k2 = (_axial_rope(t, row, col) for t in (q1, q2, k1, k2))
        q_rope = _axial_rope(q_rope, row, col)
        krope = _axial_rope(krope, row, col)
    qb = jnp.concatenate([q_nope, q_rope], -1)
    kb = jnp.concatenate([k_nope, jnp.broadcast_to(krope, (nb, s, B_HEADS, B_ROPE))], -1)
    return q1, q2, k1, k2, va, qb, kb, vb


def _diff_attend(q1, q2, k1, k2, v, lam):
    scale = A_HEAD_DIM ** -0.5
    s1 = jnp.einsum('bqhd,bkhd->bhqk', q1, k1).astype(jnp.float32) * scale
    s2 = jnp.einsum('bqhd,bkhd->bhqk', q2, k2).astype(jnp.float32) * scale
    a = jax.nn.softmax(s1, axis=-1) - lam * jax.nn.softmax(s2, axis=-1)
    return jnp.einsum('bhqk,bkhe->bqhe', a.astype(v.dtype), v)


def _softmax_attend(q, k, v):
    scale = (B_NOPE + B_ROPE) ** -0.5
    s = jnp.einsum('bqhd,bkhd->bhqk', q, k).astype(jnp.float32) * scale
    a = jax.nn.softmax(s, axis=-1)
    return jnp.einsum('bhqk,bkhe->bqhe', a.astype(v.dtype), v)


def _map_query_blocks(fn, *qs):
    nb, s = qs[0].shape[:2]
    n_blk = s // Q_BLOCK
    blocks = tuple(jnp.moveaxis(q.reshape(nb, n_blk, Q_BLOCK, *q.shape[2:]), 1, 0) for q in qs)
    out = lax.map(lambda blk: fn(*blk), blocks)
    out = jnp.moveaxis(out, 0, 1)
    return out.reshape(nb, s, *out.shape[3:])


def _merge_heads(oa, ob, lam_init, diff_subln, w_out):
    nb, s = oa.shape[:2]
    oa = _rmsnorm(oa, diff_subln) * (1 - lam_init)
    cat = jnp.concatenate([oa.reshape(nb, s, A_WIDTH), ob.reshape(nb, s, B_WIDTH)], -1)
    return cat @ w_out


def _swiglu(h, w_ffn_in, w_ffn_out):
    gate, up = jnp.split(h @ w_ffn_in, 2, axis=-1)
    return (jax.nn.silu(gate) * up) @ w_ffn_out


def _layer(x_lat, x_ctx, row, col, c, c_ctx, w_ada, b_ada, w_in, q_a_norm, kv_a_norm,
           w_q_up, w_kv_up, diff_lambda, diff_subln, w_out, w_ffn_in, w_ffn_out,
           layer_idx, update_ctx):
    lam_init = 0.8 - 0.6 * math.exp(-0.3 * layer_idx)
    lp = diff_lambda.astype(jnp.float32)
    lam = jnp.exp(jnp.sum(lp[0] * lp[1])) - jnp.exp(jnp.sum(lp[2] * lp[3])) + lam_init

    sh_a, sc_a, g_a, sh_f, sc_f, g_f = jnp.split((jax.nn.silu(c) @ w_ada + b_ada)[:, None, :], 6, axis=-1)
    csh_a, csc_a, cg_a, csh_f, csc_f, cg_f = jnp.split(jax.nn.silu(c_ctx) @ w_ada + b_ada, 6, axis=-1)

    proj = functools.partial(_project, w_in=w_in, q_a_norm=q_a_norm, kv_a_norm=kv_a_norm,
                             w_q_up=w_q_up, w_kv_up=w_kv_up)
    h_lat = _modulate(_rmsnorm(x_lat), sh_a, sc_a)
    h_ctx = _modulate(_rmsnorm(x_ctx), csh_a, csc_a)
    lq1, lq2, lk1, lk2, lva, lqb, lkb, lvb = proj(h_lat, pos=(row, col))
    cq1, cq2, ck1, ck2, cva, cqb, ckb, cvb = proj(h_ctx, pos=None)

    k1 = jnp.concatenate([ck1, lk1], axis=1)
    k2 = jnp.concatenate([ck2, lk2], axis=1)
    va = jnp.concatenate([cva, lva], axis=1)
    kb = jnp.concatenate([ckb, lkb], axis=1)
    vb = jnp.concatenate([cvb, lvb], axis=1)
    oa = _map_query_blocks(lambda q1, q2: _diff_attend(q1, q2, k1, k2, va, lam), lq1, lq2)
    ob = _map_query_blocks(lambda q: _softmax_attend(q, kb, vb), lqb)
    new_lat = x_lat + g_a * _merge_heads(oa, ob, lam_init, diff_subln, w_out)
    h = _modulate(_rmsnorm(new_lat), sh_f, sc_f)
    new_lat = new_lat + g_f * _swiglu(h, w_ffn_in, w_ffn_out)

    if update_ctx:
        oac = _diff_attend(cq1, cq2, ck1, ck2, cva, lam)
        obc = _softmax_attend(cqb, ckb, cvb)
        new_ctx = x_ctx + cg_a * _merge_heads(oac, obc, lam_init, diff_subln, w_out)
        hc = _modulate(_rmsnorm(new_ctx), csh_f, csc_f)
        new_ctx = new_ctx + cg_f * _swiglu(hc, w_ffn_in, w_ffn_out)
    else:
        new_ctx = x_ctx
    return new_lat, new_ctx


def setup_inputs(seed: int = 0) -> dict:
    key = jax.random.key(seed)
    ks = jax.random.split(key, 17)
    f32 = jnp.float32

    def nrm(k, shape, scale):
        return jax.random.normal(k, shape, f32) * scale

    return {
        "x": nrm(ks[0], (BATCH, SEQ, D_MODEL), 1.0),
        "c": nrm(ks[1], (BATCH, D_MODEL), 1.0),
        "ctx": nrm(ks[2], (BATCH, CTX_LEN, D_MODEL), 1.0),
        "c_ctx": nrm(ks[3], (D_MODEL,), 1.0),
        "w_ada": nrm(ks[4], (DEPTH, D_MODEL, 6 * D_MODEL), 0.5 * D_MODEL ** -0.5),
        "b_ada": nrm(ks[5], (DEPTH, 6 * D_MODEL), 0.01),
        "w_in": nrm(ks[6], (DEPTH, D_MODEL, IN_WIDTH), D_MODEL ** -0.5),
        "q_a_norm": 1.0 + nrm(ks[7], (DEPTH, B_Q_RANK), 0.05),
        "kv_a_norm": 1.0 + nrm(ks[8], (DEPTH, B_KV_RANK), 0.05),
        "w_q_up": nrm(ks[9], (DEPTH, B_Q_RANK, B_HEADS * (B_NOPE + B_ROPE)), B_Q_RANK ** -0.5),
        "w_kv_up": nrm(ks[10], (DEPTH, B_KV_RANK, B_HEADS * (B_NOPE + B_VDIM)), B_KV_RANK ** -0.5),
        "diff_lambda": nrm(ks[11], (DEPTH, 4, A_HEAD_DIM), 0.1),
        "diff_subln": 1.0 + nrm(ks[12], (DEPTH, 2 * A_HEAD_DIM), 0.05),
        "w_out": nrm(ks[13], (DEPTH, MIX_WIDTH, D_MODEL), MIX_WIDTH ** -0.5),
        "w_ffn_in": nrm(ks[14], (DEPTH, D_MODEL, 2 * FFN_HIDDEN), D_MODEL ** -0.5),
        "w_ffn_out": nrm(ks[15], (DEPTH, FFN_HIDDEN, D_MODEL), FFN_HIDDEN ** -0.5),
        "final_norm": 1.0 + nrm(ks[16], (D_MODEL,), 0.05),
    }


def reference(x, c, ctx, c_ctx, w_ada, b_ada, w_in, q_a_norm, kv_a_norm, w_q_up, w_kv_up,
              diff_lambda, diff_subln, w_out, w_ffn_in, w_ffn_out, final_norm):
    seq = x.shape[1]
    rows = seq // GRID_W
    row = jnp.repeat(jnp.arange(rows, dtype=jnp.int32), GRID_W)
    col = jnp.tile(jnp.arange(GRID_W, dtype=jnp.int32), rows)
    x_lat, x_ctx = x, ctx
    for layer in range(DEPTH):
        x_lat, x_ctx = _layer(
            x_lat, x_ctx, row, col, c, c_ctx,
            w_ada[layer], b_ada[layer], w_in[layer], q_a_norm[layer], kv_a_norm[layer],
            w_q_up[layer], w_kv_up[layer], diff_lambda[layer], diff_subln[layer],
            w_out[layer], w_ffn_in[layer], w_ffn_out[layer],
            layer_idx=layer, update_ctx=(layer < DEPTH - 1))
    return _rmsnorm(x_lat, final_norm)
```

```python
import functools
import math

import numpy as np
import jax
import jax.numpy as jnp
from jax import lax
from jax.experimental import pallas as pl
from jax.experimental.pallas import tpu as pltpu

D_MODEL = 1024
CTX_LEN = 256
GRID_W = 64
ROPE_THETA = 10000.0
NORM_EPS = 1e-6

A_HEADS = 4
A_HEAD_DIM = 64
A_WIDTH = A_HEADS * 2 * A_HEAD_DIM
B_HEADS = 8
B_NOPE = 64
B_ROPE = 32
B_VDIM = 64
B_Q_RANK = 256
B_KV_RANK = 128
B_WIDTH = B_HEADS * B_VDIM
FFN_HIDDEN = 2816

LANES = 128
TOKEN_TILE = 256
ADA_TILE = 1024
VMEM_LIMIT = 56 * 1024 * 1024

LAM_INIT = 0.8 - 0.6 * math.exp(-0.3 * 0)
LOG2E = 1.4426950408889634
A_QSCALE = A_HEAD_DIM ** -0.5 * LOG2E
B_QSCALE = (B_NOPE + B_ROPE) ** -0.5 * LOG2E

_F32 = jnp.float32
_BF16 = jnp.bfloat16


def _dot(a, b):
    return jnp.dot(a, b, preferred_element_type=_F32)


def _dot_nt(a, b):
    return lax.dot_general(a, b, (((1,), (1,)), ((), ())), preferred_element_type=_F32)


def _rms(x):
    return x * lax.rsqrt(jnp.mean(x * x, axis=-1, keepdims=True) + NORM_EPS)


def _adaln_kernel(c_ref, w_ref, b_ref, o_ref):
    c = c_ref[...]
    s = c / (1.0 + jnp.exp(-c))
    o_ref[...] = _dot(s.astype(_BF16), w_ref[...].astype(_BF16)) + b_ref[...]


def _adaln(c_all, w_ada, b_ada):
    rows = c_all.shape[0]
    n = w_ada.shape[1]
    return pl.pallas_call(
        _adaln_kernel,
        grid=(n // ADA_TILE,),
        in_specs=[
            pl.BlockSpec((rows, D_MODEL), lambda i: (0, 0)),
            pl.BlockSpec((D_MODEL, ADA_TILE), lambda i: (0, i)),
            pl.BlockSpec((1, ADA_TILE), lambda i: (0, i)),
        ],
        out_specs=pl.BlockSpec((rows, ADA_TILE), lambda i: (0, i)),
        out_shape=jax.ShapeDtypeStruct((rows, n), _F32),
        name="adaln",
    )(c_all, w_ada, b_ada)


def _rope(x, cos, s_up, s_dn, quarter):
    return (x * cos + pltpu.roll(x, quarter, 1) * s_up
            + pltpu.roll(x, LANES - quarter, 1) * s_dn)


def _proj_kernel(x_ref, ctx_ref, mod_ref, wa_ref, wb_ref, qn_ref, kvn_ref, wq_ref, wkv_ref,
                 ca_ref, ua_ref, da_ref, cb_ref, ub_ref, db_ref,
                 qa_ref, ka_ref, va_ref, qb_ref, kb_ref, vb_ref):
    j = pl.program_id(1)
    xin = jnp.where(j == 0, ctx_ref[0], x_ref[0])
    mod = mod_ref[0]
    shift = mod[:, 0:D_MODEL]
    scale = mod[:, D_MODEL:2 * D_MODEL]
    h = (_rms(xin) * (1.0 + scale) + shift).astype(_BF16)

    pa = _dot(h, wa_ref[...])
    pb = _dot(h, wb_ref[...])
    cq = pb[:, :B_Q_RANK]
    ckv = pb[:, B_Q_RANK:B_Q_RANK + B_KV_RANK]
    kr = pb[:, B_Q_RANK + B_KV_RANK:]
    qb = _dot((_rms(cq) * qn_ref[...]).astype(_BF16), wq_ref[...])
    kvb = _dot((_rms(ckv) * kvn_ref[...]).astype(_BF16), wkv_ref[...])

    ca, ua, da = ca_ref[...], ua_ref[...], da_ref[...]
    cb, ub, db = cb_ref[...], ub_ref[...], db_ref[...]
    lane = lax.broadcasted_iota(jnp.int32, (TOKEN_TILE, LANES), 1)
    ones_col = jnp.where(lane == 0, 1.0, 0.0).astype(_BF16)

    for hd in range(A_HEADS):
        sl = slice(hd * LANES, (hd + 1) * LANES)
        q = pa[:, sl]
        k = pa[:, A_WIDTH + hd * LANES:A_WIDTH + (hd + 1) * LANES]
        v = pa[:, 2 * A_WIDTH + hd * LANES:2 * A_WIDTH + (hd + 1) * LANES]
        qa_ref[0, hd] = (_rope(q, ca, ua, da, A_HEAD_DIM // 4) * A_QSCALE).astype(_BF16)
        ka_ref[0, hd] = _rope(k, ca, ua, da, A_HEAD_DIM // 4).astype(_BF16)
        va_ref[0, hd, :, :LANES] = v.astype(_BF16)
        va_ref[0, hd, :, LANES:] = ones_col

    kr_rot = _rope(kr, cb, ub, db, B_ROPE // 4)
    for hd in range(B_HEADS):
        sl = slice(hd * LANES, (hd + 1) * LANES)
        qb_ref[0, hd] = (_rope(qb[:, sl], cb, ub, db, B_ROPE // 4) * B_QSCALE).astype(_BF16)
        kb_ref[0, hd] = jnp.where(lane < B_NOPE, kvb[:, sl], kr_rot).astype(_BF16)
    for p in range(B_HEADS // 2):
        off = B_HEADS * LANES + p * LANES
        vb_ref[0, p, :, :LANES] = kvb[:, off:off + LANES].astype(_BF16)
        vb_ref[0, p, :, LANES:] = ones_col


def _proj(x, ctx, mod3, wa, wb, qn, kvn, wq, wkv, tables):
    nb, seq, _ = x.shape
    n_tok = CTX_LEN + seq
    n_tiles = n_tok // TOKEN_TILE
    t = TOKEN_TILE

    def full(a):
        return pl.BlockSpec(a.shape, lambda b, j: (0,) * a.ndim)

    tab_spec = pl.BlockSpec((t, LANES), lambda b, j: (j, 0))
    out_shapes = (
        jax.ShapeDtypeStruct((nb, A_HEADS, n_tok, LANES), _BF16),
        jax.ShapeDtypeStruct((nb, A_HEADS, n_tok, LANES), _BF16),
        jax.ShapeDtypeStruct((nb, A_HEADS, n_tok, 2 * LANES), _BF16),
        jax.ShapeDtypeStruct((nb, B_HEADS, n_tok, LANES), _BF16),
        jax.ShapeDtypeStruct((nb, B_HEADS, n_tok, LANES), _BF16),
        jax.ShapeDtypeStruct((nb, B_HEADS // 2, n_tok, 2 * LANES), _BF16),
    )

    def head_spec(nh, w):
        return pl.BlockSpec((1, nh, t, w), lambda b, j: (b, 0, j, 0))

    return pl.pallas_call(
        _proj_kernel,
        grid=(nb, n_tiles),
        in_specs=[
            pl.BlockSpec((1, t, D_MODEL), lambda b, j: (b, jnp.maximum(j - 1, 0), 0)),
            pl.BlockSpec((1, t, D_MODEL), lambda b, j: (b, 0, 0)),
            pl.BlockSpec((1, 1, 6 * D_MODEL), lambda b, j: (jnp.where(j == 0, nb, b), 0, 0)),
            full(wa), full(wb), full(qn), full(kvn), full(wq), full(wkv),
        ] + [tab_spec] * 6,
        out_specs=(head_spec(A_HEADS, LANES), head_spec(A_HEADS, LANES),
                   head_spec(A_HEADS, 2 * LANES), head_spec(B_HEADS, LANES),
                   head_spec(B_HEADS, LANES), head_spec(B_HEADS // 2, 2 * LANES)),
        out_shape=out_shapes,
        compiler_params=pltpu.CompilerParams(
            dimension_semantics=("arbitrary", "arbitrary"), vmem_limit_bytes=VMEM_LIMIT),
        name="proj",
    )(x, ctx, mod3, wa, wb, qn, kvn, wq, wkv, *tables)


def _softmax_pv(q, k, v_aug):
    s = _dot_nt(q, k)
    m = jnp.max(s, axis=-1, keepdims=True)
    p = jnp.exp2(s - m).astype(_BF16)
    o = _dot(p, v_aug)
    return o[:, :LANES] / o[:, LANES:LANES + 1]


def _attend_kernel(lam_ref, subln_ref, qa_ref, ka_ref, va_ref, qb_ref, kb_ref, vb_ref, o_ref):
    lp = lam_ref[...]
    lam = (jnp.exp(jnp.sum(lp[0:1] * lp[1:2], axis=-1, keepdims=True))
           - jnp.exp(jnp.sum(lp[2:3] * lp[3:4], axis=-1, keepdims=True)) + LAM_INIT)
    gain = subln_ref[...] * (1.0 - LAM_INIT)
    lane = lax.broadcasted_iota(jnp.int32, (TOKEN_TILE, LANES), 1)
    lo = lane < A_HEAD_DIM

    for hd in range(A_HEADS):
        q = qa_ref[0, hd]
        k = ka_ref[0, hd]
        v = va_ref[0, hd]
        zero = jnp.zeros_like(q)
        o1 = _softmax_pv(jnp.where(lo, q, zero), k, v)
        o2 = _softmax_pv(jnp.where(lo, zero, q), k, v)
        o = o1 - lam * o2
        o_ref[0, :, hd * LANES:(hd + 1) * LANES] = (_rms(o) * gain).astype(o_ref.dtype)

    for p in range(B_HEADS // 2):
        v = vb_ref[0, p]
        o0 = _softmax_pv(qb_ref[0, 2 * p], kb_ref[0, 2 * p], v)
        o1 = _softmax_pv(qb_ref[0, 2 * p + 1], kb_ref[0, 2 * p + 1], v)
        o_ref[0, :, A_WIDTH + p * LANES:A_WIDTH + (p + 1) * LANES] = (
            jnp.where(lane < B_VDIM, o0, o1).astype(o_ref.dtype))


def _attend(lam_p, subln, qa, ka, va, qb, kb, vb, seq):
    nb = qa.shape[0]
    n_tok = qa.shape[2]
    t = TOKEN_TILE
    q_off = CTX_LEN // t

    def q_spec(nh):
        return pl.BlockSpec((1, nh, t, LANES), lambda b, i: (b, 0, i + q_off, 0))

    def kv_spec(nh, w):
        return pl.BlockSpec((1, nh, n_tok, w), lambda b, i: (b, 0, 0, 0))

    return pl.pallas_call(
        _attend_kernel,
        grid=(nb, seq // t),
        in_specs=[
            pl.BlockSpec(lam_p.shape, lambda b, i: (0, 0)),
            pl.BlockSpec(subln.shape, lambda b, i: (0, 0)),
            q_spec(A_HEADS), kv_spec(A_HEADS, LANES), kv_spec(A_HEADS, 2 * LANES),
            q_spec(B_HEADS), kv_spec(B_HEADS, LANES), kv_spec(B_HEADS // 2, 2 * LANES),
        ],
        out_specs=pl.BlockSpec((1, t, A_WIDTH + B_WIDTH), lambda b, i: (b, i, 0)),
        out_shape=jax.ShapeDtypeStruct((nb, seq, A_WIDTH + B_WIDTH), _BF16),
        compiler_params=pltpu.CompilerParams(
            dimension_semantics=("arbitrary", "arbitrary"), vmem_limit_bytes=VMEM_LIMIT),
        name="attend",
    )(lam_p, subln, qa, ka, va, qb, kb, vb)


def _post_kernel(cat_ref, x_ref, mod_ref, wo_ref, wg_ref, wu_ref, wd_ref, fn_ref, o_ref):
    mod = mod_ref[0]
    g_a = mod[:, 2 * D_MODEL:3 * D_MODEL]
    sh_f = mod[:, 3 * D_MODEL:4 * D_MODEL]
    sc_f = mod[:, 4 * D_MODEL:5 * D_MODEL]
    g_f = mod[:, 5 * D_MODEL:6 * D_MODEL]
    lat = x_ref[0] + g_a * _dot(cat_ref[0], wo_ref[...])
    h = (_rms(lat) * (1.0 + sc_f) + sh_f).astype(_BF16)
    gate = _dot(h, wg_ref[...])
    up = _dot(h, wu_ref[...])
    act = (gate / (1.0 + jnp.exp(-gate)) * up).astype(_BF16)
    lat = lat + g_f * _dot(act, wd_ref[...])
    o_ref[0] = _rms(lat) * fn_ref[...]


def _post(cat, x, mod3, wo, wg, wu, wd, fn):
    nb, seq, _ = x.shape
    t = TOKEN_TILE

    def full(a):
        return pl.BlockSpec(a.shape, lambda b, i: (0,) * a.ndim)

    tok = pl.BlockSpec((1, t, D_MODEL), lambda b, i: (b, i, 0))
    return pl.pallas_call(
        _post_kernel,
        grid=(nb, seq // t),
        in_specs=[tok, tok, pl.BlockSpec((1, 1, 6 * D_MODEL), lambda b, i: (b, 0, 0)),
                  full(wo), full(wg), full(wu), full(wd), full(fn)],
        out_specs=tok,
        out_shape=jax.ShapeDtypeStruct((nb, seq, D_MODEL), _F32),
        compiler_params=pltpu.CompilerParams(
            dimension_semantics=("arbitrary", "arbitrary"), vmem_limit_bytes=VMEM_LIMIT),
        name="post",
    )(cat, x, mod3, wo, wg, wu, wd, fn)


def _rope_tables(seq):
    pos = np.arange(seq)
    row = (pos // GRID_W).astype(np.float32)
    col = (pos % GRID_W).astype(np.float32)
    lane = np.arange(LANES)

    def build(d, active, half):
        quarter = half // 2
        inv = (ROPE_THETA ** (-(np.arange(quarter, dtype=np.float32)) / quarter)).astype(np.float32)
        p = np.where((d < half)[None, :], row[:, None], col[:, None]).astype(np.float32)
        ang = (p * inv[d % quarter][None, :]).astype(np.float32)
        cos = np.where(active[None, :], np.cos(ang), 1.0)
        sin = np.where(active[None, :], np.sin(ang), 0.0)
        upper = ((d % half) >= quarter)[None, :]
        s_up = np.where(upper, sin, 0.0)
        s_dn = np.where(upper, 0.0, -sin)
        ident = np.zeros((CTX_LEN, LANES), np.float32)
        out = []
        for tab, fill in ((cos, 1.0), (s_up, 0.0), (s_dn, 0.0)):
            out.append(jnp.asarray(np.concatenate([ident + fill, tab.astype(np.float32)], 0)))
        return out

    tabs_a = build(lane % A_HEAD_DIM, np.ones(LANES, bool), A_HEAD_DIM // 2)
    in_rope = (lane >= B_NOPE) & (lane < B_NOPE + B_ROPE)
    tabs_b = build(np.where(in_rope, lane - B_NOPE, 0), in_rope, B_ROPE // 2)
    return tabs_a + tabs_b


def kernel(x, c, ctx, c_ctx, w_ada, b_ada, w_in, q_a_norm, kv_a_norm, w_q_up, w_kv_up,
           diff_lambda, diff_subln, w_out, w_ffn_in, w_ffn_out, final_norm):
    nb, seq, _ = x.shape
    assert w_ada.shape[0] == 1 and seq % TOKEN_TILE == 0 and ctx.shape[1] == CTX_LEN == TOKEN_TILE

    pad = (-(nb + 1)) % 8
    c_all = jnp.concatenate([c, c_ctx[None, :], jnp.zeros((pad, D_MODEL), _F32)], axis=0)
    mod = _adaln(c_all, w_ada[0], b_ada[0][None, :])
    mod3 = mod[:, None, :]

    w = w_in[0]
    wa = w[:, :3 * A_WIDTH].astype(_BF16)
    o_cq, o_ckv, o_kr = 3 * A_WIDTH, 3 * A_WIDTH + B_Q_RANK, 3 * A_WIDTH + B_Q_RANK + B_KV_RANK
    zeros = lambda n: jnp.zeros((D_MODEL, n), w.dtype)
    wb = jnp.concatenate([w[:, o_cq:o_ckv], w[:, o_ckv:o_kr], zeros(B_NOPE), w[:, o_kr:],
                          zeros(LANES - B_NOPE - B_ROPE)], axis=1).astype(_BF16)
    wq = jnp.pad(w_q_up[0].reshape(B_Q_RANK, B_HEADS, B_NOPE + B_ROPE),
                 ((0, 0), (0, 0), (0, LANES - B_NOPE - B_ROPE))).reshape(B_Q_RANK, B_HEADS * LANES)
    wkv3 = w_kv_up[0].reshape(B_KV_RANK, B_HEADS, B_NOPE + B_VDIM)
    wk = jnp.pad(wkv3[:, :, :B_NOPE], ((0, 0), (0, 0), (0, LANES - B_NOPE))).reshape(B_KV_RANK, B_HEADS * LANES)
    wv = wkv3[:, :, B_NOPE:].reshape(B_KV_RANK, B_WIDTH)
    wkv = jnp.concatenate([wk, wv], axis=1)

    qa, ka, va, qb, kb, vb = _proj(
        x, ctx, mod3, wa, wb, q_a_norm, kv_a_norm, wq.astype(_BF16), wkv.astype(_BF16),
        _rope_tables(seq))

    cat = _attend(diff_lambda[0], diff_subln, qa, ka, va, qb, kb, vb, seq)

    wf = w_ffn_in[0]
    return _post(cat, x, mod3, w_out[0].astype(_BF16), wf[:, :FFN_HIDDEN].astype(_BF16),
                 wf[:, FFN_HIDDEN:].astype(_BF16), w_ffn_out[0].astype(_BF16), final_norm[None, :])
```

```python
import functools
import math

import numpy as np
import jax
import jax.numpy as jnp
from jax import lax
from jax.experimental import pallas as pl
from jax.experimental.pallas import tpu as pltpu

D_MODEL = 1024
CTX_LEN = 256
GRID_W = 64
ROPE_THETA = 10000.0
NORM_EPS = 1e-6

A_HEADS = 4
A_HEAD_DIM = 64
A_WIDTH = A_HEADS * 2 * A_HEAD_DIM
B_HEADS = 8
B_NOPE = 64
B_ROPE = 32
B_VDIM = 64
B_Q_RANK = 256
B_KV_RANK = 128
B_WIDTH = B_HEADS * B_VDIM
FFN_HIDDEN = 2816

LANES = 128
TOKEN_TILE = 256
ATT_TILE = 512
ADA_TILE = 1024
VMEM_LIMIT = 56 * 1024 * 1024

LAM_INIT = 0.8 - 0.6 * math.exp(-0.3 * 0)
LOG2E = 1.4426950408889634
A_QSCALE = A_HEAD_DIM ** -0.5 * LOG2E
B_QSCALE = (B_NOPE + B_ROPE) ** -0.5 * LOG2E

_F32 = jnp.float32
_BF16 = jnp.bfloat16


def _dot(a, b):
    return jnp.dot(a, b, preferred_element_type=_F32)


def _dot_nt(a, b):
    return lax.dot_general(a, b, (((1,), (1,)), ((), ())), preferred_element_type=_F32)


def _rms(x):
    return x * lax.rsqrt(jnp.mean(x * x, axis=-1, keepdims=True) + NORM_EPS)


def _adaln_kernel(c_ref, w_ref, b_ref, o_ref):
    c = c_ref[...]
    s = c / (1.0 + jnp.exp(-c))
    o_ref[...] = _dot(s.astype(_BF16), w_ref[...].astype(_BF16)) + b_ref[...]


def _adaln(c_all, w_ada, b_ada):
    rows = c_all.shape[0]
    n = w_ada.shape[1]
    return pl.pallas_call(
        _adaln_kernel,
        grid=(n // ADA_TILE,),
        in_specs=[
            pl.BlockSpec((rows, D_MODEL), lambda i: (0, 0)),
            pl.BlockSpec((D_MODEL, ADA_TILE), lambda i: (0, i)),
            pl.BlockSpec((1, ADA_TILE), lambda i: (0, i)),
        ],
        out_specs=pl.BlockSpec((rows, ADA_TILE), lambda i: (0, i)),
        out_shape=jax.ShapeDtypeStruct((rows, n), _F32),
        name="adaln",
    )(c_all, w_ada, b_ada)


def _rope(x, cos, s_up, s_dn, quarter):
    return (x * cos + pltpu.roll(x, quarter, 1) * s_up
            + pltpu.roll(x, LANES - quarter, 1) * s_dn)


def _proj_kernel(x_ref, ctx_ref, mod_ref, wa_ref, wb_ref, qn_ref, kvn_ref, wq_ref, wkv_ref,
                 ca_ref, ua_ref, da_ref, cb_ref, ub_ref, db_ref,
                 qa_ref, ka_ref, va_ref, qb_ref, kb_ref, vb_ref):
    j = pl.program_id(1)
    xin = jnp.where(j == 0, ctx_ref[0], x_ref[0])
    mod = mod_ref[0]
    shift = mod[:, 0:D_MODEL]
    scale = mod[:, D_MODEL:2 * D_MODEL]
    h = (_rms(xin) * (1.0 + scale) + shift).astype(_BF16)

    pa = _dot(h, wa_ref[...])
    pb = _dot(h, wb_ref[...])
    cq = pb[:, :B_Q_RANK]
    ckv = pb[:, B_Q_RANK:B_Q_RANK + B_KV_RANK]
    kr = pb[:, B_Q_RANK + B_KV_RANK:]
    qb = _dot((_rms(cq) * qn_ref[...]).astype(_BF16), wq_ref[...])
    kvb = _dot((_rms(ckv) * kvn_ref[...]).astype(_BF16), wkv_ref[...])

    ca, ua, da = ca_ref[...], ua_ref[...], da_ref[...]
    cb, ub, db = cb_ref[...], ub_ref[...], db_ref[...]
    lane = lax.broadcasted_iota(jnp.int32, (TOKEN_TILE, LANES), 1)
    ones_col = jnp.where(lane == 0, 1.0, 0.0).astype(_BF16)

    for hd in range(A_HEADS):
        sl = slice(hd * LANES, (hd + 1) * LANES)
        q = pa[:, sl]
        k = pa[:, A_WIDTH + hd * LANES:A_WIDTH + (hd + 1) * LANES]
        v = pa[:, 2 * A_WIDTH + hd * LANES:2 * A_WIDTH + (hd + 1) * LANES]
        qa_ref[0, hd] = (_rope(q, ca, ua, da, A_HEAD_DIM // 4) * A_QSCALE).astype(_BF16)
        ka_ref[0, hd] = _rope(k, ca, ua, da, A_HEAD_DIM // 4).astype(_BF16)
        va_ref[0, hd, :, :LANES] = v.astype(_BF16)
        va_ref[0, hd, :, LANES:] = ones_col

    kr_rot = _rope(kr, cb, ub, db, B_ROPE // 4)
    for hd in range(B_HEADS):
        sl = slice(hd * LANES, (hd + 1) * LANES)
        qb_ref[0, hd] = (_rope(qb[:, sl], cb, ub, db, B_ROPE // 4) * B_QSCALE).astype(_BF16)
        kb_ref[0, hd] = jnp.where(lane < B_NOPE, kvb[:, sl], kr_rot).astype(_BF16)
    for p in range(B_HEADS // 2):
        off = B_HEADS * LANES + p * LANES
        vb_ref[0, p, :, :LANES] = kvb[:, off:off + LANES].astype(_BF16)
        vb_ref[0, p, :, LANES:] = ones_col


def _proj(x, ctx, mod3, wa, wb, qn, kvn, wq, wkv, tables):
    nb, seq, _ = x.shape
    n_tok = CTX_LEN + seq
    n_tiles = n_tok // TOKEN_TILE
    t = TOKEN_TILE

    def full(a):
        return pl.BlockSpec(a.shape, lambda b, j: (0,) * a.ndim)

    tab_spec = pl.BlockSpec((t, LANES), lambda b, j: (j, 0))
    out_shapes = (
        jax.ShapeDtypeStruct((nb, A_HEADS, seq, LANES), _BF16),
        jax.ShapeDtypeStruct((nb, A_HEADS, n_tok, LANES), _BF16),
        jax.ShapeDtypeStruct((nb, A_HEADS, n_tok, 2 * LANES), _BF16),
        jax.ShapeDtypeStruct((nb, B_HEADS, seq, LANES), _BF16),
        jax.ShapeDtypeStruct((nb, B_HEADS, n_tok, LANES), _BF16),
        jax.ShapeDtypeStruct((nb, B_HEADS // 2, n_tok, 2 * LANES), _BF16),
    )

    def head_spec(nh, w):
        return pl.BlockSpec((1, nh, t, w), lambda b, j: (b, 0, j, 0))

    def q_spec(nh):
        return pl.BlockSpec((1, nh, t, LANES), lambda b, j: (b, 0, jnp.maximum(j - 1, 0), 0))

    return pl.pallas_call(
        _proj_kernel,
        grid=(nb, n_tiles),
        in_specs=[
            pl.BlockSpec((1, t, D_MODEL), lambda b, j: (b, jnp.maximum(j - 1, 0), 0)),
            pl.BlockSpec((1, t, D_MODEL), lambda b, j: (b, 0, 0)),
            pl.BlockSpec((1, 1, 6 * D_MODEL), lambda b, j: (jnp.where(j == 0, nb, b), 0, 0)),
            full(wa), full(wb), full(qn), full(kvn), full(wq), full(wkv),
        ] + [tab_spec] * 6,
        out_specs=(q_spec(A_HEADS), head_spec(A_HEADS, LANES),
                   head_spec(A_HEADS, 2 * LANES), q_spec(B_HEADS),
                   head_spec(B_HEADS, LANES), head_spec(B_HEADS // 2, 2 * LANES)),
        out_shape=out_shapes,
        compiler_params=pltpu.CompilerParams(
            dimension_semantics=("arbitrary", "arbitrary"), vmem_limit_bytes=VMEM_LIMIT),
        name="proj",
    )(x, ctx, mod3, wa, wb, qn, kvn, wq, wkv, *tables)


def _softmax_pv(q, k, v_aug):
    s = _dot_nt(q, k)
    m = jnp.max(s, axis=-1, keepdims=True)
    p = jnp.exp2(s - m).astype(_BF16)
    o = _dot(p, v_aug)
    return o[:, :LANES] / o[:, LANES:LANES + 1]


def _attend_kernel(lam_ref, subln_ref, qa_ref, ka_ref, va_ref, qb_ref, kb_ref, vb_ref, o_ref):
    lp = lam_ref[...]
    lam = (jnp.exp(jnp.sum(lp[0:1] * lp[1:2], axis=-1, keepdims=True))
           - jnp.exp(jnp.sum(lp[2:3] * lp[3:4], axis=-1, keepdims=True)) + LAM_INIT)
    gain = subln_ref[...] * (1.0 - LAM_INIT)
    lane = lax.broadcasted_iota(jnp.int32, (ATT_TILE, LANES), 1)
    lo = lane < A_HEAD_DIM

    for hd in range(A_HEADS):
        q = qa_ref[0, hd]
        k = ka_ref[0, hd]
        v = va_ref[0, hd]
        zero = jnp.zeros_like(q)
        o1 = _softmax_pv(jnp.where(lo, q, zero), k, v)
        o2 = _softmax_pv(jnp.where(lo, zero, q), k, v)
        o = o1 - lam * o2
        o_ref[0, :, hd * LANES:(hd + 1) * LANES] = (_rms(o) * gain).astype(o_ref.dtype)

    for p in range(B_HEADS // 2):
        v = vb_ref[0, p]
        o0 = _softmax_pv(qb_ref[0, 2 * p], kb_ref[0, 2 * p], v)
        o1 = _softmax_pv(qb_ref[0, 2 * p + 1], kb_ref[0, 2 * p + 1], v)
        o_ref[0, :, A_WIDTH + p * LANES:A_WIDTH + (p + 1) * LANES] = (
            jnp.where(lane < B_VDIM, o0, o1).astype(o_ref.dtype))


def _attend(lam_p, subln, qa, ka, va, qb, kb, vb, seq):
    nb = qa.shape[0]
    n_tok = ka.shape[2]
    t = ATT_TILE

    def q_spec(nh):
        return pl.BlockSpec((1, nh, t, LANES), lambda b, i: (b, 0, i, 0))

    def kv_spec(nh, w):
        return pl.BlockSpec((1, nh, n_tok, w), lambda b, i: (b, 0, 0, 0))

    return pl.pallas_call(
        _attend_kernel,
        grid=(nb, seq // t),
        in_specs=[
            pl.BlockSpec(lam_p.shape, lambda b, i: (0, 0)),
            pl.BlockSpec(subln.shape, lambda b, i: (0, 0)),
            q_spec(A_HEADS), kv_spec(A_HEADS, LANES), kv_spec(A_HEADS, 2 * LANES),
            q_spec(B_HEADS), kv_spec(B_HEADS, LANES), kv_spec(B_HEADS // 2, 2 * LANES),
        ],
        out_specs=pl.BlockSpec((1, t, A_WIDTH + B_WIDTH), lambda b, i: (b, i, 0)),
        out_shape=jax.ShapeDtypeStruct((nb, seq, A_WIDTH + B_WIDTH), _BF16),
        compiler_params=pltpu.CompilerParams(
            dimension_semantics=("arbitrary", "arbitrary"), vmem_limit_bytes=VMEM_LIMIT),
        name="attend",
    )(lam_p, subln, qa, ka, va, qb, kb, vb)


def _post_kernel(cat_ref, x_ref, mod_ref, wo_ref, wg_ref, wu_ref, wd_ref, fn_ref, o_ref):
    mod = mod_ref[0]
    g_a = mod[:, 2 * D_MODEL:3 * D_MODEL]
    sh_f = mod[:, 3 * D_MODEL:4 * D_MODEL]
    sc_f = mod[:, 4 * D_MODEL:5 * D_MODEL]
    g_f = mod[:, 5 * D_MODEL:6 * D_MODEL]
    lat = x_ref[0] + g_a * _dot(cat_ref[0], wo_ref[...])
    h = (_rms(lat) * (1.0 + sc_f) + sh_f).astype(_BF16)
    gate = _dot(h, wg_ref[...])
    up = _dot(h, wu_ref[...])
    act = (gate / (1.0 + jnp.exp(-gate)) * up).astype(_BF16)
    lat = lat + g_f * _dot(act, wd_ref[...])
    o_ref[0] = _rms(lat) * fn_ref[...]


def _post(cat, x, mod3, wo, wg, wu, wd, fn):
    nb, seq, _ = x.shape
    t = TOKEN_TILE

    def full(a):
        return pl.BlockSpec(a.shape, lambda b, i: (0,) * a.ndim)

    tok = pl.BlockSpec((1, t, D_MODEL), lambda b, i: (b, i, 0))
    return pl.pallas_call(
        _post_kernel,
        grid=(nb, seq // t),
        in_specs=[tok, tok, pl.BlockSpec((1, 1, 6 * D_MODEL), lambda b, i: (b, 0, 0)),
                  full(wo), full(wg), full(wu), full(wd), full(fn)],
        out_specs=tok,
        out_shape=jax.ShapeDtypeStruct((nb, seq, D_MODEL), _F32),
        compiler_params=pltpu.CompilerParams(
            dimension_semantics=("arbitrary", "arbitrary"), vmem_limit_bytes=VMEM_LIMIT),
        name="post",
    )(cat, x, mod3, wo, wg, wu, wd, fn)


def _rope_tables(seq):
    pos = np.arange(seq)
    row = (pos // GRID_W).astype(np.float32)
    col = (pos % GRID_W).astype(np.float32)
    lane = np.arange(LANES)

    def build(d, active, half):
        quarter = half // 2
        inv = (ROPE_THETA ** (-(np.arange(quarter, dtype=np.float32)) / quarter)).astype(np.float32)
        p = np.where((d < half)[None, :], row[:, None], col[:, None]).astype(np.float32)
        ang = (p * inv[d % quarter][None, :]).astype(np.float32)
        cos = np.where(active[None, :], np.cos(ang), 1.0)
        sin = np.where(active[None, :], np.sin(ang), 0.0)
        upper = ((d % half) >= quarter)[None, :]
        s_up = np.where(upper, sin, 0.0)
        s_dn = np.where(upper, 0.0, -sin)
        ident = np.zeros((CTX_LEN, LANES), np.float32)
        out = []
        for tab, fill in ((cos, 1.0), (s_up, 0.0), (s_dn, 0.0)):
            out.append(jnp.asarray(np.concatenate([ident + fill, tab.astype(np.float32)], 0)))
        return out

    tabs_a = build(lane % A_HEAD_DIM, np.ones(LANES, bool), A_HEAD_DIM // 2)
    in_rope = (lane >= B_NOPE) & (lane < B_NOPE + B_ROPE)
    tabs_b = build(np.where(in_rope, lane - B_NOPE, 0), in_rope, B_ROPE // 2)
    return tabs_a + tabs_b


def kernel(x, c, ctx, c_ctx, w_ada, b_ada, w_in, q_a_norm, kv_a_norm, w_q_up, w_kv_up,
           diff_lambda, diff_subln, w_out, w_ffn_in, w_ffn_out, final_norm):
    nb, seq, _ = x.shape
    assert w_ada.shape[0] == 1 and seq % ATT_TILE == 0 and ctx.shape[1] == CTX_LEN == TOKEN_TILE

    pad = (-(nb + 1)) % 8
    c_all = jnp.concatenate([c, c_ctx[None, :], jnp.zeros((pad, D_MODEL), _F32)], axis=0)
    mod = _adaln(c_all, w_ada[0], b_ada[0][None, :])
    mod3 = mod[:, None, :]

    w = w_in[0]
    wa = w[:, :3 * A_WIDTH].astype(_BF16)
    o_cq, o_ckv, o_kr = 3 * A_WIDTH, 3 * A_WIDTH + B_Q_RANK, 3 * A_WIDTH + B_Q_RANK + B_KV_RANK
    zeros = lambda n: jnp.zeros((D_MODEL, n), w.dtype)
    wb = jnp.concatenate([w[:, o_cq:o_ckv], w[:, o_ckv:o_kr], zeros(B_NOPE), w[:, o_kr:],
                          zeros(LANES - B_NOPE - B_ROPE)], axis=1).astype(_BF16)
    wq = jnp.pad(w_q_up[0].reshape(B_Q_RANK, B_HEADS, B_NOPE + B_ROPE),
                 ((0, 0), (0, 0), (0, LANES - B_NOPE - B_ROPE))).reshape(B_Q_RANK, B_HEADS * LANES)
    wkv3 = w_kv_up[0].reshape(B_KV_RANK, B_HEADS, B_NOPE + B_VDIM)
    wk = jnp.pad(wkv3[:, :, :B_NOPE], ((0, 0), (0, 0), (0, LANES - B_NOPE))).reshape(B_KV_RANK, B_HEADS * LANES)
    wv = wkv3[:, :, B_NOPE:].reshape(B_KV_RANK, B_WIDTH)
    wkv = jnp.concatenate([wk, wv], axis=1)

    qa, ka, va, qb, kb, vb = _proj(
        x, ctx, mod3, wa, wb, q_a_norm, kv_a_norm, wq.astype(_BF16), wkv.astype(_BF16),
        _rope_tables(seq))

    cat = _attend(diff_lambda[0], diff_subln, qa, ka, va, qb, kb, vb, seq)

    wf = w_ffn_in[0]
    return _post(cat, x, mod3, w_out[0].astype(_BF16), wf[:, :FFN_HIDDEN].astype(_BF16),
                 wf[:, FFN_HIDDEN:].astype(_BF16), w_ffn_out[0].astype(_BF16), final_norm[None, :])
```

```python
import math

import numpy as np
import jax
import jax.numpy as jnp
from jax import lax
from jax.experimental import pallas as pl
from jax.experimental.pallas import tpu as pltpu

D_MODEL = 1024
CTX_LEN = 256
GRID_W = 64
ROPE_THETA = 10000.0
NORM_EPS = 1e-6

A_HEADS = 4
A_HEAD_DIM = 64
A_WIDTH = A_HEADS * 2 * A_HEAD_DIM
B_HEADS = 8
B_NOPE = 64
B_ROPE = 32
B_VDIM = 64
B_Q_RANK = 256
B_KV_RANK = 128
B_WIDTH = B_HEADS * B_VDIM
FFN_HIDDEN = 2816

LANES = 128
SUBLANES = 8
TOKEN_TILE = 256
ADA_TILE = 1024
V_PAD = 16
N_MAPS = 2 * A_HEADS + B_HEADS
VMEM_LIMIT = 56 * 1024 * 1024

LAM_INIT = 0.8 - 0.6 * math.exp(-0.3 * 0)
LOG2E = 1.4426950408889634
A_QSCALE = A_HEAD_DIM ** -0.5 * LOG2E
B_QSCALE = (B_NOPE + B_ROPE) ** -0.5 * LOG2E

FAST_LIMIT = 50.0
BOUND_MARGIN = 1.02
FAST_DEPTH = 1
EXACT_DEPTH = 2

_F32 = jnp.float32
_BF16 = jnp.bfloat16


def _dot(a, b):
    return jnp.dot(a, b, preferred_element_type=_F32)


def _dot_nt(a, b):
    return lax.dot_general(a, b, (((1,), (1,)), ((), ())), preferred_element_type=_F32)


def _rms(x, axis=-1):
    return x * lax.rsqrt(jnp.mean(x * x, axis=axis, keepdims=True) + NORM_EPS)


def _adaln_kernel(c_ref, w_ref, b_ref, o_ref):
    c = c_ref[...]
    s = c / (1.0 + jnp.exp(-c))
    o_ref[...] = _dot(s.astype(_BF16), w_ref[...].astype(_BF16)) + b_ref[...]


def _adaln(c_all, w_ada, b_ada):
    rows = c_all.shape[0]
    n = w_ada.shape[1]
    return pl.pallas_call(
        _adaln_kernel,
        grid=(n // ADA_TILE,),
        in_specs=[
            pl.BlockSpec((rows, D_MODEL), lambda i: (0, 0)),
            pl.BlockSpec((D_MODEL, ADA_TILE), lambda i: (0, i)),
            pl.BlockSpec((1, ADA_TILE), lambda i: (0, i)),
        ],
        out_specs=pl.BlockSpec((rows, ADA_TILE), lambda i: (0, i)),
        out_shape=jax.ShapeDtypeStruct((rows, n), _F32),
        name="adaln",
    )(c_all, w_ada, b_ada)


def _rope(x, cos, s_up, s_dn, quarter):
    return (x * cos + pltpu.roll(x, quarter, 1) * s_up
            + pltpu.roll(x, LANES - quarter, 1) * s_dn)


def _rope_t(x, cos_t, sin_t, lo, hi, quarter):
    parts = [x[:lo]] if lo else []
    for r in range(lo, hi, 2 * quarter):
        parts += [x[r + quarter:r + 2 * quarter], x[r:r + quarter]]
    if hi < x.shape[0]:
        parts.append(x[hi:])
    return x * cos_t + jnp.concatenate(parts, axis=0) * sin_t


def _proj_kernel(x_ref, ctx_ref, mod_ref, wc_ref, wn_ref, wqv_ref, qn_ref, kvn_col_ref, kvn_row_ref,
                 wqt_ref, wvt_ref, wk_ref, sel_ref,
                 ca_ref, ua_ref, da_ref, cb_ref, ub_ref, db_ref,
                 cat_ref, sat_ref, cbt_ref, sbt_ref,
                 qa_ref, ka_ref, va_ref, qb_ref, kb_ref, vb_ref, kn2_ref):
    t = TOKEN_TILE
    j = pl.program_id(1)
    xin = jnp.where(j == 0, ctx_ref[0], x_ref[0])
    mod = mod_ref[0]
    shift = mod[:, 0:D_MODEL]
    scale = mod[:, D_MODEL:2 * D_MODEL]
    h32 = _rms(xin) * (1.0 + scale) + shift
    h = h32.astype(_BF16)
    h_t = h32.T.astype(_BF16)

    yc_t = _dot(wc_ref[...], h_t)
    y = _dot(h, wn_ref[...])
    yqv_t = _dot(wqv_ref[...], h_t)
    cq_t = yc_t[:B_Q_RANK]
    ckv_t = yc_t[B_Q_RANK:]
    ckv = y[:, A_WIDTH:A_WIDTH + B_KV_RANK]
    kr = y[:, A_WIDTH + B_KV_RANK:]
    qb_t = _dot(wqt_ref[...], (_rms(cq_t, 0) * qn_ref[...]).astype(_BF16))
    vb_t = _dot(wvt_ref[...], (_rms(ckv_t, 0) * kvn_col_ref[...]).astype(_BF16))
    kn = _dot((_rms(ckv) * kvn_row_ref[...]).astype(_BF16), wk_ref[...])

    row16 = lax.broadcasted_iota(jnp.int32, (V_PAD, t), 0)
    ones_rows = jnp.where(row16 == 0, 1.0, 0.0).astype(_BF16)
    cat, sat = cat_ref[...], sat_ref[...]
    for hd in range(A_HEADS):
        q_t = yqv_t[hd * LANES:(hd + 1) * LANES]
        qa_ref[0, hd] = (_rope_t(q_t, cat, sat, 0, LANES, A_HEAD_DIM // 4) * A_QSCALE).astype(_BF16)
        va_ref[0, hd, :LANES, :] = yqv_t[A_WIDTH + hd * LANES:A_WIDTH + (hd + 1) * LANES].astype(_BF16)
        va_ref[0, hd, LANES:, :] = ones_rows
    cbt, sbt = cbt_ref[...], sbt_ref[...]
    for hd in range(B_HEADS):
        blk = qb_t[hd * LANES:(hd + 1) * LANES]
        qb_ref[0, hd] = (_rope_t(blk, cbt, sbt, B_NOPE, B_NOPE + B_ROPE, B_ROPE // 4)
                         * B_QSCALE).astype(_BF16)
        vb_ref[0, hd, :B_VDIM, :] = vb_t[hd * B_VDIM:(hd + 1) * B_VDIM].astype(_BF16)
        vb_ref[0, hd, B_VDIM:, :] = ones_rows

    ca, ua, da = ca_ref[...], ua_ref[...], da_ref[...]
    cb, ub, db = cb_ref[...], ub_ref[...], db_ref[...]
    lane = lax.broadcasted_iota(jnp.int32, (t, LANES), 1)
    ksq = []

    def squares(k_bf16):
        kf = k_bf16.astype(_F32)
        return (kf * kf).astype(_BF16)

    for hd in range(A_HEADS):
        k = _rope(y[:, hd * LANES:(hd + 1) * LANES], ca, ua, da, A_HEAD_DIM // 4).astype(_BF16)
        ka_ref[0, hd] = k
        ksq.append(squares(k))
    kr_rot = _rope(kr, cb, ub, db, B_ROPE // 4)
    for hd in range(B_HEADS):
        k = jnp.where(lane < B_NOPE, kn[:, hd * LANES:(hd + 1) * LANES], kr_rot).astype(_BF16)
        kb_ref[0, hd] = k
        ksq.append(squares(k))
    kn2_ref[0] = _dot_nt(sel_ref[...], jnp.concatenate(ksq, axis=1))


def _proj(x, ctx, mod3, weights, tables_nat, tables_t):
    nb, seq, _ = x.shape
    n_tok = CTX_LEN + seq
    n_tiles = n_tok // TOKEN_TILE
    t = TOKEN_TILE

    def full(a):
        return pl.BlockSpec(a.shape, lambda b, j: (0,) * a.ndim)

    lat = lambda j: jnp.maximum(j - 1, 0)
    out_shapes = (
        jax.ShapeDtypeStruct((nb, A_HEADS, LANES, seq), _BF16),
        jax.ShapeDtypeStruct((nb, A_HEADS, n_tok, LANES), _BF16),
        jax.ShapeDtypeStruct((nb, A_HEADS, LANES + V_PAD, n_tok), _BF16),
        jax.ShapeDtypeStruct((nb, B_HEADS, LANES, seq), _BF16),
        jax.ShapeDtypeStruct((nb, B_HEADS, n_tok, LANES), _BF16),
        jax.ShapeDtypeStruct((nb, B_HEADS, B_VDIM + V_PAD, n_tok), _BF16),
        jax.ShapeDtypeStruct((nb, N_MAPS, n_tok), _F32),
    )
    out_specs = (
        pl.BlockSpec((1, A_HEADS, LANES, t), lambda b, j: (b, 0, 0, lat(j))),
        pl.BlockSpec((1, A_HEADS, t, LANES), lambda b, j: (b, 0, j, 0)),
        pl.BlockSpec((1, A_HEADS, LANES + V_PAD, t), lambda b, j: (b, 0, 0, j)),
        pl.BlockSpec((1, B_HEADS, LANES, t), lambda b, j: (b, 0, 0, lat(j))),
        pl.BlockSpec((1, B_HEADS, t, LANES), lambda b, j: (b, 0, j, 0)),
        pl.BlockSpec((1, B_HEADS, B_VDIM + V_PAD, t), lambda b, j: (b, 0, 0, j)),
        pl.BlockSpec((1, N_MAPS, t), lambda b, j: (b, 0, j)),
    )
    return pl.pallas_call(
        _proj_kernel,
        grid=(nb, n_tiles),
        in_specs=[
            pl.BlockSpec((1, t, D_MODEL), lambda b, j: (b, lat(j), 0)),
            pl.BlockSpec((1, t, D_MODEL), lambda b, j: (b, 0, 0)),
            pl.BlockSpec((1, 1, 6 * D_MODEL), lambda b, j: (jnp.where(j == 0, nb, b), 0, 0)),
        ] + [full(w) for w in weights]
          + [pl.BlockSpec((t, LANES), lambda b, j: (j, 0))] * len(tables_nat)
          + [pl.BlockSpec((LANES, t), lambda b, j: (0, lat(j)))] * len(tables_t),
        out_specs=out_specs,
        out_shape=out_shapes,
        compiler_params=pltpu.CompilerParams(
            dimension_semantics=("arbitrary", "arbitrary"), vmem_limit_bytes=VMEM_LIMIT),
        name="proj",
    )(x, ctx, mod3, *weights, *tables_nat, *tables_t)


def _attend_kernel(lam_ref, subln_ref, qa_ref, ka_ref, va_ref, qb_ref, kb_ref, vb_ref,
                   kn2_ref, o_ref):
    t = TOKEN_TILE
    lp = lam_ref[...]
    lam = (jnp.exp(jnp.sum(lp[0:1] * lp[1:2], axis=-1, keepdims=True))
           - jnp.exp(jnp.sum(lp[2:3] * lp[3:4], axis=-1, keepdims=True)) + LAM_INIT)
    gain = subln_ref[...] * (1.0 - LAM_INIT)
    row = lax.broadcasted_iota(jnp.int32, (LANES, t), 0)
    lo = row < A_HEAD_DIM

    def bound(q_m, kn2):
        qf = q_m.astype(_F32)
        return (jnp.sqrt(jnp.sum(qf * qf, axis=0, keepdims=True))
                * jnp.sqrt(jnp.max(kn2, axis=-1, keepdims=True)) * BOUND_MARGIN)

    maps = []
    for hd in range(A_HEADS):
        q_t = qa_ref[0, hd]
        zero = jnp.zeros_like(q_t)
        for half in range(2):
            q_m = jnp.where(lo, q_t, zero) if half == 0 else jnp.where(lo, zero, q_t)
            maps.append((q_m, ka_ref.at[0, hd], va_ref.at[0, hd], LANES,
                         bound(q_m, kn2_ref[0, 2 * hd + half:2 * hd + half + 1, :])))
    for hd in range(B_HEADS):
        q_t = qb_ref[0, hd]
        maps.append((q_t, kb_ref.at[0, hd], vb_ref.at[0, hd], B_VDIM,
                     bound(q_t, kn2_ref[0, 2 * A_HEADS + hd:2 * A_HEADS + hd + 1, :])))

    worst = maps[0][4]
    for mp in maps[1:]:
        worst = jnp.maximum(worst, mp[4])
    fast_ok = jnp.max(worst) <= FAST_LIMIT

    def finish(i, o_t, acc):
        if i < 2 * A_HEADS:
            hd, half = divmod(i, 2)
            if half == 0:
                acc[hd] = o_t
                return
            o = acc.pop(hd) - lam * o_t
            o = _rms(o, 0) * gain
            o_ref[0, :, hd * LANES:(hd + 1) * LANES] = o.T.astype(o_ref.dtype)
        else:
            hd = i - 2 * A_HEADS
            if hd % 2 == 0:
                acc[hd] = o_t
                return
            o = jnp.concatenate([acc.pop(hd - 1), o_t], axis=0)
            p = hd // 2
            o_ref[0, :, A_WIDTH + p * LANES:A_WIDTH + (p + 1) * LANES] = o.T.astype(o_ref.dtype)

    def run(use_bound, depth):
        acc = {}

        def softmax_pv(i, s_t):
            _, _, v_ref, dv, m_ub = maps[i]
            m = m_ub if use_bound else jnp.max(s_t, axis=0, keepdims=True)
            p_t = jnp.exp2(s_t - m).astype(_BF16)
            o = _dot(v_ref[...], p_t)
            finish(i, o[:dv] / o[dv:dv + 1], acc)

        pending = []
        for i, (q_m, k_ref, _, _, _) in enumerate(maps):
            pending.append((i, _dot(k_ref[...], q_m)))
            if len(pending) > depth:
                softmax_pv(*pending.pop(0))
        for item in pending:
            softmax_pv(*item)

    @pl.when(fast_ok)
    def _():
        run(True, FAST_DEPTH)

    @pl.when(jnp.logical_not(fast_ok))
    def _():
        run(False, EXACT_DEPTH)


def _attend(lam_p, subln_col, qa, ka, va, qb, kb, vb, kn2):
    nb, _, _, seq = qa.shape
    t = TOKEN_TILE

    def q_spec(nh):
        return pl.BlockSpec((1, nh, LANES, t), lambda b, i: (b, 0, 0, i))

    def kv_spec(a):
        return pl.BlockSpec((1,) + a.shape[1:], lambda b, i: (b,) + (0,) * (a.ndim - 1))

    return pl.pallas_call(
        _attend_kernel,
        grid=(nb, seq // t),
        in_specs=[
            pl.BlockSpec(lam_p.shape, lambda b, i: (0, 0)),
            pl.BlockSpec(subln_col.shape, lambda b, i: (0, 0)),
            q_spec(A_HEADS), kv_spec(ka), kv_spec(va),
            q_spec(B_HEADS), kv_spec(kb), kv_spec(vb), kv_spec(kn2),
        ],
        out_specs=pl.BlockSpec((1, t, A_WIDTH + B_WIDTH), lambda b, i: (b, i, 0)),
        out_shape=jax.ShapeDtypeStruct((nb, seq, A_WIDTH + B_WIDTH), _BF16),
        compiler_params=pltpu.CompilerParams(
            dimension_semantics=("arbitrary", "arbitrary"), vmem_limit_bytes=VMEM_LIMIT),
        name="attend",
    )(lam_p, subln_col, qa, ka, va, qb, kb, vb, kn2)


def _post_kernel(cat_ref, x_ref, mod_ref, wo_ref, wg_ref, wu_ref, wd_ref, fn_ref, o_ref):
    mod = mod_ref[0]
    g_a = mod[:, 2 * D_MODEL:3 * D_MODEL]
    sh_f = mod[:, 3 * D_MODEL:4 * D_MODEL]
    sc_f = mod[:, 4 * D_MODEL:5 * D_MODEL]
    g_f = mod[:, 5 * D_MODEL:6 * D_MODEL]
    lat = x_ref[0] + g_a * _dot(cat_ref[0], wo_ref[...])
    h = (_rms(lat) * (1.0 + sc_f) + sh_f).astype(_BF16)
    gate = _dot(h, wg_ref[...])
    up = _dot(h, wu_ref[...])
    act = (gate / (1.0 + jnp.exp(-gate)) * up).astype(_BF16)
    lat = lat + g_f * _dot(act, wd_ref[...])
    o_ref[0] = _rms(lat) * fn_ref[...]


def _post(cat, x, mod3, wo, wg, wu, wd, fn):
    nb, seq, _ = x.shape
    t = TOKEN_TILE

    def full(a):
        return pl.BlockSpec(a.shape, lambda b, i: (0,) * a.ndim)

    tok = pl.BlockSpec((1, t, D_MODEL), lambda b, i: (b, i, 0))
    return pl.pallas_call(
        _post_kernel,
        grid=(nb, seq // t),
        in_specs=[tok, tok, pl.BlockSpec((1, 1, 6 * D_MODEL), lambda b, i: (b, 0, 0)),
                  full(wo), full(wg), full(wu), full(wd), full(fn)],
        out_specs=tok,
        out_shape=jax.ShapeDtypeStruct((nb, seq, D_MODEL), _F32),
        compiler_params=pltpu.CompilerParams(
            dimension_semantics=("arbitrary", "arbitrary"), vmem_limit_bytes=VMEM_LIMIT),
        name="post",
    )(cat, x, mod3, wo, wg, wu, wd, fn)


def _rope_tables(seq):
    pos = np.arange(seq)
    row = (pos // GRID_W).astype(np.float32)
    col = (pos % GRID_W).astype(np.float32)
    lane = np.arange(LANES)

    def build(d, active, half):
        quarter = half // 2
        inv = (ROPE_THETA ** (-(np.arange(quarter, dtype=np.float32)) / quarter)).astype(np.float32)
        p = np.where((d < half)[None, :], row[:, None], col[:, None]).astype(np.float32)
        ang = (p * inv[d % quarter][None, :]).astype(np.float32)
        cos = np.where(active[None, :], np.cos(ang), 1.0).astype(np.float32)
        sin = np.where(active[None, :], np.sin(ang), 0.0).astype(np.float32)
        upper = ((d % half) >= quarter)[None, :]
        s_up = np.where(upper, sin, 0.0).astype(np.float32)
        s_dn = np.where(upper, 0.0, -sin).astype(np.float32)
        ident = np.zeros((CTX_LEN, LANES), np.float32)
        nat = [jnp.asarray(np.concatenate([ident + fill, tab], 0))
               for tab, fill in ((cos, 1.0), (s_up, 0.0), (s_dn, 0.0))]
        trans = [jnp.asarray(np.ascontiguousarray(cos.T)),
                 jnp.asarray(np.ascontiguousarray((s_up + s_dn).T))]
        return nat, trans

    nat_a, trans_a = build(lane % A_HEAD_DIM, np.ones(LANES, bool), A_HEAD_DIM // 2)
    in_rope = (lane >= B_NOPE) & (lane < B_NOPE + B_ROPE)
    nat_b, trans_b = build(np.where(in_rope, lane - B_NOPE, 0), in_rope, B_ROPE // 2)
    return nat_a + nat_b, trans_a + trans_b


def kernel(x, c, ctx, c_ctx, w_ada, b_ada, w_in, q_a_norm, kv_a_norm, w_q_up, w_kv_up,
           diff_lambda, diff_subln, w_out, w_ffn_in, w_ffn_out, final_norm):
    nb, seq, _ = x.shape
    assert w_ada.shape[0] == 1 and seq % TOKEN_TILE == 0 and ctx.shape[1] == CTX_LEN == TOKEN_TILE

    pad = (-(nb + 1)) % SUBLANES
    c_all = jnp.concatenate([c, c_ctx[None, :], jnp.zeros((pad, D_MODEL), _F32)], axis=0)
    mod = _adaln(c_all, w_ada[0], b_ada[0][None, :])
    mod3 = mod[:, None, :]

    w = w_in[0]
    o_k, o_v, o_cq = A_WIDTH, 2 * A_WIDTH, 3 * A_WIDTH
    o_ckv, o_kr = o_cq + B_Q_RANK, o_cq + B_Q_RANK + B_KV_RANK
    zeros = lambda n: jnp.zeros((D_MODEL, n), w.dtype)
    w_c = w[:, o_cq:o_kr].T
    w_qv = jnp.concatenate([w[:, :o_k], w[:, o_v:o_cq]], axis=1).T
    w_n = jnp.concatenate([w[:, o_k:o_v], w[:, o_ckv:o_kr], zeros(B_NOPE), w[:, o_kr:],
                           zeros(LANES - B_NOPE - B_ROPE)], axis=1)
    wq = jnp.pad(w_q_up[0].reshape(B_Q_RANK, B_HEADS, B_NOPE + B_ROPE),
                 ((0, 0), (0, 0), (0, LANES - B_NOPE - B_ROPE))).reshape(B_Q_RANK, B_HEADS * LANES)
    wkv3 = w_kv_up[0].reshape(B_KV_RANK, B_HEADS, B_NOPE + B_VDIM)
    wk = jnp.pad(wkv3[:, :, :B_NOPE], ((0, 0), (0, 0), (0, LANES - B_NOPE))).reshape(B_KV_RANK, B_HEADS * LANES)
    wv = wkv3[:, :, B_NOPE:].reshape(B_KV_RANK, B_WIDTH)
    sel = np.zeros((N_MAPS, (A_HEADS + B_HEADS) * LANES), np.float32)
    for hd in range(A_HEADS):
        sel[2 * hd, hd * LANES:hd * LANES + A_HEAD_DIM] = 1.0
        sel[2 * hd + 1, hd * LANES + A_HEAD_DIM:(hd + 1) * LANES] = 1.0
    for hd in range(B_HEADS):
        sel[2 * A_HEADS + hd, (A_HEADS + hd) * LANES:(A_HEADS + hd + 1) * LANES] = 1.0
    weights = (w_c.astype(_BF16), w_n.astype(_BF16), w_qv.astype(_BF16),
               q_a_norm.reshape(B_Q_RANK, 1), kv_a_norm.reshape(B_KV_RANK, 1),
               kv_a_norm.reshape(1, B_KV_RANK),
               wq.T.astype(_BF16), wv.T.astype(_BF16), wk.astype(_BF16), jnp.asarray(sel, _BF16))

    tables_nat, tables_t = _rope_tables(seq)
    qa, ka, va, qb, kb, vb, kn2 = _proj(x, ctx, mod3, weights, tables_nat, tables_t)

    cat = _attend(diff_lambda[0], diff_subln.reshape(LANES, 1), qa, ka, va, qb, kb, vb, kn2)

    wf = w_ffn_in[0]
    return _post(cat, x, mod3, w_out[0].astype(_BF16), wf[:, :FFN_HIDDEN].astype(_BF16),
                 wf[:, FFN_HIDDEN:].astype(_BF16), w_ffn_out[0].astype(_BF16), final_norm[None, :])
```

```python
import math

import numpy as np
import jax
import jax.numpy as jnp
from jax import lax
from jax.experimental import pallas as pl
from jax.experimental.pallas import tpu as pltpu

D_MODEL = 1024
CTX_LEN = 256
GRID_W = 64
ROPE_THETA = 10000.0
NORM_EPS = 1e-6

A_HEADS = 4
A_HEAD_DIM = 64
A_WIDTH = A_HEADS * 2 * A_HEAD_DIM
B_HEADS = 8
B_NOPE = 64
B_ROPE = 32
B_VDIM = 64
B_Q_RANK = 256
B_KV_RANK = 128
B_WIDTH = B_HEADS * B_VDIM
FFN_HIDDEN = 2816

LANES = 128
SUBLANES = 8
TOKEN_TILE = 256
ATT_TILE = 256
ADA_TILE = 1024
POST_TILE = 512
POST_SUB = 256
V_PAD = 16
N_MAPS = 2 * A_HEADS + B_HEADS
VMEM_LIMIT = 56 * 1024 * 1024

LAM_INIT = 0.8 - 0.6 * math.exp(-0.3 * 0)
LOG2E = 1.4426950408889634
A_QSCALE = A_HEAD_DIM ** -0.5 * LOG2E
B_QSCALE = (B_NOPE + B_ROPE) ** -0.5 * LOG2E

FAST_LIMIT = 50.0
BOUND_MARGIN = 1.02
FAST_DEPTH = 1
EXACT_DEPTH = 2

_F32 = jnp.float32
_BF16 = jnp.bfloat16


def _dot(a, b):
    return jnp.dot(a, b, preferred_element_type=_F32)


def _dot_nt(a, b):
    return lax.dot_general(a, b, (((1,), (1,)), ((), ())), preferred_element_type=_F32)


def _rms(x, axis=-1):
    return x * lax.rsqrt(jnp.mean(x * x, axis=axis, keepdims=True) + NORM_EPS)


def _adaln_kernel(c_ref, w_ref, b_ref, o_ref):
    c = c_ref[...]
    s = c / (1.0 + jnp.exp(-c))
    o_ref[...] = _dot(s.astype(_BF16), w_ref[...].astype(_BF16)) + b_ref[...]


def _adaln(c_all, w_ada, b_ada):
    rows = c_all.shape[0]
    n = w_ada.shape[1]
    return pl.pallas_call(
        _adaln_kernel,
        grid=(n // ADA_TILE,),
        in_specs=[
            pl.BlockSpec((rows, D_MODEL), lambda i: (0, 0)),
            pl.BlockSpec((D_MODEL, ADA_TILE), lambda i: (0, i)),
            pl.BlockSpec((1, ADA_TILE), lambda i: (0, i)),
        ],
        out_specs=pl.BlockSpec((rows, ADA_TILE), lambda i: (0, i)),
        out_shape=jax.ShapeDtypeStruct((rows, n), _F32),
        name="adaln",
    )(c_all, w_ada, b_ada)


def _rope(x, cos, s_up, s_dn, quarter):
    return (x * cos + pltpu.roll(x, quarter, 1) * s_up
            + pltpu.roll(x, LANES - quarter, 1) * s_dn)


def _rope_t(x, cos_t, sin_t, lo, hi, quarter):
    parts = [x[:lo]] if lo else []
    for r in range(lo, hi, 2 * quarter):
        parts += [x[r + quarter:r + 2 * quarter], x[r:r + quarter]]
    if hi < x.shape[0]:
        parts.append(x[hi:])
    return x * cos_t + jnp.concatenate(parts, axis=0) * sin_t


def _proj_kernel(x_ref, ctx_ref, mod_ref, wc_ref, wn_ref, wqv_ref, qn_ref, kvn_col_ref, kvn_row_ref,
                 wqt_ref, wvt_ref, wk_ref, sel_ref,
                 ca_ref, ua_ref, da_ref, cb_ref, ub_ref, db_ref,
                 cat_ref, sat_ref, cbt_ref, sbt_ref,
                 qa_ref, ka_ref, va_ref, qb_ref, kb_ref, vb_ref, kn2_ref, qn2_ref):
    t = TOKEN_TILE
    j = pl.program_id(1)
    xin = jnp.where(j == 0, ctx_ref[0], x_ref[0])
    mod = mod_ref[0]
    shift = mod[:, 0:D_MODEL]
    scale = mod[:, D_MODEL:2 * D_MODEL]
    h32 = _rms(xin) * (1.0 + scale) + shift
    h = h32.astype(_BF16)
    h_t = h32.T.astype(_BF16)

    yc_t = _dot(wc_ref[...], h_t)
    y = _dot(h, wn_ref[...])
    yqv_t = _dot(wqv_ref[...], h_t)
    cq_t = yc_t[:B_Q_RANK]
    ckv_t = yc_t[B_Q_RANK:]
    ckv = y[:, A_WIDTH:A_WIDTH + B_KV_RANK]
    kr = y[:, A_WIDTH + B_KV_RANK:]
    qb_t = _dot(wqt_ref[...], (_rms(cq_t, 0) * qn_ref[...]).astype(_BF16))
    vb_t = _dot(wvt_ref[...], (_rms(ckv_t, 0) * kvn_col_ref[...]).astype(_BF16))
    kn = _dot((_rms(ckv) * kvn_row_ref[...]).astype(_BF16), wk_ref[...])

    row16 = lax.broadcasted_iota(jnp.int32, (V_PAD, t), 0)
    ones_rows = jnp.where(row16 == 0, 1.0, 0.0).astype(_BF16)
    cat, sat = cat_ref[...], sat_ref[...]
    qn2_rows = []

    def col_sq_norm(q_bf16):
        qf = q_bf16.astype(_F32)
        return jnp.sum(qf * qf, axis=0, keepdims=True)

    for hd in range(A_HEADS):
        q_t = yqv_t[hd * LANES:(hd + 1) * LANES]
        q = (_rope_t(q_t, cat, sat, 0, LANES, A_HEAD_DIM // 4) * A_QSCALE).astype(_BF16)
        qa_ref[0, hd] = q
        qn2_rows += [col_sq_norm(q[:A_HEAD_DIM]), col_sq_norm(q[A_HEAD_DIM:])]
        va_ref[0, hd, :LANES, :] = yqv_t[A_WIDTH + hd * LANES:A_WIDTH + (hd + 1) * LANES].astype(_BF16)
        va_ref[0, hd, LANES:, :] = ones_rows
    cbt, sbt = cbt_ref[...], sbt_ref[...]
    for hd in range(B_HEADS):
        blk = qb_t[hd * LANES:(hd + 1) * LANES]
        q = (_rope_t(blk, cbt, sbt, B_NOPE, B_NOPE + B_ROPE, B_ROPE // 4) * B_QSCALE).astype(_BF16)
        qb_ref[0, hd] = q
        qn2_rows.append(col_sq_norm(q))
        vb_ref[0, hd, :B_VDIM, :] = vb_t[hd * B_VDIM:(hd + 1) * B_VDIM].astype(_BF16)
        vb_ref[0, hd, B_VDIM:, :] = ones_rows
    qn2_ref[0] = jnp.concatenate(qn2_rows, axis=0)

    ca, ua, da = ca_ref[...], ua_ref[...], da_ref[...]
    cb, ub, db = cb_ref[...], ub_ref[...], db_ref[...]
    lane = lax.broadcasted_iota(jnp.int32, (t, LANES), 1)
    ksq = []

    def squares(k_bf16):
        kf = k_bf16.astype(_F32)
        return (kf * kf).astype(_BF16)

    for hd in range(A_HEADS):
        k = _rope(y[:, hd * LANES:(hd + 1) * LANES], ca, ua, da, A_HEAD_DIM // 4).astype(_BF16)
        ka_ref[0, hd] = k
        ksq.append(squares(k))
    kr_rot = _rope(kr, cb, ub, db, B_ROPE // 4)
    for hd in range(B_HEADS):
        k = jnp.where(lane < B_NOPE, kn[:, hd * LANES:(hd + 1) * LANES], kr_rot).astype(_BF16)
        kb_ref[0, hd] = k
        ksq.append(squares(k))
    kn2_ref[0] = _dot_nt(sel_ref[...], jnp.concatenate(ksq, axis=1))


def _proj(x, ctx, mod3, weights, tables_nat, tables_t):
    nb, seq, _ = x.shape
    n_tok = CTX_LEN + seq
    n_tiles = n_tok // TOKEN_TILE
    t = TOKEN_TILE

    def full(a):
        return pl.BlockSpec(a.shape, lambda b, j: (0,) * a.ndim)

    lat = lambda j: jnp.maximum(j - 1, 0)
    out_shapes = (
        jax.ShapeDtypeStruct((nb, A_HEADS, LANES, seq), _BF16),
        jax.ShapeDtypeStruct((nb, A_HEADS, n_tok, LANES), _BF16),
        jax.ShapeDtypeStruct((nb, A_HEADS, LANES + V_PAD, n_tok), _BF16),
        jax.ShapeDtypeStruct((nb, B_HEADS, LANES, seq), _BF16),
        jax.ShapeDtypeStruct((nb, B_HEADS, n_tok, LANES), _BF16),
        jax.ShapeDtypeStruct((nb, B_HEADS, B_VDIM + V_PAD, n_tok), _BF16),
        jax.ShapeDtypeStruct((nb, N_MAPS, n_tok), _F32),
        jax.ShapeDtypeStruct((nb, N_MAPS, seq), _F32),
    )
    out_specs = (
        pl.BlockSpec((1, A_HEADS, LANES, t), lambda b, j: (b, 0, 0, lat(j))),
        pl.BlockSpec((1, A_HEADS, t, LANES), lambda b, j: (b, 0, j, 0)),
        pl.BlockSpec((1, A_HEADS, LANES + V_PAD, t), lambda b, j: (b, 0, 0, j)),
        pl.BlockSpec((1, B_HEADS, LANES, t), lambda b, j: (b, 0, 0, lat(j))),
        pl.BlockSpec((1, B_HEADS, t, LANES), lambda b, j: (b, 0, j, 0)),
        pl.BlockSpec((1, B_HEADS, B_VDIM + V_PAD, t), lambda b, j: (b, 0, 0, j)),
        pl.BlockSpec((1, N_MAPS, t), lambda b, j: (b, 0, j)),
        pl.BlockSpec((1, N_MAPS, t), lambda b, j: (b, 0, lat(j))),
    )
    return pl.pallas_call(
        _proj_kernel,
        grid=(nb, n_tiles),
        in_specs=[
            pl.BlockSpec((1, t, D_MODEL), lambda b, j: (b, lat(j), 0)),
            pl.BlockSpec((1, t, D_MODEL), lambda b, j: (b, 0, 0)),
            pl.BlockSpec((1, 1, 6 * D_MODEL), lambda b, j: (jnp.where(j == 0, nb, b), 0, 0)),
        ] + [full(w) for w in weights]
          + [pl.BlockSpec((t, LANES), lambda b, j: (j, 0))] * len(tables_nat)
          + [pl.BlockSpec((LANES, t), lambda b, j: (0, lat(j)))] * len(tables_t),
        out_specs=out_specs,
        out_shape=out_shapes,
        compiler_params=pltpu.CompilerParams(
            dimension_semantics=("arbitrary", "arbitrary"), vmem_limit_bytes=VMEM_LIMIT),
        name="proj",
    )(x, ctx, mod3, *weights, *tables_nat, *tables_t)


def _attend_kernel(lam_ref, subln_ref, qa_ref, ka_ref, va_ref, qb_ref, kb_ref, vb_ref,
                   kn2_ref, qn2_ref, o_ref):
    t = ATT_TILE
    lp = lam_ref[...]
    lam = (jnp.exp(jnp.sum(lp[0:1] * lp[1:2], axis=-1, keepdims=True))
           - jnp.exp(jnp.sum(lp[2:3] * lp[3:4], axis=-1, keepdims=True)) + LAM_INIT)
    gain = subln_ref[...] * (1.0 - LAM_INIT)
    row = lax.broadcasted_iota(jnp.int32, (LANES, t), 0)
    lo = row < A_HEAD_DIM

    bounds = jnp.sqrt(qn2_ref[0] * jnp.max(kn2_ref[0], axis=-1, keepdims=True)) * BOUND_MARGIN
    fast_ok = jnp.max(bounds) <= FAST_LIMIT

    maps = []
    for hd in range(A_HEADS):
        q_t = qa_ref[0, hd]
        zero = jnp.zeros_like(q_t)
        for half in range(2):
            q_m = jnp.where(lo, q_t, zero) if half == 0 else jnp.where(lo, zero, q_t)
            r = 2 * hd + half
            maps.append((q_m, ka_ref.at[0, hd], va_ref.at[0, hd], LANES, bounds[r:r + 1]))
    for hd in range(B_HEADS):
        r = 2 * A_HEADS + hd
        maps.append((qb_ref[0, hd], kb_ref.at[0, hd], vb_ref.at[0, hd], B_VDIM, bounds[r:r + 1]))

    def finish(i, o_t, acc):
        if i < 2 * A_HEADS:
            hd, half = divmod(i, 2)
            if half == 0:
                acc[hd] = o_t
                return
            o = acc.pop(hd) - lam * o_t
            o = _rms(o, 0) * gain
            o_ref[0, :, hd * LANES:(hd + 1) * LANES] = o.T.astype(o_ref.dtype)
        else:
            hd = i - 2 * A_HEADS
            if hd % 2 == 0:
                acc[hd] = o_t
                return
            o = jnp.concatenate([acc.pop(hd - 1), o_t], axis=0)
            p = hd // 2
            o_ref[0, :, A_WIDTH + p * LANES:A_WIDTH + (p + 1) * LANES] = o.T.astype(o_ref.dtype)

    def run(use_bound, depth):
        acc = {}

        def softmax_pv(i, s_t):
            _, _, v_ref, dv, m_ub = maps[i]
            m = m_ub if use_bound else jnp.max(s_t, axis=0, keepdims=True)
            p_t = jnp.exp2(s_t - m).astype(_BF16)
            o = _dot(v_ref[...], p_t)
            finish(i, o[:dv] / o[dv:dv + 1], acc)

        pending = []
        for i, (q_m, k_ref, _, _, _) in enumerate(maps):
            pending.append((i, _dot(k_ref[...], q_m)))
            if len(pending) > depth:
                softmax_pv(*pending.pop(0))
        for item in pending:
            softmax_pv(*item)

    @pl.when(fast_ok)
    def _():
        run(True, FAST_DEPTH)

    @pl.when(jnp.logical_not(fast_ok))
    def _():
        run(False, EXACT_DEPTH)


def _attend(lam_p, subln_col, qa, ka, va, qb, kb, vb, kn2, qn2):
    nb, _, _, seq = qa.shape
    t = ATT_TILE

    def q_spec(nh):
        return pl.BlockSpec((1, nh, LANES, t), lambda b, i: (b, 0, 0, i))

    def kv_spec(a):
        return pl.BlockSpec((1,) + a.shape[1:], lambda b, i: (b,) + (0,) * (a.ndim - 1))

    return pl.pallas_call(
        _attend_kernel,
        grid=(nb, seq // t),
        in_specs=[
            pl.BlockSpec(lam_p.shape, lambda b, i: (0, 0)),
            pl.BlockSpec(subln_col.shape, lambda b, i: (0, 0)),
            q_spec(A_HEADS), kv_spec(ka), kv_spec(va),
            q_spec(B_HEADS), kv_spec(kb), kv_spec(vb), kv_spec(kn2),
            pl.BlockSpec((1, N_MAPS, t), lambda b, i: (b, 0, i)),
        ],
        out_specs=pl.BlockSpec((1, t, A_WIDTH + B_WIDTH), lambda b, i: (b, i, 0)),
        out_shape=jax.ShapeDtypeStruct((nb, seq, A_WIDTH + B_WIDTH), _BF16),
        compiler_params=pltpu.CompilerParams(
            dimension_semantics=("arbitrary", "arbitrary"), vmem_limit_bytes=VMEM_LIMIT),
        name="attend",
    )(lam_p, subln_col, qa, ka, va, qb, kb, vb, kn2, qn2)


def _post_kernel(cat_ref, x_ref, mod_ref, wo_ref, wg_ref, wu_ref, wd_ref, fn_ref, o_ref):
    mod = mod_ref[0]
    g_a = mod[:, 2 * D_MODEL:3 * D_MODEL]
    sh_f = mod[:, 3 * D_MODEL:4 * D_MODEL]
    sc_f = mod[:, 4 * D_MODEL:5 * D_MODEL]
    g_f = mod[:, 5 * D_MODEL:6 * D_MODEL]
    rows = [slice(r, r + POST_SUB) for r in range(0, POST_TILE, POST_SUB)]
    lats = [x_ref[0, r, :] + g_a * _dot(cat_ref[0, r, :], wo_ref[...]) for r in rows]
    hs = [(_rms(lat) * (1.0 + sc_f) + sh_f).astype(_BF16) for lat in lats]
    gates = [_dot(h, wg_ref[...]) for h in hs]
    ups = [_dot(h, wu_ref[...]) for h in hs]
    acts = [(g / (1.0 + jnp.exp(-g)) * u).astype(_BF16) for g, u in zip(gates, ups)]
    ffns = [_dot(a, wd_ref[...]) for a in acts]
    for r, lat, ffn in zip(rows, lats, ffns):
        o_ref[0, r, :] = _rms(lat + g_f * ffn) * fn_ref[...]


def _post(cat, x, mod3, wo, wg, wu, wd, fn):
    nb, seq, _ = x.shape
    t = POST_TILE

    def full(a):
        return pl.BlockSpec(a.shape, lambda b, i: (0,) * a.ndim, pipeline_mode=pl.Buffered(1))

    tok = pl.BlockSpec((1, t, D_MODEL), lambda b, i: (b, i, 0))
    return pl.pallas_call(
        _post_kernel,
        grid=(nb, seq // t),
        in_specs=[tok, tok, pl.BlockSpec((1, 1, 6 * D_MODEL), lambda b, i: (b, 0, 0)),
                  full(wo), full(wg), full(wu), full(wd), full(fn)],
        out_specs=tok,
        out_shape=jax.ShapeDtypeStruct((nb, seq, D_MODEL), _F32),
        compiler_params=pltpu.CompilerParams(
            dimension_semantics=("arbitrary", "arbitrary"), vmem_limit_bytes=VMEM_LIMIT),
        name="post",
    )(cat, x, mod3, wo, wg, wu, wd, fn)


def _rope_tables(seq):
    pos = np.arange(seq)
    row = (pos // GRID_W).astype(np.float32)
    col = (pos % GRID_W).astype(np.float32)
    lane = np.arange(LANES)

    def build(d, active, half):
        quarter = half // 2
        inv = (ROPE_THETA ** (-(np.arange(quarter, dtype=np.float32)) / quarter)).astype(np.float32)
        p = np.where((d < half)[None, :], row[:, None], col[:, None]).astype(np.float32)
        ang = (p * inv[d % quarter][None, :]).astype(np.float32)
        cos = np.where(active[None, :], np.cos(ang), 1.0).astype(np.float32)
        sin = np.where(active[None, :], np.sin(ang), 0.0).astype(np.float32)
        upper = ((d % half) >= quarter)[None, :]
        s_up = np.where(upper, sin, 0.0).astype(np.float32)
        s_dn = np.where(upper, 0.0, -sin).astype(np.float32)
        ident = np.zeros((CTX_LEN, LANES), np.float32)
        nat = [jnp.asarray(np.concatenate([ident + fill, tab], 0))
               for tab, fill in ((cos, 1.0), (s_up, 0.0), (s_dn, 0.0))]
        trans = [jnp.asarray(np.ascontiguousarray(cos.T)),
                 jnp.asarray(np.ascontiguousarray((s_up + s_dn).T))]
        return nat, trans

    nat_a, trans_a = build(lane % A_HEAD_DIM, np.ones(LANES, bool), A_HEAD_DIM // 2)
    in_rope = (lane >= B_NOPE) & (lane < B_NOPE + B_ROPE)
    nat_b, trans_b = build(np.where(in_rope, lane - B_NOPE, 0), in_rope, B_ROPE // 2)
    return nat_a + nat_b, trans_a + trans_b


def kernel(x, c, ctx, c_ctx, w_ada, b_ada, w_in, q_a_norm, kv_a_norm, w_q_up, w_kv_up,
           diff_lambda, diff_subln, w_out, w_ffn_in, w_ffn_out, final_norm):
    nb, seq, _ = x.shape
    assert w_ada.shape[0] == 1 and seq % ATT_TILE == 0 and ctx.shape[1] == CTX_LEN == TOKEN_TILE

    pad = (-(nb + 1)) % SUBLANES
    c_all = jnp.concatenate([c, c_ctx[None, :], jnp.zeros((pad, D_MODEL), _F32)], axis=0)
    mod = _adaln(c_all, w_ada[0], b_ada[0][None, :])
    mod3 = mod[:, None, :]

    w = w_in[0]
    o_k, o_v, o_cq = A_WIDTH, 2 * A_WIDTH, 3 * A_WIDTH
    o_ckv, o_kr = o_cq + B_Q_RANK, o_cq + B_Q_RANK + B_KV_RANK
    zeros = lambda n: jnp.zeros((D_MODEL, n), w.dtype)
    w_c = w[:, o_cq:o_kr].T
    w_qv = jnp.concatenate([w[:, :o_k], w[:, o_v:o_cq]], axis=1).T
    w_n = jnp.concatenate([w[:, o_k:o_v], w[:, o_ckv:o_kr], zeros(B_NOPE), w[:, o_kr:],
                           zeros(LANES - B_NOPE - B_ROPE)], axis=1)
    wq = jnp.pad(w_q_up[0].reshape(B_Q_RANK, B_HEADS, B_NOPE + B_ROPE),
                 ((0, 0), (0, 0), (0, LANES - B_NOPE - B_ROPE))).reshape(B_Q_RANK, B_HEADS * LANES)
    wkv3 = w_kv_up[0].reshape(B_KV_RANK, B_HEADS, B_NOPE + B_VDIM)
    wk = jnp.pad(wkv3[:, :, :B_NOPE], ((0, 0), (0, 0), (0, LANES - B_NOPE))).reshape(B_KV_RANK, B_HEADS * LANES)
    wv = wkv3[:, :, B_NOPE:].reshape(B_KV_RANK, B_WIDTH)
    sel = np.zeros((N_MAPS, (A_HEADS + B_HEADS) * LANES), np.float32)
    for hd in range(A_HEADS):
        sel[2 * hd, hd * LANES:hd * LANES + A_HEAD_DIM] = 1.0
        sel[2 * hd + 1, hd * LANES + A_HEAD_DIM:(hd + 1) * LANES] = 1.0
    for hd in range(B_HEADS):
        sel[2 * A_HEADS + hd, (A_HEADS + hd) * LANES:(A_HEADS + hd + 1) * LANES] = 1.0
    weights = (w_c.astype(_BF16), w_n.astype(_BF16), w_qv.astype(_BF16),
               q_a_norm.reshape(B_Q_RANK, 1), kv_a_norm.reshape(B_KV_RANK, 1),
               kv_a_norm.reshape(1, B_KV_RANK),
               wq.T.astype(_BF16), wv.T.astype(_BF16), wk.astype(_BF16), jnp.asarray(sel, _BF16))

    tables_nat, tables_t = _rope_tables(seq)
    qa, ka, va, qb, kb, vb, kn2, qn2 = _proj(x, ctx, mod3, weights, tables_nat, tables_t)

    cat = _attend(diff_lambda[0], diff_subln.reshape(LANES, 1), qa, ka, va, qb, kb, vb, kn2, qn2)

    wf = w_ffn_in[0]
    return _post(cat, x, mod3, w_out[0].astype(_BF16), wf[:, :FFN_HIDDEN].astype(_BF16),
                 wf[:, FFN_HIDDEN:].astype(_BF16), w_ffn_out[0].astype(_BF16), final_norm[None, :])
```

```python
import math

import numpy as np
import jax
import jax.numpy as jnp
from jax import lax
from jax.experimental import pallas as pl
from jax.experimental.pallas import tpu as pltpu

D_MODEL = 1024
CTX_LEN = 256
GRID_W = 64
ROPE_THETA = 10000.0
NORM_EPS = 1e-6

A_HEADS = 4
A_HEAD_DIM = 64
A_WIDTH = A_HEADS * 2 * A_HEAD_DIM
B_HEADS = 8
B_NOPE = 64
B_ROPE = 32
B_VDIM = 64
B_Q_RANK = 256
B_KV_RANK = 128
B_WIDTH = B_HEADS * B_VDIM
FFN_HIDDEN = 2816

LANES = 128
SUBLANES = 8
TOKEN_TILE = 256
ATT_TILE = 256
ADA_TILE = 1024
PROJ_ROWS = 2
POST_TILE = 512
POST_SUB = 256
V_PAD = 16
N_MAPS = 2 * A_HEADS + B_HEADS
VMEM_LIMIT = 56 * 1024 * 1024

LAM_INIT = 0.8 - 0.6 * math.exp(-0.3 * 0)
LOG2E = 1.4426950408889634
A_QSCALE = A_HEAD_DIM ** -0.5 * LOG2E
B_QSCALE = (B_NOPE + B_ROPE) ** -0.5 * LOG2E

FAST_LIMIT = 50.0
BOUND_MARGIN = 1.02
FAST_DEPTH = 1
EXACT_DEPTH = 2

_F32 = jnp.float32
_BF16 = jnp.bfloat16


def _dot(a, b):
    return jnp.dot(a, b, preferred_element_type=_F32)


def _dot_nt(a, b):
    return lax.dot_general(a, b, (((1,), (1,)), ((), ())), preferred_element_type=_F32)


def _rms(x, axis=-1):
    return x * lax.rsqrt(jnp.mean(x * x, axis=axis, keepdims=True) + NORM_EPS)


def _adaln_kernel(c_ref, w_ref, b_ref, o_ref):
    c = c_ref[...]
    s = c / (1.0 + jnp.exp(-c))
    o_ref[...] = _dot(s.astype(_BF16), w_ref[...].astype(_BF16)) + b_ref[...]


def _adaln(c_all, w_ada, b_ada):
    rows = c_all.shape[0]
    n = w_ada.shape[1]
    return pl.pallas_call(
        _adaln_kernel,
        grid=(n // ADA_TILE,),
        in_specs=[
            pl.BlockSpec((rows, D_MODEL), lambda i: (0, 0)),
            pl.BlockSpec((D_MODEL, ADA_TILE), lambda i: (0, i)),
            pl.BlockSpec((1, ADA_TILE), lambda i: (0, i)),
        ],
        out_specs=pl.BlockSpec((rows, ADA_TILE), lambda i: (0, i)),
        out_shape=jax.ShapeDtypeStruct((rows, n), _F32),
        name="adaln",
    )(c_all, w_ada, b_ada)


def _rope(x, cos, s_up, s_dn, quarter):
    return (x * cos + pltpu.roll(x, quarter, 1) * s_up
            + pltpu.roll(x, LANES - quarter, 1) * s_dn)


def _rope_t(x, cos_t, sin_t, lo, hi, quarter):
    parts = [x[:lo]] if lo else []
    for r in range(lo, hi, 2 * quarter):
        parts += [x[r + quarter:r + 2 * quarter], x[r:r + quarter]]
    if hi < x.shape[0]:
        parts.append(x[hi:])
    return x * cos_t + jnp.concatenate(parts, axis=0) * sin_t


def _proj_kernel(x_ref, ctx_ref, mod_ref, modc_ref, wc_ref, wn_ref, wqv_ref, qn_ref, kvn_col_ref,
                 kvn_row_ref, wqt_ref, wvt_ref, wk_ref, sel_ref,
                 ca_ref, ua_ref, da_ref, cb_ref, ub_ref, db_ref,
                 cat_ref, sat_ref, cbt_ref, sbt_ref,
                 qa_ref, ka_ref, va_ref, qb_ref, kb_ref, vb_ref, kn2_ref, qn2_ref):
    t = TOKEN_TILE
    chains = range(PROJ_ROWS)
    is_ctx = pl.program_id(1) == 0

    hs, hts = [], []
    for i in chains:
        xin = jnp.where(is_ctx, ctx_ref[i], x_ref[i])
        mod = jnp.where(is_ctx, modc_ref[0], mod_ref[i])
        shift = mod[:, 0:D_MODEL]
        scale = mod[:, D_MODEL:2 * D_MODEL]
        h32 = _rms(xin) * (1.0 + scale) + shift
        hs.append(h32.astype(_BF16))
        hts.append(hs[-1].T)

    ycs = [_dot(wc_ref[...], ht) for ht in hts]
    ys = [_dot(h, wn_ref[...]) for h in hs]
    yqvs = [_dot(wqv_ref[...], ht) for ht in hts]
    qbs = [_dot(wqt_ref[...], (_rms(yc[:B_Q_RANK], 0) * qn_ref[...]).astype(_BF16))
           for yc in ycs]
    vbs = [_dot(wvt_ref[...], (_rms(yc[B_Q_RANK:], 0) * kvn_col_ref[...]).astype(_BF16))
           for yc in ycs]
    kns = [_dot((_rms(y[:, A_WIDTH:A_WIDTH + B_KV_RANK]) * kvn_row_ref[...]).astype(_BF16), wk_ref[...])
           for y in ys]

    row16 = lax.broadcasted_iota(jnp.int32, (V_PAD, t), 0)
    ones_rows = jnp.where(row16 == 0, 1.0, 0.0).astype(_BF16)
    lane = lax.broadcasted_iota(jnp.int32, (t, LANES), 1)
    cat, sat = cat_ref[...], sat_ref[...]
    cbt, sbt = cbt_ref[...], sbt_ref[...]
    ca, ua, da = ca_ref[...], ua_ref[...], da_ref[...]
    cb, ub, db = cb_ref[...], ub_ref[...], db_ref[...]

    def col_sq_norm(q):
        return jnp.sum(q * q, axis=0, keepdims=True)

    def squares(k):
        return (k * k).astype(_BF16)

    for i in chains:
        yqv_t, qb_t, vb_t, y, kn = yqvs[i], qbs[i], vbs[i], ys[i], kns[i]
        qn2_rows = []
        for hd in range(A_HEADS):
            q_t = yqv_t[hd * LANES:(hd + 1) * LANES]
            q = _rope_t(q_t, cat, sat, 0, LANES, A_HEAD_DIM // 4)
            qa_ref[i, hd] = q.astype(_BF16)
            qn2_rows += [col_sq_norm(q[:A_HEAD_DIM]), col_sq_norm(q[A_HEAD_DIM:])]
            va_ref[i, hd, :LANES, :] = yqv_t[A_WIDTH + hd * LANES:A_WIDTH + (hd + 1) * LANES].astype(_BF16)
            va_ref[i, hd, LANES:, :] = ones_rows
        for hd in range(B_HEADS):
            blk = qb_t[hd * LANES:(hd + 1) * LANES]
            q = _rope_t(blk, cbt, sbt, B_NOPE, B_NOPE + B_ROPE, B_ROPE // 4)
            qb_ref[i, hd] = q.astype(_BF16)
            qn2_rows.append(col_sq_norm(q))
            vb_ref[i, hd, :B_VDIM, :] = vb_t[hd * B_VDIM:(hd + 1) * B_VDIM].astype(_BF16)
            vb_ref[i, hd, B_VDIM:, :] = ones_rows
        qn2_ref[i] = jnp.concatenate(qn2_rows, axis=0)

        ksq = []
        for hd in range(A_HEADS):
            k = _rope(y[:, hd * LANES:(hd + 1) * LANES], ca, ua, da, A_HEAD_DIM // 4)
            ka_ref[i, hd] = k.astype(_BF16)
            ksq.append(squares(k))
        kr_rot = _rope(y[:, A_WIDTH + B_KV_RANK:], cb, ub, db, B_ROPE // 4)
        for hd in range(B_HEADS):
            k = jnp.where(lane < B_NOPE, kn[:, hd * LANES:(hd + 1) * LANES], kr_rot)
            kb_ref[i, hd] = k.astype(_BF16)
            ksq.append(squares(k))
        kn2_ref[i] = _dot_nt(sel_ref[...], jnp.concatenate(ksq, axis=1))


def _proj(x, ctx, mod3, weights, tables_nat, tables_t):
    nb, seq, _ = x.shape
    n_tok = CTX_LEN + seq
    n_tiles = n_tok // TOKEN_TILE
    t = TOKEN_TILE

    def full(a):
        return pl.BlockSpec(a.shape, lambda b, j: (0,) * a.ndim)

    lat = lambda j: jnp.maximum(j - 1, 0)
    out_shapes = (
        jax.ShapeDtypeStruct((nb, A_HEADS, LANES, seq), _BF16),
        jax.ShapeDtypeStruct((nb, A_HEADS, n_tok, LANES), _BF16),
        jax.ShapeDtypeStruct((nb, A_HEADS, LANES + V_PAD, n_tok), _BF16),
        jax.ShapeDtypeStruct((nb, B_HEADS, LANES, seq), _BF16),
        jax.ShapeDtypeStruct((nb, B_HEADS, n_tok, LANES), _BF16),
        jax.ShapeDtypeStruct((nb, B_HEADS, B_VDIM + V_PAD, n_tok), _BF16),
        jax.ShapeDtypeStruct((nb, N_MAPS, n_tok), _F32),
        jax.ShapeDtypeStruct((nb, N_MAPS, seq), _F32),
    )
    r = PROJ_ROWS
    out_specs = (
        pl.BlockSpec((r, A_HEADS, LANES, t), lambda b, j: (b, 0, 0, lat(j))),
        pl.BlockSpec((r, A_HEADS, t, LANES), lambda b, j: (b, 0, j, 0)),
        pl.BlockSpec((r, A_HEADS, LANES + V_PAD, t), lambda b, j: (b, 0, 0, j)),
        pl.BlockSpec((r, B_HEADS, LANES, t), lambda b, j: (b, 0, 0, lat(j))),
        pl.BlockSpec((r, B_HEADS, t, LANES), lambda b, j: (b, 0, j, 0)),
        pl.BlockSpec((r, B_HEADS, B_VDIM + V_PAD, t), lambda b, j: (b, 0, 0, j)),
        pl.BlockSpec((r, N_MAPS, t), lambda b, j: (b, 0, j)),
        pl.BlockSpec((r, N_MAPS, t), lambda b, j: (b, 0, lat(j))),
    )
    return pl.pallas_call(
        _proj_kernel,
        grid=(nb // r, n_tiles),
        in_specs=[
            pl.BlockSpec((r, t, D_MODEL), lambda b, j: (b, lat(j), 0)),
            pl.BlockSpec((r, t, D_MODEL), lambda b, j: (b, 0, 0)),
            pl.BlockSpec((r, 1, 6 * D_MODEL), lambda b, j: (b, 0, 0)),
            pl.BlockSpec((1, 1, 6 * D_MODEL), lambda b, j: (nb, 0, 0)),
        ] + [full(w) for w in weights]
          + [pl.BlockSpec((t, LANES), lambda b, j: (j, 0))] * len(tables_nat)
          + [pl.BlockSpec((LANES, t), lambda b, j: (0, lat(j)))] * len(tables_t),
        out_specs=out_specs,
        out_shape=out_shapes,
        compiler_params=pltpu.CompilerParams(
            dimension_semantics=("arbitrary", "arbitrary"), vmem_limit_bytes=VMEM_LIMIT),
        name="proj",
    )(x, ctx, mod3, mod3, *weights, *tables_nat, *tables_t)


def _attend_kernel(lam_ref, subln_ref, qa_ref, ka_ref, va_ref, qb_ref, kb_ref, vb_ref,
                   kn2_ref, qn2_ref, o_ref):
    t = ATT_TILE
    lp = lam_ref[...]
    lam = (jnp.exp(jnp.sum(lp[0:1] * lp[1:2], axis=-1, keepdims=True))
           - jnp.exp(jnp.sum(lp[2:3] * lp[3:4], axis=-1, keepdims=True)) + LAM_INIT)
    gain = subln_ref[...] * (1.0 - LAM_INIT)
    row = lax.broadcasted_iota(jnp.int32, (LANES, t), 0)
    lo = row < A_HEAD_DIM

    bounds = jnp.sqrt(qn2_ref[0] * jnp.max(kn2_ref[0], axis=-1, keepdims=True)) * BOUND_MARGIN
    fast_ok = jnp.max(bounds) <= FAST_LIMIT

    maps = []
    for hd in range(A_HEADS):
        q_t = qa_ref[0, hd]
        zero = jnp.zeros_like(q_t)
        for half in range(2):
            q_m = jnp.where(lo, q_t, zero) if half == 0 else jnp.where(lo, zero, q_t)
            r = 2 * hd + half
            maps.append((q_m, ka_ref.at[0, hd], va_ref.at[0, hd], LANES, bounds[r:r + 1]))
    for hd in range(B_HEADS):
        r = 2 * A_HEADS + hd
        maps.append((qb_ref[0, hd], kb_ref.at[0, hd], vb_ref.at[0, hd], B_VDIM, bounds[r:r + 1]))

    def finish(i, o_t, acc):
        if i < 2 * A_HEADS:
            hd, half = divmod(i, 2)
            if half == 0:
                acc[hd] = o_t
                return
            o = acc.pop(hd) - lam * o_t
            o = _rms(o, 0) * gain
            o_ref[0, :, hd * LANES:(hd + 1) * LANES] = o.T.astype(o_ref.dtype)
        else:
            hd = i - 2 * A_HEADS
            if hd % 2 == 0:
                acc[hd] = o_t
                return
            o = jnp.concatenate([acc.pop(hd - 1), o_t], axis=0)
            p = hd // 2
            o_ref[0, :, A_WIDTH + p * LANES:A_WIDTH + (p + 1) * LANES] = o.T.astype(o_ref.dtype)

    def run(use_bound, depth):
        acc = {}

        def softmax_pv(i, s_t):
            _, _, v_ref, dv, m_ub = maps[i]
            m = m_ub if use_bound else jnp.max(s_t, axis=0, keepdims=True)
            p_t = jnp.exp2(s_t - m).astype(_BF16)
            o = _dot(v_ref[...], p_t)
            finish(i, o[:dv] / o[dv:dv + 1], acc)

        pending = []
        for i, (q_m, k_ref, _, _, _) in enumerate(maps):
            pending.append((i, _dot(k_ref[...], q_m)))
            if len(pending) > depth:
                softmax_pv(*pending.pop(0))
        for item in pending:
            softmax_pv(*item)

    @pl.when(fast_ok)
    def _():
        run(True, FAST_DEPTH)

    @pl.when(jnp.logical_not(fast_ok))
    def _():
        run(False, EXACT_DEPTH)


def _attend(lam_p, subln_col, qa, ka, va, qb, kb, vb, kn2, qn2):
    nb, _, _, seq = qa.shape
    t = ATT_TILE

    def q_spec(nh):
        return pl.BlockSpec((1, nh, LANES, t), lambda b, i: (b, 0, 0, i))

    def kv_spec(a):
        return pl.BlockSpec((1,) + a.shape[1:], lambda b, i: (b,) + (0,) * (a.ndim - 1))

    return pl.pallas_call(
        _attend_kernel,
        grid=(nb, seq // t),
        in_specs=[
            pl.BlockSpec(lam_p.shape, lambda b, i: (0, 0)),
            pl.BlockSpec(subln_col.shape, lambda b, i: (0, 0)),
            q_spec(A_HEADS), kv_spec(ka), kv_spec(va),
            q_spec(B_HEADS), kv_spec(kb), kv_spec(vb), kv_spec(kn2),
            pl.BlockSpec((1, N_MAPS, t), lambda b, i: (b, 0, i)),
        ],
        out_specs=pl.BlockSpec((1, t, A_WIDTH + B_WIDTH), lambda b, i: (b, i, 0)),
        out_shape=jax.ShapeDtypeStruct((nb, seq, A_WIDTH + B_WIDTH), _BF16),
        compiler_params=pltpu.CompilerParams(
            dimension_semantics=("arbitrary", "arbitrary"), vmem_limit_bytes=VMEM_LIMIT),
        name="attend",
    )(lam_p, subln_col, qa, ka, va, qb, kb, vb, kn2, qn2)


def _post_kernel(cat_ref, x_ref, mod_ref, wo_ref, wg_ref, wu_ref, wd_ref, fn_ref, o_ref):
    mod = mod_ref[0]
    g_a = mod[:, 2 * D_MODEL:3 * D_MODEL]
    sh_f = mod[:, 3 * D_MODEL:4 * D_MODEL]
    sc_f = mod[:, 4 * D_MODEL:5 * D_MODEL]
    g_f = mod[:, 5 * D_MODEL:6 * D_MODEL]
    rows = [slice(r, r + POST_SUB) for r in range(0, POST_TILE, POST_SUB)]
    lats = [x_ref[0, r, :] + g_a * _dot(cat_ref[0, r, :], wo_ref[...]) for r in rows]
    hs = [(_rms(lat) * (1.0 + sc_f) + sh_f).astype(_BF16) for lat in lats]
    gates = [_dot(h, wg_ref[...]) for h in hs]
    ups = [_dot(h, wu_ref[...]) for h in hs]
    acts = [(g / (1.0 + jnp.exp(-g)) * u).astype(_BF16) for g, u in zip(gates, ups)]
    ffns = [_dot(a, wd_ref[...]) for a in acts]
    for r, lat, ffn in zip(rows, lats, ffns):
        o_ref[0, r, :] = _rms(lat + g_f * ffn) * fn_ref[...]


def _post(cat, x, mod3, wo, wg, wu, wd, fn):
    nb, seq, _ = x.shape
    t = POST_TILE

    def full(a):
        return pl.BlockSpec(a.shape, lambda b, i: (0,) * a.ndim, pipeline_mode=pl.Buffered(1))

    tok = pl.BlockSpec((1, t, D_MODEL), lambda b, i: (b, i, 0))
    return pl.pallas_call(
        _post_kernel,
        grid=(nb, seq // t),
        in_specs=[tok, tok, pl.BlockSpec((1, 1, 6 * D_MODEL), lambda b, i: (b, 0, 0)),
                  full(wo), full(wg), full(wu), full(wd), full(fn)],
        out_specs=tok,
        out_shape=jax.ShapeDtypeStruct((nb, seq, D_MODEL), _F32),
        compiler_params=pltpu.CompilerParams(
            dimension_semantics=("arbitrary", "arbitrary"), vmem_limit_bytes=VMEM_LIMIT),
        name="post",
    )(cat, x, mod3, wo, wg, wu, wd, fn)


def _rope_tables(seq):
    pos = np.arange(seq)
    row = (pos // GRID_W).astype(np.float32)
    col = (pos % GRID_W).astype(np.float32)
    lane = np.arange(LANES)

    def build(d, active, half, q_scale):
        quarter = half // 2
        inv = (ROPE_THETA ** (-(np.arange(quarter, dtype=np.float32)) / quarter)).astype(np.float32)
        p = np.where((d < half)[None, :], row[:, None], col[:, None]).astype(np.float32)
        ang = (p * inv[d % quarter][None, :]).astype(np.float32)
        cos = np.where(active[None, :], np.cos(ang), 1.0).astype(np.float32)
        sin = np.where(active[None, :], np.sin(ang), 0.0).astype(np.float32)
        upper = ((d % half) >= quarter)[None, :]
        s_up = np.where(upper, sin, 0.0).astype(np.float32)
        s_dn = np.where(upper, 0.0, -sin).astype(np.float32)
        ident = np.zeros((CTX_LEN, LANES), np.float32)
        nat = [jnp.asarray(np.concatenate([ident + fill, tab], 0))
               for tab, fill in ((cos, 1.0), (s_up, 0.0), (s_dn, 0.0))]
        trans = [jnp.asarray(np.ascontiguousarray(cos.T * np.float32(q_scale))),
                 jnp.asarray(np.ascontiguousarray((s_up + s_dn).T * np.float32(q_scale)))]
        return nat, trans

    nat_a, trans_a = build(lane % A_HEAD_DIM, np.ones(LANES, bool), A_HEAD_DIM // 2, A_QSCALE)
    in_rope = (lane >= B_NOPE) & (lane < B_NOPE + B_ROPE)
    nat_b, trans_b = build(np.where(in_rope, lane - B_NOPE, 0), in_rope, B_ROPE // 2, B_QSCALE)
    return nat_a + nat_b, trans_a + trans_b


def kernel(x, c, ctx, c_ctx, w_ada, b_ada, w_in, q_a_norm, kv_a_norm, w_q_up, w_kv_up,
           diff_lambda, diff_subln, w_out, w_ffn_in, w_ffn_out, final_norm):
    nb, seq, _ = x.shape
    assert w_ada.shape[0] == 1 and seq % ATT_TILE == 0 and ctx.shape[1] == CTX_LEN == TOKEN_TILE

    pad = (-(nb + 1)) % SUBLANES
    c_all = jnp.concatenate([c, c_ctx[None, :], jnp.zeros((pad, D_MODEL), _F32)], axis=0)
    mod = _adaln(c_all, w_ada[0], b_ada[0][None, :])
    mod3 = mod[:, None, :]

    w = w_in[0]
    o_k, o_v, o_cq = A_WIDTH, 2 * A_WIDTH, 3 * A_WIDTH
    o_ckv, o_kr = o_cq + B_Q_RANK, o_cq + B_Q_RANK + B_KV_RANK
    zeros = lambda n: jnp.zeros((D_MODEL, n), w.dtype)
    w_c = w[:, o_cq:o_kr].T
    w_qv = jnp.concatenate([w[:, :o_k], w[:, o_v:o_cq]], axis=1).T
    w_n = jnp.concatenate([w[:, o_k:o_v], w[:, o_ckv:o_kr], zeros(B_NOPE), w[:, o_kr:],
                           zeros(LANES - B_NOPE - B_ROPE)], axis=1)
    wq = jnp.pad(w_q_up[0].reshape(B_Q_RANK, B_HEADS, B_NOPE + B_ROPE),
                 ((0, 0), (0, 0), (0, LANES - B_NOPE - B_ROPE))).reshape(B_Q_RANK, B_HEADS * LANES)
    wkv3 = w_kv_up[0].reshape(B_KV_RANK, B_HEADS, B_NOPE + B_VDIM)
    wk = jnp.pad(wkv3[:, :, :B_NOPE], ((0, 0), (0, 0), (0, LANES - B_NOPE))).reshape(B_KV_RANK, B_HEADS * LANES)
    wv = wkv3[:, :, B_NOPE:].reshape(B_KV_RANK, B_WIDTH)
    sel = np.zeros((N_MAPS, (A_HEADS + B_HEADS) * LANES), np.float32)
    for hd in range(A_HEADS):
        sel[2 * hd, hd * LANES:hd * LANES + A_HEAD_DIM] = 1.0
        sel[2 * hd + 1, hd * LANES + A_HEAD_DIM:(hd + 1) * LANES] = 1.0
    for hd in range(B_HEADS):
        sel[2 * A_HEADS + hd, (A_HEADS + hd) * LANES:(A_HEADS + hd + 1) * LANES] = 1.0
    weights = (w_c.astype(_BF16), w_n.astype(_BF16), w_qv.astype(_BF16),
               q_a_norm.reshape(B_Q_RANK, 1), kv_a_norm.reshape(B_KV_RANK, 1),
               kv_a_norm.reshape(1, B_KV_RANK),
               wq.T.astype(_BF16), wv.T.astype(_BF16), wk.astype(_BF16), jnp.asarray(sel, _BF16))

    tables_nat, tables_t = _rope_tables(seq)
    qa, ka, va, qb, kb, vb, kn2, qn2 = _proj(x, ctx, mod3, weights, tables_nat, tables_t)

    cat = _attend(diff_lambda[0], diff_subln.reshape(LANES, 1), qa, ka, va, qb, kb, vb, kn2, qn2)

    wf = w_ffn_in[0]
    return _post(cat, x, mod3, w_out[0].astype(_BF16), wf[:, :FFN_HIDDEN].astype(_BF16),
                 wf[:, FFN_HIDDEN:].astype(_BF16), w_ffn_out[0].astype(_BF16), final_norm[None, :])
```

```python
import math

import numpy as np
import jax
import jax.numpy as jnp
from jax import lax
from jax.experimental import pallas as pl
from jax.experimental.pallas import tpu as pltpu

D_MODEL = 1024
CTX_LEN = 256
GRID_W = 64
ROPE_THETA = 10000.0
NORM_EPS = 1e-6

A_HEADS = 4
A_HEAD_DIM = 64
A_WIDTH = A_HEADS * 2 * A_HEAD_DIM
B_HEADS = 8
B_NOPE = 64
B_ROPE = 32
B_VDIM = 64
B_Q_RANK = 256
B_KV_RANK = 128
B_WIDTH = B_HEADS * B_VDIM
FFN_HIDDEN = 2816

LANES = 128
SUBLANES = 8
TOKEN_TILE = 256
ATT_TILE = 256
ADA_TILE = 1024
PROJ_ROWS = 2
POST_TILE = 512
POST_SUB = 256
V_PAD = 16
N_MAPS = 2 * A_HEADS + B_HEADS
VMEM_LIMIT = 56 * 1024 * 1024

LAM_INIT = 0.8 - 0.6 * math.exp(-0.3 * 0)
LOG2E = 1.4426950408889634
A_QSCALE = A_HEAD_DIM ** -0.5 * LOG2E
B_QSCALE = (B_NOPE + B_ROPE) ** -0.5 * LOG2E

FAST_LIMIT = 50.0
BOUND_MARGIN = 1.02
FAST_DEPTH = 1
EXACT_DEPTH = 2

_F32 = jnp.float32
_BF16 = jnp.bfloat16


def _dot(a, b):
    return jnp.dot(a, b, preferred_element_type=_F32)


def _dot_nt(a, b):
    return lax.dot_general(a, b, (((1,), (1,)), ((), ())), preferred_element_type=_F32)


def _rms(x, axis=-1):
    return x * lax.rsqrt(jnp.mean(x * x, axis=axis, keepdims=True) + NORM_EPS)


def _adaln_kernel(c_ref, w_ref, b_ref, o_ref):
    c = c_ref[...]
    s = c / (1.0 + jnp.exp(-c))
    o_ref[...] = _dot(s.astype(_BF16), w_ref[...].astype(_BF16)) + b_ref[...]


def _adaln(c_all, w_ada, b_ada):
    rows = c_all.shape[0]
    n = w_ada.shape[1]
    return pl.pallas_call(
        _adaln_kernel,
        grid=(n // ADA_TILE,),
        in_specs=[
            pl.BlockSpec((rows, D_MODEL), lambda i: (0, 0)),
            pl.BlockSpec((D_MODEL, ADA_TILE), lambda i: (0, i)),
            pl.BlockSpec((1, ADA_TILE), lambda i: (0, i)),
        ],
        out_specs=pl.BlockSpec((rows, ADA_TILE), lambda i: (0, i)),
        out_shape=jax.ShapeDtypeStruct((rows, n), _F32),
        name="adaln",
    )(c_all, w_ada, b_ada)


def _rope(x, cos, s_up, s_dn, quarter):
    return (x * cos + pltpu.roll(x, quarter, 1) * s_up
            + pltpu.roll(x, LANES - quarter, 1) * s_dn)


def _rope_t(x, cos_t, sin_t, lo, hi, quarter):
    parts = [x[:lo]] if lo else []
    for r in range(lo, hi, 2 * quarter):
        parts += [x[r + quarter:r + 2 * quarter], x[r:r + quarter]]
    if hi < x.shape[0]:
        parts.append(x[hi:])
    return x * cos_t + jnp.concatenate(parts, axis=0) * sin_t


def _proj_kernel(x_ref, ctx_ref, mod_ref, modc_ref, wc_ref, wn_ref, wqv_ref, qn_ref, kvn_col_ref,
                 kvn_row_ref, wqt_ref, wvt_ref, wk_ref, sel_ref,
                 ca_ref, ua_ref, da_ref, cb_ref, ub_ref, db_ref,
                 cat_ref, sat_ref, cbt_ref, sbt_ref,
                 qa_ref, ka_ref, va_ref, qb_ref, kb_ref, vb_ref, kn2_ref, qn2_ref):
    t = TOKEN_TILE
    chains = range(PROJ_ROWS)
    is_ctx = pl.program_id(1) == 0

    hs, hts = [], []
    for i in chains:
        xin = jnp.where(is_ctx, ctx_ref[i], x_ref[i])
        mod = jnp.where(is_ctx, modc_ref[0], mod_ref[i])
        shift = mod[:, 0:D_MODEL]
        scale = mod[:, D_MODEL:2 * D_MODEL]
        h32 = _rms(xin) * (1.0 + scale) + shift
        hs.append(h32.astype(_BF16))
        hts.append(hs[-1].T)

    ycs = [_dot(wc_ref[...], ht) for ht in hts]
    ys = [_dot(h, wn_ref[...]) for h in hs]
    yqvs = [_dot(wqv_ref[...], ht) for ht in hts]
    qbs = [_dot(wqt_ref[...], (_rms(yc[:B_Q_RANK], 0) * qn_ref[...]).astype(_BF16))
           for yc in ycs]
    vbs = [_dot(wvt_ref[...], (_rms(yc[B_Q_RANK:], 0) * kvn_col_ref[...]).astype(_BF16))
           for yc in ycs]
    kns = [_dot((_rms(y[:, A_WIDTH:A_WIDTH + B_KV_RANK]) * kvn_row_ref[...]).astype(_BF16), wk_ref[...])
           for y in ys]

    row16 = lax.broadcasted_iota(jnp.int32, (V_PAD, t), 0)
    ones_rows = jnp.where(row16 == 0, 1.0, 0.0).astype(_BF16)
    lane = lax.broadcasted_iota(jnp.int32, (t, LANES), 1)
    cat, sat = cat_ref[...], sat_ref[...]
    cbt, sbt = cbt_ref[...], sbt_ref[...]
    ca, ua, da = ca_ref[...], ua_ref[...], da_ref[...]
    cb, ub, db = cb_ref[...], ub_ref[...], db_ref[...]

    def col_sq_norm(q):
        return jnp.sum(q * q, axis=0, keepdims=True)

    def squares(k):
        return (k * k).astype(_BF16)

    for i in chains:
        yqv_t, qb_t, vb_t, y, kn = yqvs[i], qbs[i], vbs[i], ys[i], kns[i]
        qn2_rows = []
        for hd in range(A_HEADS):
            q_t = yqv_t[hd * LANES:(hd + 1) * LANES]
            q = _rope_t(q_t, cat, sat, 0, LANES, A_HEAD_DIM // 4)
            qa_ref[i, hd] = q.astype(_BF16)
            qn2_rows += [col_sq_norm(q[:A_HEAD_DIM]), col_sq_norm(q[A_HEAD_DIM:])]
            va_ref[i, hd, :LANES, :] = yqv_t[A_WIDTH + hd * LANES:A_WIDTH + (hd + 1) * LANES].astype(_BF16)
            va_ref[i, hd, LANES:, :] = ones_rows
        for hd in range(B_HEADS):
            blk = qb_t[hd * LANES:(hd + 1) * LANES]
            q = _rope_t(blk, cbt, sbt, B_NOPE, B_NOPE + B_ROPE, B_ROPE // 4)
            qb_ref[i, hd] = q.astype(_BF16)
            qn2_rows.append(col_sq_norm(q))
            vb_ref[i, hd, :B_VDIM, :] = vb_t[hd * B_VDIM:(hd + 1) * B_VDIM].astype(_BF16)
            vb_ref[i, hd, B_VDIM:, :] = ones_rows
        qn2_ref[i] = jnp.concatenate(qn2_rows, axis=0)

        ksq = []
        for hd in range(A_HEADS):
            k = _rope(y[:, hd * LANES:(hd + 1) * LANES], ca, ua, da, A_HEAD_DIM // 4)
            ka_ref[i, hd] = k.astype(_BF16)
            ksq.append(squares(k))
        kr_rot = _rope(y[:, A_WIDTH + B_KV_RANK:], cb, ub, db, B_ROPE // 4)
        for hd in range(B_HEADS):
            k = jnp.where(lane < B_NOPE, kn[:, hd * LANES:(hd + 1) * LANES], kr_rot)
            kb_ref[i, hd] = k.astype(_BF16)
            ksq.append(squares(k))
        kn2_ref[i] = _dot_nt(sel_ref[...], jnp.concatenate(ksq, axis=1))


def _proj(x, ctx, mod3, weights, tables_nat, tables_t):
    nb, seq, _ = x.shape
    n_tok = CTX_LEN + seq
    n_tiles = n_tok // TOKEN_TILE
    t = TOKEN_TILE

    def full(a):
        return pl.BlockSpec(a.shape, lambda b, j: (0,) * a.ndim)

    lat = lambda j: jnp.maximum(j - 1, 0)
    out_shapes = (
        jax.ShapeDtypeStruct((nb, A_HEADS, LANES, seq), _BF16),
        jax.ShapeDtypeStruct((nb, A_HEADS, n_tok, LANES), _BF16),
        jax.ShapeDtypeStruct((nb, A_HEADS, LANES + V_PAD, n_tok), _BF16),
        jax.ShapeDtypeStruct((nb, B_HEADS, LANES, seq), _BF16),
        jax.ShapeDtypeStruct((nb, B_HEADS, n_tok, LANES), _BF16),
        jax.ShapeDtypeStruct((nb, B_HEADS, B_VDIM + V_PAD, n_tok), _BF16),
        jax.ShapeDtypeStruct((nb, N_MAPS, n_tok), _F32),
        jax.ShapeDtypeStruct((nb, N_MAPS, seq), _F32),
    )
    r = PROJ_ROWS
    out_specs = (
        pl.BlockSpec((r, A_HEADS, LANES, t), lambda b, j: (b, 0, 0, lat(j))),
        pl.BlockSpec((r, A_HEADS, t, LANES), lambda b, j: (b, 0, j, 0)),
        pl.BlockSpec((r, A_HEADS, LANES + V_PAD, t), lambda b, j: (b, 0, 0, j)),
        pl.BlockSpec((r, B_HEADS, LANES, t), lambda b, j: (b, 0, 0, lat(j))),
        pl.BlockSpec((r, B_HEADS, t, LANES), lambda b, j: (b, 0, j, 0)),
        pl.BlockSpec((r, B_HEADS, B_VDIM + V_PAD, t), lambda b, j: (b, 0, 0, j)),
        pl.BlockSpec((r, N_MAPS, t), lambda b, j: (b, 0, j)),
        pl.BlockSpec((r, N_MAPS, t), lambda b, j: (b, 0, lat(j))),
    )
    return pl.pallas_call(
        _proj_kernel,
        grid=(nb // r, n_tiles),
        in_specs=[
            pl.BlockSpec((r, t, D_MODEL), lambda b, j: (b, lat(j), 0)),
            pl.BlockSpec((r, t, D_MODEL), lambda b, j: (b, 0, 0)),
            pl.BlockSpec((r, 1, 6 * D_MODEL), lambda b, j: (b, 0, 0)),
            pl.BlockSpec((1, 1, 6 * D_MODEL), lambda b, j: (nb, 0, 0)),
        ] + [full(w) for w in weights]
          + [pl.BlockSpec((t, LANES), lambda b, j: (j, 0))] * len(tables_nat)
          + [pl.BlockSpec((LANES, t), lambda b, j: (0, lat(j)))] * len(tables_t),
        out_specs=out_specs,
        out_shape=out_shapes,
        compiler_params=pltpu.CompilerParams(
            dimension_semantics=("arbitrary", "arbitrary"), vmem_limit_bytes=VMEM_LIMIT),
        name="proj",
    )(x, ctx, mod3, mod3, *weights, *tables_nat, *tables_t)


def _score_bound(qn2, kn2):
    return jnp.sqrt(qn2 * jnp.max(kn2, axis=-1, keepdims=True)) * BOUND_MARGIN


def _flags_kernel(kn2_ref, qn2_ref, o_ref):
    kn2 = kn2_ref[...]
    cols = []
    for i in range(qn2_ref.shape[2] // ATT_TILE):
        bound = _score_bound(qn2_ref[:, :, i * ATT_TILE:(i + 1) * ATT_TILE], kn2)
        worst = jnp.max(jnp.max(bound, axis=2), axis=1, keepdims=True)
        cols.append(jnp.where(worst <= FAST_LIMIT, 1, 0).astype(jnp.int32))
    o_ref[...] = jnp.concatenate(cols, axis=1)


def _flags(kn2, qn2):
    nb, _, seq = qn2.shape
    return pl.pallas_call(
        _flags_kernel,
        out_shape=jax.ShapeDtypeStruct((nb, seq // ATT_TILE), jnp.int32),
        name="flags",
    )(kn2, qn2)


def _attend_body(use_bound, depth, lam_ref, subln_ref, qa_ref, ka_ref, va_ref, qb_ref, kb_ref, vb_ref,
                 kn2_ref, qn2_ref, o_ref):
    t = ATT_TILE
    lp = lam_ref[...]
    lam = (jnp.exp(jnp.sum(lp[0:1] * lp[1:2], axis=-1, keepdims=True))
           - jnp.exp(jnp.sum(lp[2:3] * lp[3:4], axis=-1, keepdims=True)) + LAM_INIT)
    gain = subln_ref[...] * (1.0 - LAM_INIT)
    row = lax.broadcasted_iota(jnp.int32, (LANES, t), 0)
    lo = row < A_HEAD_DIM

    maps = []
    for hd in range(A_HEADS):
        q_t = qa_ref[0, hd]
        zero = jnp.zeros_like(q_t)
        for half in range(2):
            q_m = jnp.where(lo, q_t, zero) if half == 0 else jnp.where(lo, zero, q_t)
            maps.append((q_m, ka_ref.at[0, hd], va_ref.at[0, hd], LANES))
    for hd in range(B_HEADS):
        maps.append((qb_ref[0, hd], kb_ref.at[0, hd], vb_ref.at[0, hd], B_VDIM))

    def scores_t(i):
        q_m, k_ref, _, _ = maps[i]
        return _dot(k_ref[...], q_m)

    bounds = _score_bound(qn2_ref[0], kn2_ref[0]) if use_bound else None

    def finish(i, o_t, acc):
        if i < 2 * A_HEADS:
            hd, half = divmod(i, 2)
            if half == 0:
                acc[hd] = o_t
                return
            o = acc.pop(hd) - lam * o_t
            o = _rms(o, 0) * gain
            o_ref[0, :, hd * LANES:(hd + 1) * LANES] = o.T.astype(o_ref.dtype)
        else:
            hd = i - 2 * A_HEADS
            if hd % 2 == 0:
                acc[hd] = o_t
                return
            o = jnp.concatenate([acc.pop(hd - 1), o_t], axis=0)
            p = hd // 2
            o_ref[0, :, A_WIDTH + p * LANES:A_WIDTH + (p + 1) * LANES] = o.T.astype(o_ref.dtype)

    acc = {}

    def softmax_pv(i, s_t):
        _, _, v_ref, dv = maps[i]
        m = bounds[i:i + 1] if use_bound else jnp.max(s_t, axis=0, keepdims=True)
        p_t = jnp.exp2(s_t - m).astype(_BF16)
        o = _dot(v_ref[...], p_t)
        finish(i, o[:dv] / o[dv:dv + 1], acc)

    pending = []
    for i in range(len(maps)):
        pending.append((i, scores_t(i)))
        if len(pending) > depth:
            softmax_pv(*pending.pop(0))
    for item in pending:
        softmax_pv(*item)


def _attend_kernel(flags_ref, *refs):
    fast_ok = flags_ref[pl.program_id(0), pl.program_id(1)] != 0

    @pl.when(fast_ok)
    def _():
        _attend_body(True, FAST_DEPTH, *refs)

    @pl.when(jnp.logical_not(fast_ok))
    def _():
        _attend_body(False, EXACT_DEPTH, *refs)


def _attend(lam_p, subln_col, qa, ka, va, qb, kb, vb, kn2, qn2):
    nb, _, _, seq = qa.shape
    t = ATT_TILE

    def q_spec(nh):
        return pl.BlockSpec((1, nh, LANES, t), lambda b, i, flags: (b, 0, 0, i))

    def kv_spec(a):
        return pl.BlockSpec((1,) + a.shape[1:], lambda b, i, flags: (b,) + (0,) * (a.ndim - 1))

    grid_spec = pltpu.PrefetchScalarGridSpec(
        num_scalar_prefetch=1,
        grid=(nb, seq // t),
        in_specs=[
            pl.BlockSpec(lam_p.shape, lambda b, i, flags: (0, 0)),
            pl.BlockSpec(subln_col.shape, lambda b, i, flags: (0, 0)),
            q_spec(A_HEADS), kv_spec(ka), kv_spec(va),
            q_spec(B_HEADS), kv_spec(kb), kv_spec(vb), kv_spec(kn2),
            pl.BlockSpec((1, N_MAPS, t), lambda b, i, flags: (b, 0, i)),
        ],
        out_specs=pl.BlockSpec((1, t, A_WIDTH + B_WIDTH), lambda b, i, flags: (b, i, 0)),
    )
    return pl.pallas_call(
        _attend_kernel,
        grid_spec=grid_spec,
        out_shape=jax.ShapeDtypeStruct((nb, seq, A_WIDTH + B_WIDTH), _BF16),
        compiler_params=pltpu.CompilerParams(
            dimension_semantics=("arbitrary", "arbitrary"), vmem_limit_bytes=VMEM_LIMIT),
        name="attend",
    )(_flags(kn2, qn2), lam_p, subln_col, qa, ka, va, qb, kb, vb, kn2, qn2)


def _post_kernel(cat_ref, x_ref, mod_ref, wo_ref, wg_ref, wu_ref, wd_ref, fn_ref, o_ref):
    mod = mod_ref[0]
    g_a = mod[:, 2 * D_MODEL:3 * D_MODEL]
    sh_f = mod[:, 3 * D_MODEL:4 * D_MODEL]
    sc_f = mod[:, 4 * D_MODEL:5 * D_MODEL]
    g_f = mod[:, 5 * D_MODEL:6 * D_MODEL]
    rows = [slice(r, r + POST_SUB) for r in range(0, POST_TILE, POST_SUB)]
    lats = [x_ref[0, r, :] + g_a * _dot(cat_ref[0, r, :], wo_ref[...]) for r in rows]
    hs = [(_rms(lat) * (1.0 + sc_f) + sh_f).astype(_BF16) for lat in lats]
    gates = [_dot(h, wg_ref[...]) for h in hs]
    ups = [_dot(h, wu_ref[...]) for h in hs]
    acts = [(g / (1.0 + jnp.exp(-g)) * u).astype(_BF16) for g, u in zip(gates, ups)]
    ffns = [_dot(a, wd_ref[...]) for a in acts]
    for r, lat, ffn in zip(rows, lats, ffns):
        o_ref[0, r, :] = _rms(lat + g_f * ffn) * fn_ref[...]


def _post(cat, x, mod3, wo, wg, wu, wd, fn):
    nb, seq, _ = x.shape
    t = POST_TILE

    def full(a):
        return pl.BlockSpec(a.shape, lambda b, i: (0,) * a.ndim, pipeline_mode=pl.Buffered(1))

    tok = pl.BlockSpec((1, t, D_MODEL), lambda b, i: (b, i, 0))
    return pl.pallas_call(
        _post_kernel,
        grid=(nb, seq // t),
        in_specs=[tok, tok, pl.BlockSpec((1, 1, 6 * D_MODEL), lambda b, i: (b, 0, 0)),
                  full(wo), full(wg), full(wu), full(wd), full(fn)],
        out_specs=tok,
        out_shape=jax.ShapeDtypeStruct((nb, seq, D_MODEL), _F32),
        compiler_params=pltpu.CompilerParams(
            dimension_semantics=("arbitrary", "arbitrary"), vmem_limit_bytes=VMEM_LIMIT),
        name="post",
    )(cat, x, mod3, wo, wg, wu, wd, fn)


def _rope_tables(seq):
    pos = np.arange(seq)
    row = (pos // GRID_W).astype(np.float32)
    col = (pos % GRID_W).astype(np.float32)
    lane = np.arange(LANES)

    def build(d, active, half, q_scale):
        quarter = half // 2
        inv = (ROPE_THETA ** (-(np.arange(quarter, dtype=np.float32)) / quarter)).astype(np.float32)
        p = np.where((d < half)[None, :], row[:, None], col[:, None]).astype(np.float32)
        ang = (p * inv[d % quarter][None, :]).astype(np.float32)
        cos = np.where(active[None, :], np.cos(ang), 1.0).astype(np.float32)
        sin = np.where(active[None, :], np.sin(ang), 0.0).astype(np.float32)
        upper = ((d % half) >= quarter)[None, :]
        s_up = np.where(upper, sin, 0.0).astype(np.float32)
        s_dn = np.where(upper, 0.0, -sin).astype(np.float32)
        ident = np.zeros((CTX_LEN, LANES), np.float32)
        nat = [jnp.asarray(np.concatenate([ident + fill, tab], 0))
               for tab, fill in ((cos, 1.0), (s_up, 0.0), (s_dn, 0.0))]
        trans = [jnp.asarray(np.ascontiguousarray(cos.T * np.float32(q_scale))),
                 jnp.asarray(np.ascontiguousarray((s_up + s_dn).T * np.float32(q_scale)))]
        return nat, trans

    nat_a, trans_a = build(lane % A_HEAD_DIM, np.ones(LANES, bool), A_HEAD_DIM // 2, A_QSCALE)
    in_rope = (lane >= B_NOPE) & (lane < B_NOPE + B_ROPE)
    nat_b, trans_b = build(np.where(in_rope, lane - B_NOPE, 0), in_rope, B_ROPE // 2, B_QSCALE)
    return nat_a + nat_b, trans_a + trans_b


def kernel(x, c, ctx, c_ctx, w_ada, b_ada, w_in, q_a_norm, kv_a_norm, w_q_up, w_kv_up,
           diff_lambda, diff_subln, w_out, w_ffn_in, w_ffn_out, final_norm):
    nb, seq, _ = x.shape
    assert w_ada.shape[0] == 1 and seq % ATT_TILE == 0 and ctx.shape[1] == CTX_LEN == TOKEN_TILE

    pad = (-(nb + 1)) % SUBLANES
    c_all = jnp.concatenate([c, c_ctx[None, :], jnp.zeros((pad, D_MODEL), _F32)], axis=0)
    mod = _adaln(c_all, w_ada[0], b_ada[0][None, :])
    mod3 = mod[:, None, :]

    w = w_in[0]
    o_k, o_v, o_cq = A_WIDTH, 2 * A_WIDTH, 3 * A_WIDTH
    o_ckv, o_kr = o_cq + B_Q_RANK, o_cq + B_Q_RANK + B_KV_RANK
    zeros = lambda n: jnp.zeros((D_MODEL, n), w.dtype)
    w_c = w[:, o_cq:o_kr].T
    w_qv = jnp.concatenate([w[:, :o_k], w[:, o_v:o_cq]], axis=1).T
    w_n = jnp.concatenate([w[:, o_k:o_v], w[:, o_ckv:o_kr], zeros(B_NOPE), w[:, o_kr:],
                           zeros(LANES - B_NOPE - B_ROPE)], axis=1)
    wq = jnp.pad(w_q_up[0].reshape(B_Q_RANK, B_HEADS, B_NOPE + B_ROPE),
                 ((0, 0), (0, 0), (0, LANES - B_NOPE - B_ROPE))).reshape(B_Q_RANK, B_HEADS * LANES)
    wkv3 = w_kv_up[0].reshape(B_KV_RANK, B_HEADS, B_NOPE + B_VDIM)
    wk = jnp.pad(wkv3[:, :, :B_NOPE], ((0, 0), (0, 0), (0, LANES - B_NOPE))).reshape(B_KV_RANK, B_HEADS * LANES)
    wv = wkv3[:, :, B_NOPE:].reshape(B_KV_RANK, B_WIDTH)
    sel = np.zeros((N_MAPS, (A_HEADS + B_HEADS) * LANES), np.float32)
    for hd in range(A_HEADS):
        sel[2 * hd, hd * LANES:hd * LANES + A_HEAD_DIM] = 1.0
        sel[2 * hd + 1, hd * LANES + A_HEAD_DIM:(hd + 1) * LANES] = 1.0
    for hd in range(B_HEADS):
        sel[2 * A_HEADS + hd, (A_HEADS + hd) * LANES:(A_HEADS + hd + 1) * LANES] = 1.0
    weights = (w_c.astype(_BF16), w_n.astype(_BF16), w_qv.astype(_BF16),
               q_a_norm.reshape(B_Q_RANK, 1), kv_a_norm.reshape(B_KV_RANK, 1),
               kv_a_norm.reshape(1, B_KV_RANK),
               wq.T.astype(_BF16), wv.T.astype(_BF16), wk.astype(_BF16), jnp.asarray(sel, _BF16))

    tables_nat, tables_t = _rope_tables(seq)
    qa, ka, va, qb, kb, vb, kn2, qn2 = _proj(x, ctx, mod3, weights, tables_nat, tables_t)

    cat = _attend(diff_lambda[0], diff_subln.reshape(LANES, 1), qa, ka, va, qb, kb, vb, kn2, qn2)

    wf = w_ffn_in[0]
    return _post(cat, x, mod3, w_out[0].astype(_BF16), wf[:, :FFN_HIDDEN].astype(_BF16),
                 wf[:, FFN_HIDDEN:].astype(_BF16), w_ffn_out[0].astype(_BF16), final_norm[None, :])
```

```python
import math

import numpy as np
import jax
import jax.numpy as jnp
from jax import lax
from jax.experimental import pallas as pl
from jax.experimental.pallas import tpu as pltpu

D_MODEL = 1024
CTX_LEN = 256
GRID_W = 64
ROPE_THETA = 10000.0
NORM_EPS = 1e-6

A_HEADS = 4
A_HEAD_DIM = 64
A_WIDTH = A_HEADS * 2 * A_HEAD_DIM
B_HEADS = 8
B_NOPE = 64
B_ROPE = 32
B_VDIM = 64
B_Q_RANK = 256
B_KV_RANK = 128
B_WIDTH = B_HEADS * B_VDIM
FFN_HIDDEN = 2816

LANES = 128
SUBLANES = 8
TOKEN_TILE = 256
ATT_TILE = 256
ADA_TILE = 1024
PROJ_ROWS = 4
POST_TILE = 512
POST_SUB = 256
V_PAD = 16
N_MAPS = 2 * A_HEADS + B_HEADS
VMEM_LIMIT = 56 * 1024 * 1024

LAM_INIT = 0.8 - 0.6 * math.exp(-0.3 * 0)
LOG2E = 1.4426950408889634
A_QSCALE = A_HEAD_DIM ** -0.5 * LOG2E
B_QSCALE = (B_NOPE + B_ROPE) ** -0.5 * LOG2E

FAST_LIMIT = 50.0
BOUND_MARGIN = 1.02
FAST_DEPTH = 1
EXACT_DEPTH = 2

_F32 = jnp.float32
_BF16 = jnp.bfloat16


def _dot(a, b):
    return jnp.dot(a, b, preferred_element_type=_F32)


def _dot_nt(a, b):
    return lax.dot_general(a, b, (((1,), (1,)), ((), ())), preferred_element_type=_F32)


def _rms(x, axis=-1):
    return x * lax.rsqrt(jnp.mean(x * x, axis=axis, keepdims=True) + NORM_EPS)


def _adaln_kernel(c_ref, w_ref, b_ref, o_ref):
    c = c_ref[...]
    s = c / (1.0 + jnp.exp(-c))
    o_ref[...] = _dot(s.astype(_BF16), w_ref[...].astype(_BF16)) + b_ref[...]


def _adaln(c_all, w_ada, b_ada):
    rows = c_all.shape[0]
    n = w_ada.shape[1]
    return pl.pallas_call(
        _adaln_kernel,
        grid=(n // ADA_TILE,),
        in_specs=[
            pl.BlockSpec((rows, D_MODEL), lambda i: (0, 0)),
            pl.BlockSpec((D_MODEL, ADA_TILE), lambda i: (0, i)),
            pl.BlockSpec((1, ADA_TILE), lambda i: (0, i)),
        ],
        out_specs=pl.BlockSpec((rows, ADA_TILE), lambda i: (0, i)),
        out_shape=jax.ShapeDtypeStruct((rows, n), _F32),
        name="adaln",
    )(c_all, w_ada, b_ada)


def _rope(x, cos, s_up, s_dn, quarter):
    return (x * cos + pltpu.roll(x, quarter, 1) * s_up
            + pltpu.roll(x, LANES - quarter, 1) * s_dn)


def _rope_t(x, cos_t, sin_t, lo, hi, quarter):
    parts = [x[:lo]] if lo else []
    for r in range(lo, hi, 2 * quarter):
        parts += [x[r + quarter:r + 2 * quarter], x[r:r + quarter]]
    if hi < x.shape[0]:
        parts.append(x[hi:])
    return x * cos_t + jnp.concatenate(parts, axis=0) * sin_t


def _proj_kernel(x_ref, ctx_ref, mod_ref, modc_ref, wc_ref, wn_ref, wqv_ref, qn_ref, kvn_col_ref,
                 kvn_row_ref, wqt_ref, wvt_ref, wk_ref, sel_ref,
                 ca_ref, ua_ref, da_ref, cb_ref, ub_ref, db_ref,
                 cat_ref, sat_ref, cbt_ref, sbt_ref,
                 qa_ref, ka_ref, va_ref, qb_ref, kb_ref, vb_ref, kn2_ref, qn2_ref):
    t = TOKEN_TILE
    chains = range(PROJ_ROWS)
    is_ctx = pl.program_id(1) == 0

    hs, hts = [], []
    for i in chains:
        xin = jnp.where(is_ctx, ctx_ref[i], x_ref[i])
        mod = jnp.where(is_ctx, modc_ref[0], mod_ref[i])
        shift = mod[:, 0:D_MODEL]
        scale = mod[:, D_MODEL:2 * D_MODEL]
        h32 = _rms(xin) * (1.0 + scale) + shift
        hs.append(h32.astype(_BF16))
        hts.append(hs[-1].T)

    ycs = [_dot(wc_ref[...], ht) for ht in hts]
    ys = [_dot(h, wn_ref[...]) for h in hs]
    yqvs = [_dot(wqv_ref[...], ht) for ht in hts]
    qbs = [_dot(wqt_ref[...], (_rms(yc[:B_Q_RANK], 0) * qn_ref[...]).astype(_BF16))
           for yc in ycs]
    vbs = [_dot(wvt_ref[...], (_rms(yc[B_Q_RANK:], 0) * kvn_col_ref[...]).astype(_BF16))
           for yc in ycs]
    kns = [_dot((_rms(y[:, A_WIDTH:A_WIDTH + B_KV_RANK]) * kvn_row_ref[...]).astype(_BF16), wk_ref[...])
           for y in ys]

    row16 = lax.broadcasted_iota(jnp.int32, (V_PAD, t), 0)
    ones_rows = jnp.where(row16 == 0, 1.0, 0.0).astype(_BF16)
    lane = lax.broadcasted_iota(jnp.int32, (t, LANES), 1)
    cat, sat = cat_ref[...], sat_ref[...]
    cbt, sbt = cbt_ref[...], sbt_ref[...]
    ca, ua, da = ca_ref[...], ua_ref[...], da_ref[...]
    cb, ub, db = cb_ref[...], ub_ref[...], db_ref[...]

    def col_sq_norm(q):
        return jnp.sum(q * q, axis=0, keepdims=True)

    def squares(k):
        return (k * k).astype(_BF16)

    for i in chains:
        yqv_t, qb_t, vb_t, y, kn = yqvs[i], qbs[i], vbs[i], ys[i], kns[i]
        qn2_rows = []
        for hd in range(A_HEADS):
            q_t = yqv_t[hd * LANES:(hd + 1) * LANES]
            q = _rope_t(q_t, cat, sat, 0, LANES, A_HEAD_DIM // 4)
            qa_ref[i, hd] = q.astype(_BF16)
            qn2_rows += [col_sq_norm(q[:A_HEAD_DIM]), col_sq_norm(q[A_HEAD_DIM:])]
            va_ref[i, hd, :LANES, :] = yqv_t[A_WIDTH + hd * LANES:A_WIDTH + (hd + 1) * LANES].astype(_BF16)
            va_ref[i, hd, LANES:, :] = ones_rows
        for hd in range(B_HEADS):
            blk = qb_t[hd * LANES:(hd + 1) * LANES]
            q = _rope_t(blk, cbt, sbt, B_NOPE, B_NOPE + B_ROPE, B_ROPE // 4)
            qb_ref[i, hd] = q.astype(_BF16)
            qn2_rows.append(col_sq_norm(q))
            vb_ref[i, hd, :B_VDIM, :] = vb_t[hd * B_VDIM:(hd + 1) * B_VDIM].astype(_BF16)
            vb_ref[i, hd, B_VDIM:, :] = ones_rows
        qn2_ref[i] = jnp.concatenate(qn2_rows, axis=0)

        ksq = []
        for hd in range(A_HEADS):
            k = _rope(y[:, hd * LANES:(hd + 1) * LANES], ca, ua, da, A_HEAD_DIM // 4)
            ka_ref[i, hd] = k.astype(_BF16)
            ksq.append(squares(k))
        kr_rot = _rope(y[:, A_WIDTH + B_KV_RANK:], cb, ub, db, B_ROPE // 4)
        for hd in range(B_HEADS):
            k = jnp.where(lane < B_NOPE, kn[:, hd * LANES:(hd + 1) * LANES], kr_rot)
            kb_ref[i, hd] = k.astype(_BF16)
            ksq.append(squares(k))
        kn2_ref[i] = _dot_nt(sel_ref[...], jnp.concatenate(ksq, axis=1))


def _proj(x, ctx, mod3, weights, tables_nat, tables_t):
    nb, seq, _ = x.shape
    n_tok = CTX_LEN + seq
    n_tiles = n_tok // TOKEN_TILE
    t = TOKEN_TILE

    def full(a):
        return pl.BlockSpec(a.shape, lambda b, j: (0,) * a.ndim)

    lat = lambda j: jnp.maximum(j - 1, 0)
    out_shapes = (
        jax.ShapeDtypeStruct((nb, A_HEADS, LANES, seq), _BF16),
        jax.ShapeDtypeStruct((nb, A_HEADS, n_tok, LANES), _BF16),
        jax.ShapeDtypeStruct((nb, A_HEADS, LANES + V_PAD, n_tok), _BF16),
        jax.ShapeDtypeStruct((nb, B_HEADS, LANES, seq), _BF16),
        jax.ShapeDtypeStruct((nb, B_HEADS, n_tok, LANES), _BF16),
        jax.ShapeDtypeStruct((nb, B_HEADS, B_VDIM + V_PAD, n_tok), _BF16),
        jax.ShapeDtypeStruct((nb, N_MAPS, n_tok), _F32),
        jax.ShapeDtypeStruct((nb, N_MAPS, seq), _F32),
    )
    r = PROJ_ROWS
    out_specs = (
        pl.BlockSpec((r, A_HEADS, LANES, t), lambda b, j: (b, 0, 0, lat(j))),
        pl.BlockSpec((r, A_HEADS, t, LANES), lambda b, j: (b, 0, j, 0)),
        pl.BlockSpec((r, A_HEADS, LANES + V_PAD, t), lambda b, j: (b, 0, 0, j)),
        pl.BlockSpec((r, B_HEADS, LANES, t), lambda b, j: (b, 0, 0, lat(j))),
        pl.BlockSpec((r, B_HEADS, t, LANES), lambda b, j: (b, 0, j, 0)),
        pl.BlockSpec((r, B_HEADS, B_VDIM + V_PAD, t), lambda b, j: (b, 0, 0, j)),
        pl.BlockSpec((r, N_MAPS, t), lambda b, j: (b, 0, j)),
        pl.BlockSpec((r, N_MAPS, t), lambda b, j: (b, 0, lat(j))),
    )
    return pl.pallas_call(
        _proj_kernel,
        grid=(nb // r, n_tiles),
        in_specs=[
            pl.BlockSpec((r, t, D_MODEL), lambda b, j: (b, lat(j), 0)),
            pl.BlockSpec((r, t, D_MODEL), lambda b, j: (b, 0, 0)),
            pl.BlockSpec((r, 1, 6 * D_MODEL), lambda b, j: (b, 0, 0)),
            pl.BlockSpec((1, 1, 6 * D_MODEL), lambda b, j: (nb, 0, 0)),
        ] + [full(w) for w in weights]
          + [pl.BlockSpec((t, LANES), lambda b, j: (j, 0))] * len(tables_nat)
          + [pl.BlockSpec((LANES, t), lambda b, j: (0, lat(j)))] * len(tables_t),
        out_specs=out_specs,
        out_shape=out_shapes,
        compiler_params=pltpu.CompilerParams(
            dimension_semantics=("arbitrary", "arbitrary"), vmem_limit_bytes=VMEM_LIMIT),
        name="proj",
    )(x, ctx, mod3, mod3, *weights, *tables_nat, *tables_t)


def _score_bound(qn2, kn2):
    return jnp.sqrt(qn2 * jnp.max(kn2, axis=-1, keepdims=True)) * BOUND_MARGIN


def _flags_kernel(kn2_ref, qn2_ref, o_ref):
    kn2 = kn2_ref[...]
    cols = []
    for i in range(qn2_ref.shape[2] // ATT_TILE):
        bound = _score_bound(qn2_ref[:, :, i * ATT_TILE:(i + 1) * ATT_TILE], kn2)
        worst = jnp.max(jnp.max(bound, axis=2), axis=1, keepdims=True)
        cols.append(jnp.where(worst <= FAST_LIMIT, 1, 0).astype(jnp.int32))
    o_ref[...] = jnp.concatenate(cols, axis=1)


def _flags(kn2, qn2):
    nb, _, seq = qn2.shape
    return pl.pallas_call(
        _flags_kernel,
        out_shape=jax.ShapeDtypeStruct((nb, seq // ATT_TILE), jnp.int32),
        name="flags",
    )(kn2, qn2)


def _attend_body(use_bound, depth, lam_ref, subln_ref, qa_ref, ka_ref, va_ref, qb_ref, kb_ref, vb_ref,
                 kn2_ref, qn2_ref, o_ref):
    t = ATT_TILE
    lp = lam_ref[...]
    lam = (jnp.exp(jnp.sum(lp[0:1] * lp[1:2], axis=-1, keepdims=True))
           - jnp.exp(jnp.sum(lp[2:3] * lp[3:4], axis=-1, keepdims=True)) + LAM_INIT)
    gain = subln_ref[...] * (1.0 - LAM_INIT)
    row = lax.broadcasted_iota(jnp.int32, (LANES, t), 0)
    lo = row < A_HEAD_DIM

    maps = []
    for hd in range(A_HEADS):
        q_t = qa_ref[0, hd]
        zero = jnp.zeros_like(q_t)
        for half in range(2):
            q_m = jnp.where(lo, q_t, zero) if half == 0 else jnp.where(lo, zero, q_t)
            maps.append((q_m, ka_ref.at[0, hd], va_ref.at[0, hd], LANES))
    for hd in range(B_HEADS):
        maps.append((qb_ref[0, hd], kb_ref.at[0, hd], vb_ref.at[0, hd], B_VDIM))

    def scores_t(i):
        q_m, k_ref, _, _ = maps[i]
        return _dot(k_ref[...], q_m)

    bounds = _score_bound(qn2_ref[0], kn2_ref[0]) if use_bound else None

    def finish(i, o_t, acc):
        if i < 2 * A_HEADS:
            hd, half = divmod(i, 2)
            if half == 0:
                acc[hd] = o_t
                return
            o = acc.pop(hd) - lam * o_t
            o = _rms(o, 0) * gain
            o_ref[0, :, hd * LANES:(hd + 1) * LANES] = o.T.astype(o_ref.dtype)
        else:
            hd = i - 2 * A_HEADS
            if hd % 2 == 0:
                acc[hd] = o_t
                return
            o = jnp.concatenate([acc.pop(hd - 1), o_t], axis=0)
            p = hd // 2
            o_ref[0, :, A_WIDTH + p * LANES:A_WIDTH + (p + 1) * LANES] = o.T.astype(o_ref.dtype)

    acc = {}

    def softmax_pv(i, s_t):
        _, _, v_ref, dv = maps[i]
        m = bounds[i:i + 1] if use_bound else jnp.max(s_t, axis=0, keepdims=True)
        p_t = jnp.exp2(s_t - m).astype(_BF16)
        o = _dot(v_ref[...], p_t)
        finish(i, o[:dv] / o[dv:dv + 1], acc)

    pending = []
    for i in range(len(maps)):
        pending.append((i, scores_t(i)))
        if len(pending) > depth:
            softmax_pv(*pending.pop(0))
    for item in pending:
        softmax_pv(*item)


def _attend_kernel(flags_ref, *refs):
    fast_ok = flags_ref[pl.program_id(0), pl.program_id(1)] != 0

    @pl.when(fast_ok)
    def _():
        _attend_body(True, FAST_DEPTH, *refs)

    @pl.when(jnp.logical_not(fast_ok))
    def _():
        _attend_body(False, EXACT_DEPTH, *refs)


def _attend(lam_p, subln_col, qa, ka, va, qb, kb, vb, kn2, qn2):
    nb, _, _, seq = qa.shape
    t = ATT_TILE

    def q_spec(nh):
        return pl.BlockSpec((1, nh, LANES, t), lambda b, i, flags: (b, 0, 0, i))

    def kv_spec(a):
        return pl.BlockSpec((1,) + a.shape[1:], lambda b, i, flags: (b,) + (0,) * (a.ndim - 1))

    grid_spec = pltpu.PrefetchScalarGridSpec(
        num_scalar_prefetch=1,
        grid=(nb, seq // t),
        in_specs=[
            pl.BlockSpec(lam_p.shape, lambda b, i, flags: (0, 0)),
            pl.BlockSpec(subln_col.shape, lambda b, i, flags: (0, 0)),
            q_spec(A_HEADS), kv_spec(ka), kv_spec(va),
            q_spec(B_HEADS), kv_spec(kb), kv_spec(vb), kv_spec(kn2),
            pl.BlockSpec((1, N_MAPS, t), lambda b, i, flags: (b, 0, i)),
        ],
        out_specs=pl.BlockSpec((1, t, A_WIDTH + B_WIDTH), lambda b, i, flags: (b, i, 0)),
    )
    return pl.pallas_call(
        _attend_kernel,
        grid_spec=grid_spec,
        out_shape=jax.ShapeDtypeStruct((nb, seq, A_WIDTH + B_WIDTH), _BF16),
        compiler_params=pltpu.CompilerParams(
            dimension_semantics=("arbitrary", "arbitrary"), vmem_limit_bytes=VMEM_LIMIT),
        name="attend",
    )(_flags(kn2, qn2), lam_p, subln_col, qa, ka, va, qb, kb, vb, kn2, qn2)


def _post_kernel(cat_ref, x_ref, mod_ref, wo_ref, wgu_ref, wd_ref, fn_ref, o_ref):
    mod = mod_ref[0]
    g_a = mod[:, 2 * D_MODEL:3 * D_MODEL]
    sh_f = mod[:, 3 * D_MODEL:4 * D_MODEL]
    sc_f = mod[:, 4 * D_MODEL:5 * D_MODEL]
    g_f = mod[:, 5 * D_MODEL:6 * D_MODEL]
    rows = [slice(r, r + POST_SUB) for r in range(0, POST_TILE, POST_SUB)]
    lats = [x_ref[0, r, :] + g_a * _dot(cat_ref[0, r, :], wo_ref[...]) for r in rows]
    hs = [(_rms(lat) * (1.0 + sc_f) + sh_f).astype(_BF16) for lat in lats]
    gates = [_dot(h, wgu_ref[:, :FFN_HIDDEN]) for h in hs]
    ups = [_dot(h, wgu_ref[:, FFN_HIDDEN:]) for h in hs]
    acts = [(g / (1.0 + jnp.exp(-g)) * u).astype(_BF16) for g, u in zip(gates, ups)]
    ffns = [_dot(a, wd_ref[...]) for a in acts]
    for r, lat, ffn in zip(rows, lats, ffns):
        o_ref[0, r, :] = _rms(lat + g_f * ffn) * fn_ref[...]


def _post(cat, x, mod3, wo, wgu, wd, fn):
    nb, seq, _ = x.shape
    t = POST_TILE

    def full(a):
        return pl.BlockSpec(a.shape, lambda b, i: (0,) * a.ndim, pipeline_mode=pl.Buffered(1))

    tok = pl.BlockSpec((1, t, D_MODEL), lambda b, i: (b, i, 0))
    return pl.pallas_call(
        _post_kernel,
        grid=(nb, seq // t),
        in_specs=[tok, tok, pl.BlockSpec((1, 1, 6 * D_MODEL), lambda b, i: (b, 0, 0)),
                  full(wo), full(wgu), full(wd), full(fn)],
        out_specs=tok,
        out_shape=jax.ShapeDtypeStruct((nb, seq, D_MODEL), _F32),
        compiler_params=pltpu.CompilerParams(
            dimension_semantics=("arbitrary", "arbitrary"), vmem_limit_bytes=VMEM_LIMIT),
        name="post",
    )(cat, x, mod3, wo, wgu, wd, fn)


def _rope_tables(seq):
    pos = np.arange(seq)
    row = (pos // GRID_W).astype(np.float32)
    col = (pos % GRID_W).astype(np.float32)
    lane = np.arange(LANES)

    def build(d, active, half, q_scale):
        quarter = half // 2
        inv = (ROPE_THETA ** (-(np.arange(quarter, dtype=np.float32)) / quarter)).astype(np.float32)
        p = np.where((d < half)[None, :], row[:, None], col[:, None]).astype(np.float32)
        ang = (p * inv[d % quarter][None, :]).astype(np.float32)
        cos = np.where(active[None, :], np.cos(ang), 1.0).astype(np.float32)
        sin = np.where(active[None, :], np.sin(ang), 0.0).astype(np.float32)
        upper = ((d % half) >= quarter)[None, :]
        s_up = np.where(upper, sin, 0.0).astype(np.float32)
        s_dn = np.where(upper, 0.0, -sin).astype(np.float32)
        ident = np.zeros((CTX_LEN, LANES), np.float32)
        nat = [jnp.asarray(np.concatenate([ident + fill, tab], 0))
               for tab, fill in ((cos, 1.0), (s_up, 0.0), (s_dn, 0.0))]
        trans = [jnp.asarray(np.ascontiguousarray(cos.T * np.float32(q_scale))),
                 jnp.asarray(np.ascontiguousarray((s_up + s_dn).T * np.float32(q_scale)))]
        return nat, trans

    nat_a, trans_a = build(lane % A_HEAD_DIM, np.ones(LANES, bool), A_HEAD_DIM // 2, A_QSCALE)
    in_rope = (lane >= B_NOPE) & (lane < B_NOPE + B_ROPE)
    nat_b, trans_b = build(np.where(in_rope, lane - B_NOPE, 0), in_rope, B_ROPE // 2, B_QSCALE)
    return nat_a + nat_b, trans_a + trans_b


def kernel(x, c, ctx, c_ctx, w_ada, b_ada, w_in, q_a_norm, kv_a_norm, w_q_up, w_kv_up,
           diff_lambda, diff_subln, w_out, w_ffn_in, w_ffn_out, final_norm):
    nb, seq, _ = x.shape
    assert w_ada.shape[0] == 1 and seq % ATT_TILE == 0 and ctx.shape[1] == CTX_LEN == TOKEN_TILE

    pad = (-(nb + 1)) % SUBLANES
    c_all = jnp.concatenate([c, c_ctx[None, :], jnp.zeros((pad, D_MODEL), _F32)], axis=0)
    mod = _adaln(c_all, w_ada[0], b_ada[0][None, :])
    mod3 = mod[:, None, :]

    w = w_in[0]
    o_k, o_v, o_cq = A_WIDTH, 2 * A_WIDTH, 3 * A_WIDTH
    o_ckv, o_kr = o_cq + B_Q_RANK, o_cq + B_Q_RANK + B_KV_RANK
    zeros = lambda n: jnp.zeros((D_MODEL, n), w.dtype)
    w_c = w[:, o_cq:o_kr].T
    w_qv = jnp.concatenate([w[:, :o_k], w[:, o_v:o_cq]], axis=1).T
    w_n = jnp.concatenate([w[:, o_k:o_v], w[:, o_ckv:o_kr], zeros(B_NOPE), w[:, o_kr:],
                           zeros(LANES - B_NOPE - B_ROPE)], axis=1)
    wq = jnp.pad(w_q_up[0].reshape(B_Q_RANK, B_HEADS, B_NOPE + B_ROPE),
                 ((0, 0), (0, 0), (0, LANES - B_NOPE - B_ROPE))).reshape(B_Q_RANK, B_HEADS * LANES)
    wkv3 = w_kv_up[0].reshape(B_KV_RANK, B_HEADS, B_NOPE + B_VDIM)
    wk = jnp.pad(wkv3[:, :, :B_NOPE], ((0, 0), (0, 0), (0, LANES - B_NOPE))).reshape(B_KV_RANK, B_HEADS * LANES)
    wv = wkv3[:, :, B_NOPE:].reshape(B_KV_RANK, B_WIDTH)
    sel = np.zeros((N_MAPS, (A_HEADS + B_HEADS) * LANES), np.float32)
    for hd in range(A_HEADS):
        sel[2 * hd, hd * LANES:hd * LANES + A_HEAD_DIM] = 1.0
        sel[2 * hd + 1, hd * LANES + A_HEAD_DIM:(hd + 1) * LANES] = 1.0
    for hd in range(B_HEADS):
        sel[2 * A_HEADS + hd, (A_HEADS + hd) * LANES:(A_HEADS + hd + 1) * LANES] = 1.0
    weights = (w_c.astype(_BF16), w_n.astype(_BF16), w_qv.astype(_BF16),
               q_a_norm.reshape(B_Q_RANK, 1), kv_a_norm.reshape(B_KV_RANK, 1),
               kv_a_norm.reshape(1, B_KV_RANK),
               wq.T.astype(_BF16), wv.T.astype(_BF16), wk.astype(_BF16), jnp.asarray(sel, _BF16))

    tables_nat, tables_t = _rope_tables(seq)
    qa, ka, va, qb, kb, vb, kn2, qn2 = _proj(x, ctx, mod3, weights, tables_nat, tables_t)

    cat = _attend(diff_lambda[0], diff_subln.reshape(LANES, 1), qa, ka, va, qb, kb, vb, kn2, qn2)

    return _post(cat, x, mod3, w_out[0].astype(_BF16), w_ffn_in[0].astype(_BF16),
                 w_ffn_out[0].astype(_BF16), final_norm[None, :])
```

```python
import math

import numpy as np
import jax
import jax.numpy as jnp
from jax import lax
from jax.experimental import pallas as pl
from jax.experimental.pallas import tpu as pltpu

D_MODEL = 1024
CTX_LEN = 256
GRID_W = 64
ROPE_THETA = 10000.0
NORM_EPS = 1e-6

A_HEADS = 4
A_HEAD_DIM = 64
A_WIDTH = A_HEADS * 2 * A_HEAD_DIM
B_HEADS = 8
B_NOPE = 64
B_ROPE = 32
B_VDIM = 64
B_Q_RANK = 256
B_KV_RANK = 128
B_WIDTH = B_HEADS * B_VDIM
FFN_HIDDEN = 2816

LANES = 128
SUBLANES = 8
TOKEN_TILE = 256
ATT_TILE = 256
ADA_TILE = 1024
PROJ_ROWS = 4
POST_TILE = 512
POST_SUB = 256
V_PAD = 16
N_MAPS = 2 * A_HEADS + B_HEADS
VMEM_LIMIT = 56 * 1024 * 1024

LAM_INIT = 0.8 - 0.6 * math.exp(-0.3 * 0)
LOG2E = 1.4426950408889634
A_QSCALE = A_HEAD_DIM ** -0.5 * LOG2E
B_QSCALE = (B_NOPE + B_ROPE) ** -0.5 * LOG2E

FAST_LIMIT = 50.0
BOUND_MARGIN = 1.02
FAST_DEPTH = 1
EXACT_DEPTH = 2

_F32 = jnp.float32
_BF16 = jnp.bfloat16


def _dot(a, b):
    return jnp.dot(a, b, preferred_element_type=_F32)


def _dot_nt(a, b):
    return lax.dot_general(a, b, (((1,), (1,)), ((), ())), preferred_element_type=_F32)


def _rms(x, axis=-1):
    return x * lax.rsqrt(jnp.mean(x * x, axis=axis, keepdims=True) + NORM_EPS)


def _adaln_kernel(c_ref, w_ref, b_ref, o_ref):
    c = c_ref[...]
    s = c / (1.0 + jnp.exp(-c))
    o_ref[...] = _dot(s.astype(_BF16), w_ref[...].astype(_BF16)) + b_ref[...]


def _adaln(c_all, w_ada, b_ada):
    rows = c_all.shape[0]
    n = w_ada.shape[1]
    return pl.pallas_call(
        _adaln_kernel,
        grid=(n // ADA_TILE,),
        in_specs=[
            pl.BlockSpec((rows, D_MODEL), lambda i: (0, 0)),
            pl.BlockSpec((D_MODEL, ADA_TILE), lambda i: (0, i)),
            pl.BlockSpec((1, ADA_TILE), lambda i: (0, i)),
        ],
        out_specs=pl.BlockSpec((rows, ADA_TILE), lambda i: (0, i)),
        out_shape=jax.ShapeDtypeStruct((rows, n), _F32),
        name="adaln",
    )(c_all, w_ada, b_ada)


def _rope(x, cos, s_up, s_dn, quarter):
    return (x * cos + pltpu.roll(x, quarter, 1) * s_up
            + pltpu.roll(x, LANES - quarter, 1) * s_dn)


def _rope_t(x, cos_t, sin_t, lo, hi, quarter):
    parts = [x[:lo]] if lo else []
    for r in range(lo, hi, 2 * quarter):
        parts += [x[r + quarter:r + 2 * quarter], x[r:r + quarter]]
    if hi < x.shape[0]:
        parts.append(x[hi:])
    return x * cos_t + jnp.concatenate(parts, axis=0) * sin_t


def _proj_kernel(x_ref, ctx_ref, mod_ref, modc_ref, wc_ref, wn_ref, wqv_ref, qn_ref, kvn_col_ref,
                 kvn_row_ref, wqt_ref, wvt_ref, wk_ref, sel_ref,
                 ca_ref, ua_ref, da_ref, cb_ref, ub_ref, db_ref,
                 cat_ref, sat_ref, cbt_ref, sbt_ref,
                 qa_ref, ka_ref, va_ref, qb_ref, kb_ref, vb_ref, kn2_ref, qn2_ref):
    t = TOKEN_TILE
    chains = range(PROJ_ROWS)
    is_ctx = pl.program_id(1) == 0

    hs, hts = [], []
    for i in chains:
        xin = jnp.where(is_ctx, ctx_ref[i], x_ref[i])
        mod = jnp.where(is_ctx, modc_ref[0], mod_ref[i])
        shift = mod[:, 0:D_MODEL]
        scale = mod[:, D_MODEL:2 * D_MODEL]
        h32 = _rms(xin) * (1.0 + scale) + shift
        hs.append(h32.astype(_BF16))
        hts.append(hs[-1].T)

    ycs = [_dot(wc_ref[...], ht) for ht in hts]
    ys = [_dot(h, wn_ref[...]) for h in hs]
    yqvs = [_dot(wqv_ref[...], ht) for ht in hts]
    qbs = [_dot(wqt_ref[...], (_rms(yc[:B_Q_RANK], 0) * qn_ref[...]).astype(_BF16))
           for yc in ycs]
    vbs = [_dot(wvt_ref[...], (_rms(yc[B_Q_RANK:], 0) * kvn_col_ref[...]).astype(_BF16))
           for yc in ycs]
    kns = [_dot((_rms(y[:, A_WIDTH:A_WIDTH + B_KV_RANK]) * kvn_row_ref[...]).astype(_BF16), wk_ref[...])
           for y in ys]

    row16 = lax.broadcasted_iota(jnp.int32, (V_PAD, t), 0)
    ones_rows = jnp.where(row16 == 0, 1.0, 0.0).astype(_BF16)
    lane = lax.broadcasted_iota(jnp.int32, (t, LANES), 1)
    cat, sat = cat_ref[...], sat_ref[...]
    cbt, sbt = cbt_ref[...], sbt_ref[...]
    ca, ua, da = ca_ref[...], ua_ref[...], da_ref[...]
    cb, ub, db = cb_ref[...], ub_ref[...], db_ref[...]

    def col_sq_norm(q):
        return jnp.sum(q * q, axis=0, keepdims=True)

    def squares(k):
        return (k * k).astype(_BF16)

    for i in chains:
        yqv_t, qb_t, vb_t, y, kn = yqvs[i], qbs[i], vbs[i], ys[i], kns[i]
        qn2_rows = []
        for hd in range(A_HEADS):
            q_t = yqv_t[hd * LANES:(hd + 1) * LANES]
            q = _rope_t(q_t, cat, sat, 0, LANES, A_HEAD_DIM // 4)
            qa_ref[i, hd] = q.astype(_BF16)
            qn2_rows += [col_sq_norm(q[:A_HEAD_DIM]), col_sq_norm(q[A_HEAD_DIM:])]
            va_ref[i, hd] = yqv_t[A_WIDTH + hd * LANES:A_WIDTH + (hd + 1) * LANES].astype(_BF16)
        for hd in range(B_HEADS):
            blk = qb_t[hd * LANES:(hd + 1) * LANES]
            q = _rope_t(blk, cbt, sbt, B_NOPE, B_NOPE + B_ROPE, B_ROPE // 4)
            qb_ref[i, hd] = q.astype(_BF16)
            qn2_rows.append(col_sq_norm(q))
            vb_ref[i, hd, :B_VDIM, :] = vb_t[hd * B_VDIM:(hd + 1) * B_VDIM].astype(_BF16)
            vb_ref[i, hd, B_VDIM:, :] = ones_rows
        qn2_ref[i] = jnp.concatenate(qn2_rows, axis=0)

        ksq = []
        for hd in range(A_HEADS):
            k = _rope(y[:, hd * LANES:(hd + 1) * LANES], ca, ua, da, A_HEAD_DIM // 4)
            ka_ref[i, hd] = k.astype(_BF16)
            ksq.append(squares(k))
        kr_rot = _rope(y[:, A_WIDTH + B_KV_RANK:], cb, ub, db, B_ROPE // 4)
        for hd in range(B_HEADS):
            k = jnp.where(lane < B_NOPE, kn[:, hd * LANES:(hd + 1) * LANES], kr_rot)
            kb_ref[i, hd] = k.astype(_BF16)
            ksq.append(squares(k))
        kn2_ref[i] = _dot_nt(sel_ref[...], jnp.concatenate(ksq, axis=1))


def _proj(x, ctx, mod3, weights, tables_nat, tables_t):
    nb, seq, _ = x.shape
    n_tok = CTX_LEN + seq
    n_tiles = n_tok // TOKEN_TILE
    t = TOKEN_TILE

    def full(a):
        return pl.BlockSpec(a.shape, lambda b, j: (0,) * a.ndim)

    lat = lambda j: jnp.maximum(j - 1, 0)
    out_shapes = (
        jax.ShapeDtypeStruct((nb, A_HEADS, LANES, seq), _BF16),
        jax.ShapeDtypeStruct((nb, A_HEADS, n_tok, LANES), _BF16),
        jax.ShapeDtypeStruct((nb, A_HEADS, LANES, n_tok), _BF16),
        jax.ShapeDtypeStruct((nb, B_HEADS, LANES, seq), _BF16),
        jax.ShapeDtypeStruct((nb, B_HEADS, n_tok, LANES), _BF16),
        jax.ShapeDtypeStruct((nb, B_HEADS, B_VDIM + V_PAD, n_tok), _BF16),
        jax.ShapeDtypeStruct((nb, N_MAPS, n_tok), _F32),
        jax.ShapeDtypeStruct((nb, N_MAPS, seq), _F32),
    )
    r = PROJ_ROWS
    out_specs = (
        pl.BlockSpec((r, A_HEADS, LANES, t), lambda b, j: (b, 0, 0, lat(j))),
        pl.BlockSpec((r, A_HEADS, t, LANES), lambda b, j: (b, 0, j, 0)),
        pl.BlockSpec((r, A_HEADS, LANES, t), lambda b, j: (b, 0, 0, j)),
        pl.BlockSpec((r, B_HEADS, LANES, t), lambda b, j: (b, 0, 0, lat(j))),
        pl.BlockSpec((r, B_HEADS, t, LANES), lambda b, j: (b, 0, j, 0)),
        pl.BlockSpec((r, B_HEADS, B_VDIM + V_PAD, t), lambda b, j: (b, 0, 0, j)),
        pl.BlockSpec((r, N_MAPS, t), lambda b, j: (b, 0, j)),
        pl.BlockSpec((r, N_MAPS, t), lambda b, j: (b, 0, lat(j))),
    )
    return pl.pallas_call(
        _proj_kernel,
        grid=(nb // r, n_tiles),
        in_specs=[
            pl.BlockSpec((r, t, D_MODEL), lambda b, j: (b, lat(j), 0)),
            pl.BlockSpec((r, t, D_MODEL), lambda b, j: (b, 0, 0)),
            pl.BlockSpec((r, 1, 6 * D_MODEL), lambda b, j: (b, 0, 0)),
            pl.BlockSpec((1, 1, 6 * D_MODEL), lambda b, j: (nb, 0, 0)),
        ] + [full(w) for w in weights]
          + [pl.BlockSpec((t, LANES), lambda b, j: (j, 0))] * len(tables_nat)
          + [pl.BlockSpec((LANES, t), lambda b, j: (0, lat(j)))] * len(tables_t),
        out_specs=out_specs,
        out_shape=out_shapes,
        compiler_params=pltpu.CompilerParams(
            dimension_semantics=("arbitrary", "arbitrary"), vmem_limit_bytes=VMEM_LIMIT),
        name="proj",
    )(x, ctx, mod3, mod3, *weights, *tables_nat, *tables_t)


def _score_bound(qn2, kn2):
    return jnp.sqrt(qn2 * jnp.max(kn2, axis=-1, keepdims=True)) * BOUND_MARGIN


def _flags_kernel(kn2_ref, qn2_ref, o_ref):
    kn2 = kn2_ref[...]
    cols = []
    for i in range(qn2_ref.shape[2] // ATT_TILE):
        bound = _score_bound(qn2_ref[:, :, i * ATT_TILE:(i + 1) * ATT_TILE], kn2)
        worst = jnp.max(jnp.max(bound, axis=2), axis=1, keepdims=True)
        cols.append(jnp.where(worst <= FAST_LIMIT, 1, 0).astype(jnp.int32))
    o_ref[...] = jnp.concatenate(cols, axis=1)


def _flags(kn2, qn2):
    nb, _, seq = qn2.shape
    return pl.pallas_call(
        _flags_kernel,
        out_shape=jax.ShapeDtypeStruct((nb, seq // ATT_TILE), jnp.int32),
        name="flags",
    )(kn2, qn2)


def _attend_body(use_bound, depth, lam_ref, subln_ref, qa_ref, ka_ref, va_ref, qb_ref, kb_ref, vb_ref,
                 kn2_ref, qn2_ref, o_ref):
    t = ATT_TILE
    lp = lam_ref[...]
    lam = (jnp.exp(jnp.sum(lp[0:1] * lp[1:2], axis=-1, keepdims=True))
           - jnp.exp(jnp.sum(lp[2:3] * lp[3:4], axis=-1, keepdims=True)) + LAM_INIT)
    gain = subln_ref[...] * (1.0 - LAM_INIT)
    row = lax.broadcasted_iota(jnp.int32, (LANES, t), 0)
    lo = row < A_HEAD_DIM

    maps = []
    for hd in range(A_HEADS):
        q_t = qa_ref[0, hd]
        zero = jnp.zeros_like(q_t)
        for half in range(2):
            q_m = jnp.where(lo, q_t, zero) if half == 0 else jnp.where(lo, zero, q_t)
            maps.append((q_m, ka_ref.at[0, hd], va_ref.at[0, hd], LANES))
    for hd in range(B_HEADS):
        maps.append((qb_ref[0, hd], kb_ref.at[0, hd], vb_ref.at[0, hd], B_VDIM))

    def scores_t(i):
        q_m, k_ref, _, _ = maps[i]
        return _dot(k_ref[...], q_m)

    bounds = _score_bound(qn2_ref[0], kn2_ref[0]) if use_bound else None

    def finish(i, o_t, acc):
        if i < 2 * A_HEADS:
            hd, half = divmod(i, 2)
            if half == 0:
                acc[hd] = o_t
                return
            o = acc.pop(hd) - lam * o_t
            o = _rms(o, 0) * gain
            o_ref[0, :, hd * LANES:(hd + 1) * LANES] = o.T.astype(o_ref.dtype)
        else:
            hd = i - 2 * A_HEADS
            if hd % 2 == 0:
                acc[hd] = o_t
                return
            o = jnp.concatenate([acc.pop(hd - 1), o_t], axis=0)
            p = hd // 2
            o_ref[0, :, A_WIDTH + p * LANES:A_WIDTH + (p + 1) * LANES] = o.T.astype(o_ref.dtype)

    acc = {}

    def softmax_pv(i, s_t):
        _, _, v_ref, dv = maps[i]
        m = bounds[i:i + 1] if use_bound else jnp.max(s_t, axis=0, keepdims=True)
        p = jnp.exp2(s_t - m)
        if v_ref.shape[0] == dv:
            o = _dot(v_ref[...], p.astype(_BF16)) / jnp.sum(p, axis=0, keepdims=True)
        else:
            o = _dot(v_ref[...], p.astype(_BF16))
            o = o[:dv] / o[dv:dv + 1]
        finish(i, o, acc)

    pending = []
    for i in range(len(maps)):
        pending.append((i, scores_t(i)))
        if len(pending) > depth:
            softmax_pv(*pending.pop(0))
    for item in pending:
        softmax_pv(*item)


def _attend_kernel(flags_ref, *refs):
    fast_ok = flags_ref[pl.program_id(0), pl.program_id(1)] != 0

    @pl.when(fast_ok)
    def _():
        _attend_body(True, FAST_DEPTH, *refs)

    @pl.when(jnp.logical_not(fast_ok))
    def _():
        _attend_body(False, EXACT_DEPTH, *refs)


def _attend(lam_p, subln_col, qa, ka, va, qb, kb, vb, kn2, qn2):
    nb, _, _, seq = qa.shape
    t = ATT_TILE

    def q_spec(nh):
        return pl.BlockSpec((1, nh, LANES, t), lambda b, i, flags: (b, 0, 0, i))

    def kv_spec(a):
        return pl.BlockSpec((1,) + a.shape[1:], lambda b, i, flags: (b,) + (0,) * (a.ndim - 1))

    grid_spec = pltpu.PrefetchScalarGridSpec(
        num_scalar_prefetch=1,
        grid=(nb, seq // t),
        in_specs=[
            pl.BlockSpec(lam_p.shape, lambda b, i, flags: (0, 0)),
            pl.BlockSpec(subln_col.shape, lambda b, i, flags: (0, 0)),
            q_spec(A_HEADS), kv_spec(ka), kv_spec(va),
            q_spec(B_HEADS), kv_spec(kb), kv_spec(vb), kv_spec(kn2),
            pl.BlockSpec((1, N_MAPS, t), lambda b, i, flags: (b, 0, i)),
        ],
        out_specs=pl.BlockSpec((1, t, A_WIDTH + B_WIDTH), lambda b, i, flags: (b, i, 0)),
    )
    return pl.pallas_call(
        _attend_kernel,
        grid_spec=grid_spec,
        out_shape=jax.ShapeDtypeStruct((nb, seq, A_WIDTH + B_WIDTH), _BF16),
        compiler_params=pltpu.CompilerParams(
            dimension_semantics=("arbitrary", "arbitrary"), vmem_limit_bytes=VMEM_LIMIT),
        name="attend",
    )(_flags(kn2, qn2), lam_p, subln_col, qa, ka, va, qb, kb, vb, kn2, qn2)


def _post_kernel(cat_ref, x_ref, mod_ref, wo_ref, wgu_ref, wd_ref, fn_ref, o_ref):
    mod = mod_ref[0]
    g_a = mod[:, 2 * D_MODEL:3 * D_MODEL]
    sh_f = mod[:, 3 * D_MODEL:4 * D_MODEL]
    sc_f = mod[:, 4 * D_MODEL:5 * D_MODEL]
    g_f = mod[:, 5 * D_MODEL:6 * D_MODEL]
    rows = [slice(r, r + POST_SUB) for r in range(0, POST_TILE, POST_SUB)]
    lats = [x_ref[0, r, :] + g_a * _dot(cat_ref[0, r, :], wo_ref[...]) for r in rows]
    hs = [(_rms(lat) * (1.0 + sc_f) + sh_f).astype(_BF16) for lat in lats]
    gates = [_dot(h, wgu_ref[:, :FFN_HIDDEN]) for h in hs]
    ups = [_dot(h, wgu_ref[:, FFN_HIDDEN:]) for h in hs]
    acts = [(g / (1.0 + jnp.exp(-g)) * u).astype(_BF16) for g, u in zip(gates, ups)]
    ffns = [_dot(a, wd_ref[...]) for a in acts]
    for r, lat, ffn in zip(rows, lats, ffns):
        o_ref[0, r, :] = _rms(lat + g_f * ffn) * fn_ref[...]


def _post(cat, x, mod3, wo, wgu, wd, fn):
    nb, seq, _ = x.shape
    t = POST_TILE

    def full(a):
        return pl.BlockSpec(a.shape, lambda b, i: (0,) * a.ndim, pipeline_mode=pl.Buffered(1))

    tok = pl.BlockSpec((1, t, D_MODEL), lambda b, i: (b, i, 0))
    return pl.pallas_call(
        _post_kernel,
        grid=(nb, seq // t),
        in_specs=[tok, tok, pl.BlockSpec((1, 1, 6 * D_MODEL), lambda b, i: (b, 0, 0)),
                  full(wo), full(wgu), full(wd), full(fn)],
        out_specs=tok,
        out_shape=jax.ShapeDtypeStruct((nb, seq, D_MODEL), _F32),
        compiler_params=pltpu.CompilerParams(
            dimension_semantics=("arbitrary", "arbitrary"), vmem_limit_bytes=VMEM_LIMIT),
        name="post",
    )(cat, x, mod3, wo, wgu, wd, fn)


def _rope_tables(seq):
    pos = np.arange(seq)
    row = (pos // GRID_W).astype(np.float32)
    col = (pos % GRID_W).astype(np.float32)
    lane = np.arange(LANES)

    def build(d, active, half, q_scale):
        quarter = half // 2
        inv = (ROPE_THETA ** (-(np.arange(quarter, dtype=np.float32)) / quarter)).astype(np.float32)
        p = np.where((d < half)[None, :], row[:, None], col[:, None]).astype(np.float32)
        ang = (p * inv[d % quarter][None, :]).astype(np.float32)
        cos = np.where(active[None, :], np.cos(ang), 1.0).astype(np.float32)
        sin = np.where(active[None, :], np.sin(ang), 0.0).astype(np.float32)
        upper = ((d % half) >= quarter)[None, :]
        s_up = np.where(upper, sin, 0.0).astype(np.float32)
        s_dn = np.where(upper, 0.0, -sin).astype(np.float32)
        ident = np.zeros((CTX_LEN, LANES), np.float32)
        nat = [jnp.asarray(np.concatenate([ident + fill, tab], 0))
               for tab, fill in ((cos, 1.0), (s_up, 0.0), (s_dn, 0.0))]
        trans = [jnp.asarray(np.ascontiguousarray(cos.T * np.float32(q_scale))),
                 jnp.asarray(np.ascontiguousarray((s_up + s_dn).T * np.float32(q_scale)))]
        return nat, trans

    nat_a, trans_a = build(lane % A_HEAD_DIM, np.ones(LANES, bool), A_HEAD_DIM // 2, A_QSCALE)
    in_rope = (lane >= B_NOPE) & (lane < B_NOPE + B_ROPE)
    nat_b, trans_b = build(np.where(in_rope, lane - B_NOPE, 0), in_rope, B_ROPE // 2, B_QSCALE)
    return nat_a + nat_b, trans_a + trans_b


def kernel(x, c, ctx, c_ctx, w_ada, b_ada, w_in, q_a_norm, kv_a_norm, w_q_up, w_kv_up,
           diff_lambda, diff_subln, w_out, w_ffn_in, w_ffn_out, final_norm):
    nb, seq, _ = x.shape
    assert w_ada.shape[0] == 1 and seq % ATT_TILE == 0 and ctx.shape[1] == CTX_LEN == TOKEN_TILE

    pad = (-(nb + 1)) % SUBLANES
    c_all = jnp.concatenate([c, c_ctx[None, :], jnp.zeros((pad, D_MODEL), _F32)], axis=0)
    mod = _adaln(c_all, w_ada[0], b_ada[0][None, :])
    mod3 = mod[:, None, :]

    w = w_in[0]
    o_k, o_v, o_cq = A_WIDTH, 2 * A_WIDTH, 3 * A_WIDTH
    o_ckv, o_kr = o_cq + B_Q_RANK, o_cq + B_Q_RANK + B_KV_RANK
    zeros = lambda n: jnp.zeros((D_MODEL, n), w.dtype)
    w_c = w[:, o_cq:o_kr].T
    w_qv = jnp.concatenate([w[:, :o_k], w[:, o_v:o_cq]], axis=1).T
    w_n = jnp.concatenate([w[:, o_k:o_v], w[:, o_ckv:o_kr], zeros(B_NOPE), w[:, o_kr:],
                           zeros(LANES - B_NOPE - B_ROPE)], axis=1)
    wq = jnp.pad(w_q_up[0].reshape(B_Q_RANK, B_HEADS, B_NOPE + B_ROPE),
                 ((0, 0), (0, 0), (0, LANES - B_NOPE - B_ROPE))).reshape(B_Q_RANK, B_HEADS * LANES)
    wkv3 = w_kv_up[0].reshape(B_KV_RANK, B_HEADS, B_NOPE + B_VDIM)
    wk = jnp.pad(wkv3[:, :, :B_NOPE], ((0, 0), (0, 0), (0, LANES - B_NOPE))).reshape(B_KV_RANK, B_HEADS * LANES)
    wv = wkv3[:, :, B_NOPE:].reshape(B_KV_RANK, B_WIDTH)
    sel = np.zeros((N_MAPS, (A_HEADS + B_HEADS) * LANES), np.float32)
    for hd in range(A_HEADS):
        sel[2 * hd, hd * LANES:hd * LANES + A_HEAD_DIM] = 1.0
        sel[2 * hd + 1, hd * LANES + A_HEAD_DIM:(hd + 1) * LANES] = 1.0
    for hd in range(B_HEADS):
        sel[2 * A_HEADS + hd, (A_HEADS + hd) * LANES:(A_HEADS + hd + 1) * LANES] = 1.0
    weights = (w_c.astype(_BF16), w_n.astype(_BF16), w_qv.astype(_BF16),
               q_a_norm.reshape(B_Q_RANK, 1), kv_a_norm.reshape(B_KV_RANK, 1),
               kv_a_norm.reshape(1, B_KV_RANK),
               wq.T.astype(_BF16), wv.T.astype(_BF16), wk.astype(_BF16), jnp.asarray(sel, _BF16))

    tables_nat, tables_t = _rope_tables(seq)
    qa, ka, va, qb, kb, vb, kn2, qn2 = _proj(x, ctx, mod3, weights, tables_nat, tables_t)

    cat = _attend(diff_lambda[0], diff_subln.reshape(LANES, 1), qa, ka, va, qb, kb, vb, kn2, qn2)

    return _post(cat, x, mod3, w_out[0].astype(_BF16), w_ffn_in[0].astype(_BF16),
                 w_ffn_out[0].astype(_BF16), final_norm[None, :])
```

```python
import math

import numpy as np
import jax
import jax.numpy as jnp
from jax import lax
from jax.experimental import pallas as pl
from jax.experimental.pallas import tpu as pltpu

D_MODEL = 1024
CTX_LEN = 256
GRID_W = 64
ROPE_THETA = 10000.0
NORM_EPS = 1e-6

A_HEADS = 4
A_HEAD_DIM = 64
A_WIDTH = A_HEADS * 2 * A_HEAD_DIM
B_HEADS = 8
B_NOPE = 64
B_ROPE = 32
B_VDIM = 64
B_Q_RANK = 256
B_KV_RANK = 128
B_WIDTH = B_HEADS * B_VDIM
FFN_HIDDEN = 2816

LANES = 128
SUBLANES = 8
TOKEN_TILE = 256
ATT_TILE = 256
ADA_TILE = 1024
PROJ_ROWS = 4
POST_TILE = 512
POST_SUB = 256
N_MAPS = 2 * A_HEADS + B_HEADS
VMEM_LIMIT = 56 * 1024 * 1024

LAM_INIT = 0.8 - 0.6 * math.exp(-0.3 * 0)
LOG2E = 1.4426950408889634
A_QSCALE = A_HEAD_DIM ** -0.5 * LOG2E
B_QSCALE = (B_NOPE + B_ROPE) ** -0.5 * LOG2E

FAST_LIMIT = 50.0
BOUND_MARGIN = 1.02
FAST_DEPTH = 1
EXACT_DEPTH = 2

_F32 = jnp.float32
_BF16 = jnp.bfloat16


def _dot(a, b):
    return jnp.dot(a, b, preferred_element_type=_F32)


def _dot_nt(a, b):
    return lax.dot_general(a, b, (((1,), (1,)), ((), ())), preferred_element_type=_F32)


def _rms(x, axis=-1):
    return x * lax.rsqrt(jnp.mean(x * x, axis=axis, keepdims=True) + NORM_EPS)


def _adaln_kernel(c_ref, w_ref, b_ref, o_ref):
    c = c_ref[...]
    s = c / (1.0 + jnp.exp(-c))
    o_ref[...] = _dot(s.astype(_BF16), w_ref[...].astype(_BF16)) + b_ref[...]


def _adaln(c_all, w_ada, b_ada):
    rows = c_all.shape[0]
    n = w_ada.shape[1]
    return pl.pallas_call(
        _adaln_kernel,
        grid=(n // ADA_TILE,),
        in_specs=[
            pl.BlockSpec((rows, D_MODEL), lambda i: (0, 0)),
            pl.BlockSpec((D_MODEL, ADA_TILE), lambda i: (0, i)),
            pl.BlockSpec((1, ADA_TILE), lambda i: (0, i)),
        ],
        out_specs=pl.BlockSpec((rows, ADA_TILE), lambda i: (0, i)),
        out_shape=jax.ShapeDtypeStruct((rows, n), _F32),
        name="adaln",
    )(c_all, w_ada, b_ada)


def _rope(x, cos, s_up, s_dn, quarter):
    return (x * cos + pltpu.roll(x, quarter, 1) * s_up
            + pltpu.roll(x, LANES - quarter, 1) * s_dn)


def _rope_t(x, cos_t, sin_t, lo, hi, quarter):
    parts = [x[:lo]] if lo else []
    for r in range(lo, hi, 2 * quarter):
        parts += [x[r + quarter:r + 2 * quarter], x[r:r + quarter]]
    if hi < x.shape[0]:
        parts.append(x[hi:])
    return x * cos_t + jnp.concatenate(parts, axis=0) * sin_t


def _proj_kernel(x_ref, ctx_ref, mod_ref, modc_ref, wc_ref, wn_ref, wqv_ref, qn_ref, kvn_col_ref,
                 kvn_row_ref, wqt_ref, wvt_ref, wk_ref, sel_ref,
                 ca_ref, ua_ref, da_ref, cb_ref, ub_ref, db_ref,
                 cat_ref, sat_ref, cbt_ref, sbt_ref,
                 qa_ref, ka_ref, va_ref, qb_ref, kb_ref, vb_ref, kn2_ref, qn2_ref):
    t = TOKEN_TILE
    chains = range(PROJ_ROWS)
    is_ctx = pl.program_id(1) == 0

    hs, hts = [], []
    for i in chains:
        xin = jnp.where(is_ctx, ctx_ref[i], x_ref[i])
        mod = jnp.where(is_ctx, modc_ref[0], mod_ref[i])
        shift = mod[:, 0:D_MODEL]
        scale = mod[:, D_MODEL:2 * D_MODEL]
        h32 = _rms(xin) * (1.0 + scale) + shift
        hs.append(h32.astype(_BF16))
        hts.append(hs[-1].T)

    ycs = [_dot(wc_ref[...], ht) for ht in hts]
    ys = [_dot(h, wn_ref[...]) for h in hs]
    yqvs = [_dot(wqv_ref[...], ht) for ht in hts]
    qbs = [_dot(wqt_ref[...], (_rms(yc[:B_Q_RANK], 0) * qn_ref[...]).astype(_BF16))
           for yc in ycs]
    vbs = [_dot(wvt_ref[...], (_rms(yc[B_Q_RANK:], 0) * kvn_col_ref[...]).astype(_BF16))
           for yc in ycs]
    kns = [_dot((_rms(y[:, A_WIDTH:A_WIDTH + B_KV_RANK]) * kvn_row_ref[...]).astype(_BF16), wk_ref[...])
           for y in ys]

    lane = lax.broadcasted_iota(jnp.int32, (t, LANES), 1)
    cat, sat = cat_ref[...], sat_ref[...]
    cbt, sbt = cbt_ref[...], sbt_ref[...]
    ca, ua, da = ca_ref[...], ua_ref[...], da_ref[...]
    cb, ub, db = cb_ref[...], ub_ref[...], db_ref[...]

    def col_sq_norm(q):
        return jnp.sum(q * q, axis=0, keepdims=True)

    def squares(k):
        return (k * k).astype(_BF16)

    for i in chains:
        yqv_t, qb_t, vb_t, y, kn = yqvs[i], qbs[i], vbs[i], ys[i], kns[i]
        qn2_rows = []
        for hd in range(A_HEADS):
            q_t = yqv_t[hd * LANES:(hd + 1) * LANES]
            q = _rope_t(q_t, cat, sat, 0, LANES, A_HEAD_DIM // 4)
            qa_ref[i, hd] = q.astype(_BF16)
            qn2_rows += [col_sq_norm(q[:A_HEAD_DIM]), col_sq_norm(q[A_HEAD_DIM:])]
            va_ref[i, hd] = yqv_t[A_WIDTH + hd * LANES:A_WIDTH + (hd + 1) * LANES].astype(_BF16)
        for hd in range(B_HEADS):
            blk = qb_t[hd * LANES:(hd + 1) * LANES]
            q = _rope_t(blk, cbt, sbt, B_NOPE, B_NOPE + B_ROPE, B_ROPE // 4)
            qb_ref[i, hd] = q.astype(_BF16)
            qn2_rows.append(col_sq_norm(q))
            vb_ref[i, hd] = vb_t[hd * B_VDIM:(hd + 1) * B_VDIM].astype(_BF16)
        qn2_ref[i] = jnp.concatenate(qn2_rows, axis=0)

        ksq = []
        for hd in range(A_HEADS):
            k = _rope(y[:, hd * LANES:(hd + 1) * LANES], ca, ua, da, A_HEAD_DIM // 4)
            ka_ref[i, hd] = k.astype(_BF16)
            ksq.append(squares(k))
        kr_rot = _rope(y[:, A_WIDTH + B_KV_RANK:], cb, ub, db, B_ROPE // 4)
        for hd in range(B_HEADS):
            k = jnp.where(lane < B_NOPE, kn[:, hd * LANES:(hd + 1) * LANES], kr_rot)
            kb_ref[i, hd] = k.astype(_BF16)
            ksq.append(squares(k))
        kn2_ref[i] = _dot_nt(sel_ref[...], jnp.concatenate(ksq, axis=1))


def _proj(x, ctx, mod3, weights, tables_nat, tables_t):
    nb, seq, _ = x.shape
    n_tok = CTX_LEN + seq
    n_tiles = n_tok // TOKEN_TILE
    t = TOKEN_TILE

    def full(a):
        return pl.BlockSpec(a.shape, lambda b, j: (0,) * a.ndim)

    lat = lambda j: jnp.maximum(j - 1, 0)
    out_shapes = (
        jax.ShapeDtypeStruct((nb, A_HEADS, LANES, seq), _BF16),
        jax.ShapeDtypeStruct((nb, A_HEADS, n_tok, LANES), _BF16),
        jax.ShapeDtypeStruct((nb, A_HEADS, LANES, n_tok), _BF16),
        jax.ShapeDtypeStruct((nb, B_HEADS, LANES, seq), _BF16),
        jax.ShapeDtypeStruct((nb, B_HEADS, n_tok, LANES), _BF16),
        jax.ShapeDtypeStruct((nb, B_HEADS, B_VDIM, n_tok), _BF16),
        jax.ShapeDtypeStruct((nb, N_MAPS, n_tok), _F32),
        jax.ShapeDtypeStruct((nb, N_MAPS, seq), _F32),
    )
    r = PROJ_ROWS
    out_specs = (
        pl.BlockSpec((r, A_HEADS, LANES, t), lambda b, j: (b, 0, 0, lat(j))),
        pl.BlockSpec((r, A_HEADS, t, LANES), lambda b, j: (b, 0, j, 0)),
        pl.BlockSpec((r, A_HEADS, LANES, t), lambda b, j: (b, 0, 0, j)),
        pl.BlockSpec((r, B_HEADS, LANES, t), lambda b, j: (b, 0, 0, lat(j))),
        pl.BlockSpec((r, B_HEADS, t, LANES), lambda b, j: (b, 0, j, 0)),
        pl.BlockSpec((r, B_HEADS, B_VDIM, t), lambda b, j: (b, 0, 0, j)),
        pl.BlockSpec((r, N_MAPS, t), lambda b, j: (b, 0, j)),
        pl.BlockSpec((r, N_MAPS, t), lambda b, j: (b, 0, lat(j))),
    )
    return pl.pallas_call(
        _proj_kernel,
        grid=(nb // r, n_tiles),
        in_specs=[
            pl.BlockSpec((r, t, D_MODEL), lambda b, j: (b, lat(j), 0)),
            pl.BlockSpec((r, t, D_MODEL), lambda b, j: (b, 0, 0)),
            pl.BlockSpec((r, 1, 6 * D_MODEL), lambda b, j: (b, 0, 0)),
            pl.BlockSpec((1, 1, 6 * D_MODEL), lambda b, j: (nb, 0, 0)),
        ] + [full(w) for w in weights]
          + [pl.BlockSpec((t, LANES), lambda b, j: (j, 0))] * len(tables_nat)
          + [pl.BlockSpec((LANES, t), lambda b, j: (0, lat(j)))] * len(tables_t),
        out_specs=out_specs,
        out_shape=out_shapes,
        compiler_params=pltpu.CompilerParams(
            dimension_semantics=("arbitrary", "arbitrary"), vmem_limit_bytes=VMEM_LIMIT),
        name="proj",
    )(x, ctx, mod3, mod3, *weights, *tables_nat, *tables_t)


def _score_bound(qn2, kn2):
    return jnp.sqrt(qn2 * jnp.max(kn2, axis=-1, keepdims=True)) * BOUND_MARGIN


def _flags_kernel(kn2_ref, qn2_ref, o_ref):
    kn2 = kn2_ref[...]
    cols = []
    for i in range(qn2_ref.shape[2] // ATT_TILE):
        bound = _score_bound(qn2_ref[:, :, i * ATT_TILE:(i + 1) * ATT_TILE], kn2)
        worst = jnp.max(jnp.max(bound, axis=2), axis=1, keepdims=True)
        cols.append(jnp.where(worst <= FAST_LIMIT, 1, 0).astype(jnp.int32))
    o_ref[...] = jnp.concatenate(cols, axis=1)


def _flags(kn2, qn2):
    nb, _, seq = qn2.shape
    return pl.pallas_call(
        _flags_kernel,
        out_shape=jax.ShapeDtypeStruct((nb, seq // ATT_TILE), jnp.int32),
        name="flags",
    )(kn2, qn2)


def _attend_body(use_bound, depth, lam_ref, subln_ref, qa_ref, ka_ref, va_ref, qb_ref, kb_ref, vb_ref,
                 kn2_ref, qn2_ref, o_ref):
    t = ATT_TILE
    lp = lam_ref[...]
    lam = (jnp.exp(jnp.sum(lp[0:1] * lp[1:2], axis=-1, keepdims=True))
           - jnp.exp(jnp.sum(lp[2:3] * lp[3:4], axis=-1, keepdims=True)) + LAM_INIT)
    gain = subln_ref[...] * (1.0 - LAM_INIT)
    row = lax.broadcasted_iota(jnp.int32, (LANES, t), 0)
    lo = row < A_HEAD_DIM

    maps = []
    for hd in range(A_HEADS):
        q_t = qa_ref[0, hd]
        zero = jnp.zeros_like(q_t)
        for half in range(2):
            q_m = jnp.where(lo, q_t, zero) if half == 0 else jnp.where(lo, zero, q_t)
            maps.append((q_m, ka_ref.at[0, hd], va_ref.at[0, hd], LANES))
    for hd in range(B_HEADS):
        maps.append((qb_ref[0, hd], kb_ref.at[0, hd], vb_ref.at[0, hd], B_VDIM))

    def scores_t(i):
        q_m, k_ref, _, _ = maps[i]
        return _dot(k_ref[...], q_m)

    bounds = _score_bound(qn2_ref[0], kn2_ref[0]) if use_bound else None

    def finish(i, o_t, acc):
        if i < 2 * A_HEADS:
            hd, half = divmod(i, 2)
            if half == 0:
                acc[hd] = o_t
                return
            o = acc.pop(hd) - lam * o_t
            o = _rms(o, 0) * gain
            o_ref[0, :, hd * LANES:(hd + 1) * LANES] = o.T.astype(o_ref.dtype)
        else:
            hd = i - 2 * A_HEADS
            if hd % 2 == 0:
                acc[hd] = o_t
                return
            o = jnp.concatenate([acc.pop(hd - 1), o_t], axis=0)
            p = hd // 2
            o_ref[0, :, A_WIDTH + p * LANES:A_WIDTH + (p + 1) * LANES] = o.T.astype(o_ref.dtype)

    acc = {}

    def softmax_pv(i, s_t):
        v_ref = maps[i][2]
        m = bounds[i:i + 1] if use_bound else jnp.max(s_t, axis=0, keepdims=True)
        p = jnp.exp2(s_t - m)
        o = _dot(v_ref[...], p.astype(_BF16)) / jnp.sum(p, axis=0, keepdims=True)
        finish(i, o, acc)

    pending = []
    for i in range(len(maps)):
        pending.append((i, scores_t(i)))
        if len(pending) > depth:
            softmax_pv(*pending.pop(0))
    for item in pending:
        softmax_pv(*item)


def _attend_kernel(flags_ref, *refs):
    fast_ok = flags_ref[pl.program_id(0), pl.program_id(1)] != 0

    @pl.when(fast_ok)
    def _():
        _attend_body(True, FAST_DEPTH, *refs)

    @pl.when(jnp.logical_not(fast_ok))
    def _():
        _attend_body(False, EXACT_DEPTH, *refs)


def _attend(lam_p, subln_col, qa, ka, va, qb, kb, vb, kn2, qn2):
    nb, _, _, seq = qa.shape
    t = ATT_TILE

    def q_spec(nh):
        return pl.BlockSpec((1, nh, LANES, t), lambda b, i, flags: (b, 0, 0, i))

    def kv_spec(a):
        return pl.BlockSpec((1,) + a.shape[1:], lambda b, i, flags: (b,) + (0,) * (a.ndim - 1))

    grid_spec = pltpu.PrefetchScalarGridSpec(
        num_scalar_prefetch=1,
        grid=(nb, seq // t),
        in_specs=[
            pl.BlockSpec(lam_p.shape, lambda b, i, flags: (0, 0)),
            pl.BlockSpec(subln_col.shape, lambda b, i, flags: (0, 0)),
            q_spec(A_HEADS), kv_spec(ka), kv_spec(va),
            q_spec(B_HEADS), kv_spec(kb), kv_spec(vb), kv_spec(kn2),
            pl.BlockSpec((1, N_MAPS, t), lambda b, i, flags: (b, 0, i)),
        ],
        out_specs=pl.BlockSpec((1, t, A_WIDTH + B_WIDTH), lambda b, i, flags: (b, i, 0)),
    )
    return pl.pallas_call(
        _attend_kernel,
        grid_spec=grid_spec,
        out_shape=jax.ShapeDtypeStruct((nb, seq, A_WIDTH + B_WIDTH), _BF16),
        compiler_params=pltpu.CompilerParams(
            dimension_semantics=("arbitrary", "arbitrary"), vmem_limit_bytes=VMEM_LIMIT),
        name="attend",
    )(_flags(kn2, qn2), lam_p, subln_col, qa, ka, va, qb, kb, vb, kn2, qn2)


def _post_kernel(cat_ref, x_ref, mod_ref, wo_ref, wgu_ref, wd_ref, fn_ref, o_ref):
    mod = mod_ref[0]
    g_a = mod[:, 2 * D_MODEL:3 * D_MODEL]
    sh_f = mod[:, 3 * D_MODEL:4 * D_MODEL]
    sc_f = mod[:, 4 * D_MODEL:5 * D_MODEL]
    g_f = mod[:, 5 * D_MODEL:6 * D_MODEL]
    rows = [slice(r, r + POST_SUB) for r in range(0, POST_TILE, POST_SUB)]
    lats = [x_ref[0, r, :] + g_a * _dot(cat_ref[0, r, :], wo_ref[...]) for r in rows]
    hs = [(_rms(lat) * (1.0 + sc_f) + sh_f).astype(_BF16) for lat in lats]
    gates = [_dot(h, wgu_ref[:, :FFN_HIDDEN]) for h in hs]
    ups = [_dot(h, wgu_ref[:, FFN_HIDDEN:]) for h in hs]
    acts = [(g / (1.0 + jnp.exp(-g)) * u).astype(_BF16) for g, u in zip(gates, ups)]
    ffns = [_dot(a, wd_ref[...]) for a in acts]
    for r, lat, ffn in zip(rows, lats, ffns):
        o_ref[0, r, :] = _rms(lat + g_f * ffn) * fn_ref[...]


def _post(cat, x, mod3, wo, wgu, wd, fn):
    nb, seq, _ = x.shape
    t = POST_TILE

    def full(a):
        return pl.BlockSpec(a.shape, lambda b, i: (0,) * a.ndim, pipeline_mode=pl.Buffered(1))

    tok = pl.BlockSpec((1, t, D_MODEL), lambda b, i: (b, i, 0))
    return pl.pallas_call(
        _post_kernel,
        grid=(nb, seq // t),
        in_specs=[tok, tok, pl.BlockSpec((1, 1, 6 * D_MODEL), lambda b, i: (b, 0, 0)),
                  full(wo), full(wgu), full(wd), full(fn)],
        out_specs=tok,
        out_shape=jax.ShapeDtypeStruct((nb, seq, D_MODEL), _F32),
        compiler_params=pltpu.CompilerParams(
            dimension_semantics=("arbitrary", "arbitrary"), vmem_limit_bytes=VMEM_LIMIT),
        name="post",
    )(cat, x, mod3, wo, wgu, wd, fn)


def _rope_tables(seq):
    pos = np.arange(seq)
    row = (pos // GRID_W).astype(np.float32)
    col = (pos % GRID_W).astype(np.float32)
    lane = np.arange(LANES)

    def build(d, active, half, q_scale):
        quarter = half // 2
        inv = (ROPE_THETA ** (-(np.arange(quarter, dtype=np.float32)) / quarter)).astype(np.float32)
        p = np.where((d < half)[None, :], row[:, None], col[:, None]).astype(np.float32)
        ang = (p * inv[d % quarter][None, :]).astype(np.float32)
        cos = np.where(active[None, :], np.cos(ang), 1.0).astype(np.float32)
        sin = np.where(active[None, :], np.sin(ang), 0.0).astype(np.float32)
        upper = ((d % half) >= quarter)[None, :]
        s_up = np.where(upper, sin, 0.0).astype(np.float32)
        s_dn = np.where(upper, 0.0, -sin).astype(np.float32)
        ident = np.zeros((CTX_LEN, LANES), np.float32)
        nat = [jnp.asarray(np.concatenate([ident + fill, tab], 0))
               for tab, fill in ((cos, 1.0), (s_up, 0.0), (s_dn, 0.0))]
        trans = [jnp.asarray(np.ascontiguousarray(cos.T * np.float32(q_scale))),
                 jnp.asarray(np.ascontiguousarray((s_up + s_dn).T * np.float32(q_scale)))]
        return nat, trans

    nat_a, trans_a = build(lane % A_HEAD_DIM, np.ones(LANES, bool), A_HEAD_DIM // 2, A_QSCALE)
    in_rope = (lane >= B_NOPE) & (lane < B_NOPE + B_ROPE)
    nat_b, trans_b = build(np.where(in_rope, lane - B_NOPE, 0), in_rope, B_ROPE // 2, B_QSCALE)
    return nat_a + nat_b, trans_a + trans_b


def kernel(x, c, ctx, c_ctx, w_ada, b_ada, w_in, q_a_norm, kv_a_norm, w_q_up, w_kv_up,
           diff_lambda, diff_subln, w_out, w_ffn_in, w_ffn_out, final_norm):
    nb, seq, _ = x.shape
    assert w_ada.shape[0] == 1 and seq % ATT_TILE == 0 and ctx.shape[1] == CTX_LEN == TOKEN_TILE

    pad = (-(nb + 1)) % SUBLANES
    c_all = jnp.concatenate([c, c_ctx[None, :], jnp.zeros((pad, D_MODEL), _F32)], axis=0)
    mod = _adaln(c_all, w_ada[0], b_ada[0][None, :])
    mod3 = mod[:, None, :]

    w = w_in[0]
    o_k, o_v, o_cq = A_WIDTH, 2 * A_WIDTH, 3 * A_WIDTH
    o_ckv, o_kr = o_cq + B_Q_RANK, o_cq + B_Q_RANK + B_KV_RANK
    zeros = lambda n: jnp.zeros((D_MODEL, n), w.dtype)
    w_c = w[:, o_cq:o_kr].T
    w_qv = jnp.concatenate([w[:, :o_k], w[:, o_v:o_cq]], axis=1).T
    w_n = jnp.concatenate([w[:, o_k:o_v], w[:, o_ckv:o_kr], zeros(B_NOPE), w[:, o_kr:],
                           zeros(LANES - B_NOPE - B_ROPE)], axis=1)
    wq = jnp.pad(w_q_up[0].reshape(B_Q_RANK, B_HEADS, B_NOPE + B_ROPE),
                 ((0, 0), (0, 0), (0, LANES - B_NOPE - B_ROPE))).reshape(B_Q_RANK, B_HEADS * LANES)
    wkv3 = w_kv_up[0].reshape(B_KV_RANK, B_HEADS, B_NOPE + B_VDIM)
    wk = jnp.pad(wkv3[:, :, :B_NOPE], ((0, 0), (0, 0), (0, LANES - B_NOPE))).reshape(B_KV_RANK, B_HEADS * LANES)
    wv = wkv3[:, :, B_NOPE:].reshape(B_KV_RANK, B_WIDTH)
    sel = np.zeros((N_MAPS, (A_HEADS + B_HEADS) * LANES), np.float32)
    for hd in range(A_HEADS):
        sel[2 * hd, hd * LANES:hd * LANES + A_HEAD_DIM] = 1.0
        sel[2 * hd + 1, hd * LANES + A_HEAD_DIM:(hd + 1) * LANES] = 1.0
    for hd in range(B_HEADS):
        sel[2 * A_HEADS + hd, (A_HEADS + hd) * LANES:(A_HEADS + hd + 1) * LANES] = 1.0
    weights = (w_c.astype(_BF16), w_n.astype(_BF16), w_qv.astype(_BF16),
               q_a_norm.reshape(B_Q_RANK, 1), kv_a_norm.reshape(B_KV_RANK, 1),
               kv_a_norm.reshape(1, B_KV_RANK),
               wq.T.astype(_BF16), wv.T.astype(_BF16), wk.astype(_BF16), jnp.asarray(sel, _BF16))

    tables_nat, tables_t = _rope_tables(seq)
    qa, ka, va, qb, kb, vb, kn2, qn2 = _proj(x, ctx, mod3, weights, tables_nat, tables_t)

    cat = _attend(diff_lambda[0], diff_subln.reshape(LANES, 1), qa, ka, va, qb, kb, vb, kn2, qn2)

    return _post(cat, x, mod3, w_out[0].astype(_BF16), w_ffn_in[0].astype(_BF16),
                 w_ffn_out[0].astype(_BF16), final_norm[None, :])
```

```python
import math

import numpy as np
import jax
import jax.numpy as jnp
from jax import lax
from jax.experimental import pallas as pl
from jax.experimental.pallas import tpu as pltpu

D_MODEL = 1024
CTX_LEN = 256
GRID_W = 64
ROPE_THETA = 10000.0
NORM_EPS = 1e-6

A_HEADS = 4
A_HEAD_DIM = 64
A_WIDTH = A_HEADS * 2 * A_HEAD_DIM
B_HEADS = 8
B_NOPE = 64
B_ROPE = 32
B_VDIM = 64
B_Q_RANK = 256
B_KV_RANK = 128
B_WIDTH = B_HEADS * B_VDIM
FFN_HIDDEN = 2816

LANES = 128
SUBLANES = 8
TOKEN_TILE = 256
ATT_TILE = 256
ADA_TILE = 1024
PROJ_ROWS = 4
POST_TILE = 512
POST_SUB = 256
V_PAD = 16
ONE_LANE = B_NOPE + B_ROPE
N_MAPS = 2 * A_HEADS + B_HEADS
VMEM_LIMIT = 56 * 1024 * 1024

LAM_INIT = 0.8 - 0.6 * math.exp(-0.3 * 0)
LOG2E = 1.4426950408889634
A_QSCALE = A_HEAD_DIM ** -0.5 * LOG2E
B_QSCALE = (B_NOPE + B_ROPE) ** -0.5 * LOG2E

FAST_LIMIT = 50.0
BOUND_MARGIN = 1.02
FAST_DEPTH = 1
EXACT_DEPTH = 2

_F32 = jnp.float32
_BF16 = jnp.bfloat16


def _dot(a, b):
    return jnp.dot(a, b, preferred_element_type=_F32)


def _dot_nt(a, b):
    return lax.dot_general(a, b, (((1,), (1,)), ((), ())), preferred_element_type=_F32)


def _rms(x, axis=-1):
    return x * lax.rsqrt(jnp.mean(x * x, axis=axis, keepdims=True) + NORM_EPS)


def _adaln_kernel(c_ref, w_ref, b_ref, o_ref):
    c = c_ref[...]
    s = c / (1.0 + jnp.exp(-c))
    o_ref[...] = _dot(s.astype(_BF16), w_ref[...].astype(_BF16)) + b_ref[...]


def _adaln(c_all, w_ada, b_ada):
    rows = c_all.shape[0]
    n = w_ada.shape[1]
    return pl.pallas_call(
        _adaln_kernel,
        grid=(n // ADA_TILE,),
        in_specs=[
            pl.BlockSpec((rows, D_MODEL), lambda i: (0, 0)),
            pl.BlockSpec((D_MODEL, ADA_TILE), lambda i: (0, i)),
            pl.BlockSpec((1, ADA_TILE), lambda i: (0, i)),
        ],
        out_specs=pl.BlockSpec((rows, ADA_TILE), lambda i: (0, i)),
        out_shape=jax.ShapeDtypeStruct((rows, n), _F32),
        name="adaln",
    )(c_all, w_ada, b_ada)


def _rope(x, cos, s_up, s_dn, quarter):
    return (x * cos + pltpu.roll(x, quarter, 1) * s_up
            + pltpu.roll(x, LANES - quarter, 1) * s_dn)


def _rope_t(x, cos_t, sin_t, lo, hi, quarter):
    parts = [x[:lo]] if lo else []
    for r in range(lo, hi, 2 * quarter):
        parts += [x[r + quarter:r + 2 * quarter], x[r:r + quarter]]
    if hi < x.shape[0]:
        parts.append(x[hi:])
    return x * cos_t + jnp.concatenate(parts, axis=0) * sin_t


def _proj_kernel(x_ref, ctx_ref, mod_ref, modc_ref, wc_ref, wn_ref, wqv_ref, qn_ref, kvn_col_ref,
                 kvn_row_ref, wqt_ref, wvt_ref, wk_ref, sel_ref,
                 ca_ref, ua_ref, da_ref, cb_ref, ub_ref, db_ref,
                 cat_ref, sat_ref, cbt_ref, sbt_ref,
                 qa_ref, ka_ref, va_ref, qb_ref, kb_ref, vb_ref, kn2_ref, qn2_ref):
    t = TOKEN_TILE
    chains = range(PROJ_ROWS)
    is_ctx = pl.program_id(1) == 0

    hs, hts = [], []
    for i in chains:
        xin = jnp.where(is_ctx, ctx_ref[i], x_ref[i])
        mod = jnp.where(is_ctx, modc_ref[0], mod_ref[i])
        shift = mod[:, 0:D_MODEL]
        scale = mod[:, D_MODEL:2 * D_MODEL]
        h32 = _rms(xin) * (1.0 + scale) + shift
        hs.append(h32.astype(_BF16))
        hts.append(hs[-1].T)

    ycs = [_dot(wc_ref[...], ht) for ht in hts]
    ys = [_dot(h, wn_ref[...]) for h in hs]
    yqvs = [_dot(wqv_ref[...], ht) for ht in hts]
    qbs = [_dot(wqt_ref[...], (_rms(yc[:B_Q_RANK], 0) * qn_ref[...]).astype(_BF16))
           for yc in ycs]
    vbs = [_dot(wvt_ref[...], (_rms(yc[B_Q_RANK:], 0) * kvn_col_ref[...]).astype(_BF16))
           for yc in ycs]
    kns = [_dot((_rms(y[:, A_WIDTH:A_WIDTH + B_KV_RANK]) * kvn_row_ref[...]).astype(_BF16), wk_ref[...])
           for y in ys]

    row16 = lax.broadcasted_iota(jnp.int32, (V_PAD, t), 0)
    ones_rows = jnp.where(row16 == 0, 1.0, 0.0).astype(_BF16)
    lane = lax.broadcasted_iota(jnp.int32, (t, LANES), 1)
    one_lane = jnp.where(lax.broadcasted_iota(jnp.int32, (1, LANES), 1) == ONE_LANE, 1.0, 0.0)
    cat, sat = cat_ref[...], sat_ref[...]
    cbt, sbt = cbt_ref[...], sbt_ref[...]
    ca, ua, da = ca_ref[...], ua_ref[...], da_ref[...]
    cb, ub, db = cb_ref[...], ub_ref[...], db_ref[...]

    def col_sq_norm(q):
        return jnp.sum(q * q, axis=0, keepdims=True)

    def squares(k):
        return (k * k).astype(_BF16)

    for i in chains:
        yqv_t, qb_t, vb_t, y, kn = yqvs[i], qbs[i], vbs[i], ys[i], kns[i]
        qn2_rows = []
        for hd in range(A_HEADS):
            q_t = yqv_t[hd * LANES:(hd + 1) * LANES]
            q = _rope_t(q_t, cat, sat, 0, LANES, A_HEAD_DIM // 4)
            qa_ref[i, hd] = q.astype(_BF16)
            qn2_rows += [col_sq_norm(q[:A_HEAD_DIM]), col_sq_norm(q[A_HEAD_DIM:])]
            va_ref[i, hd] = yqv_t[A_WIDTH + hd * LANES:A_WIDTH + (hd + 1) * LANES].astype(_BF16)
        for hd in range(B_HEADS):
            blk = qb_t[hd * LANES:(hd + 1) * LANES]
            q = _rope_t(blk, cbt, sbt, B_NOPE, B_NOPE + B_ROPE, B_ROPE // 4)
            qb_ref[i, hd] = q.astype(_BF16)
            qn2_rows.append(col_sq_norm(q))
            vb_ref[i, hd, :B_VDIM, :] = vb_t[hd * B_VDIM:(hd + 1) * B_VDIM].astype(_BF16)
            vb_ref[i, hd, B_VDIM:, :] = ones_rows
        qn2_ref[i] = jnp.concatenate(qn2_rows, axis=0)

        ksq = []
        for hd in range(A_HEADS):
            k = _rope(y[:, hd * LANES:(hd + 1) * LANES], ca, ua, da, A_HEAD_DIM // 4)
            ka_ref[i, hd] = k.astype(_BF16)
            ksq.append(squares(k))
        kr_rot = _rope(y[:, A_WIDTH + B_KV_RANK:], cb, ub, db, B_ROPE // 4) + one_lane
        for hd in range(B_HEADS):
            k = jnp.where(lane < B_NOPE, kn[:, hd * LANES:(hd + 1) * LANES], kr_rot)
            kb_ref[i, hd] = k.astype(_BF16)
            ksq.append(squares(k))
        kn2_ref[i] = _dot_nt(sel_ref[...], jnp.concatenate(ksq, axis=1))


def _proj(x, ctx, mod3, weights, tables_nat, tables_t):
    nb, seq, _ = x.shape
    n_tok = CTX_LEN + seq
    n_tiles = n_tok // TOKEN_TILE
    t = TOKEN_TILE

    def full(a):
        return pl.BlockSpec(a.shape, lambda b, j: (0,) * a.ndim)

    lat = lambda j: jnp.maximum(j - 1, 0)
    out_shapes = (
        jax.ShapeDtypeStruct((nb, A_HEADS, LANES, seq), _BF16),
        jax.ShapeDtypeStruct((nb, A_HEADS, n_tok, LANES), _BF16),
        jax.ShapeDtypeStruct((nb, A_HEADS, LANES, n_tok), _BF16),
        jax.ShapeDtypeStruct((nb, B_HEADS, LANES, seq), _BF16),
        jax.ShapeDtypeStruct((nb, B_HEADS, n_tok, LANES), _BF16),
        jax.ShapeDtypeStruct((nb, B_HEADS, B_VDIM + V_PAD, n_tok), _BF16),
        jax.ShapeDtypeStruct((nb, N_MAPS, n_tok), _F32),
        jax.ShapeDtypeStruct((nb, N_MAPS, seq), _F32),
    )
    r = PROJ_ROWS
    out_specs = (
        pl.BlockSpec((r, A_HEADS, LANES, t), lambda b, j: (b, 0, 0, lat(j))),
        pl.BlockSpec((r, A_HEADS, t, LANES), lambda b, j: (b, 0, j, 0)),
        pl.BlockSpec((r, A_HEADS, LANES, t), lambda b, j: (b, 0, 0, j)),
        pl.BlockSpec((r, B_HEADS, LANES, t), lambda b, j: (b, 0, 0, lat(j))),
        pl.BlockSpec((r, B_HEADS, t, LANES), lambda b, j: (b, 0, j, 0)),
        pl.BlockSpec((r, B_HEADS, B_VDIM + V_PAD, t), lambda b, j: (b, 0, 0, j)),
        pl.BlockSpec((r, N_MAPS, t), lambda b, j: (b, 0, j)),
        pl.BlockSpec((r, N_MAPS, t), lambda b, j: (b, 0, lat(j))),
    )
    return pl.pallas_call(
        _proj_kernel,
        grid=(nb // r, n_tiles),
        in_specs=[
            pl.BlockSpec((r, t, D_MODEL), lambda b, j: (b, lat(j), 0)),
            pl.BlockSpec((r, t, D_MODEL), lambda b, j: (b, 0, 0)),
            pl.BlockSpec((r, 1, 6 * D_MODEL), lambda b, j: (b, 0, 0)),
            pl.BlockSpec((1, 1, 6 * D_MODEL), lambda b, j: (nb, 0, 0)),
        ] + [full(w) for w in weights]
          + [pl.BlockSpec((t, LANES), lambda b, j: (j, 0))] * len(tables_nat)
          + [pl.BlockSpec((LANES, t), lambda b, j: (0, lat(j)))] * len(tables_t),
        out_specs=out_specs,
        out_shape=out_shapes,
        compiler_params=pltpu.CompilerParams(
            dimension_semantics=("arbitrary", "arbitrary"), vmem_limit_bytes=VMEM_LIMIT),
        name="proj",
    )(x, ctx, mod3, mod3, *weights, *tables_nat, *tables_t)


def _score_bound(qn2, kn2):
    return jnp.sqrt(qn2 * jnp.max(kn2, axis=-1, keepdims=True)) * BOUND_MARGIN


def _flags_kernel(kn2_ref, qn2_ref, o_ref):
    kn2 = kn2_ref[...]
    cols = []
    for i in range(qn2_ref.shape[2] // ATT_TILE):
        bound = _score_bound(qn2_ref[:, :, i * ATT_TILE:(i + 1) * ATT_TILE], kn2)
        worst = jnp.max(jnp.max(bound, axis=2), axis=1, keepdims=True)
        cols.append(jnp.where(worst <= FAST_LIMIT, 1, 0).astype(jnp.int32))
    o_ref[...] = jnp.concatenate(cols, axis=1)


def _flags(kn2, qn2):
    nb, _, seq = qn2.shape
    return pl.pallas_call(
        _flags_kernel,
        out_shape=jax.ShapeDtypeStruct((nb, seq // ATT_TILE), jnp.int32),
        name="flags",
    )(kn2, qn2)


def _attend_body(use_bound, depth, lam_ref, subln_ref, qa_ref, ka_ref, va_ref, qb_ref, kb_ref, vb_ref,
                 kn2_ref, qn2_ref, o_ref):
    t = ATT_TILE
    lp = lam_ref[...]
    lam = (jnp.exp(jnp.sum(lp[0:1] * lp[1:2], axis=-1, keepdims=True))
           - jnp.exp(jnp.sum(lp[2:3] * lp[3:4], axis=-1, keepdims=True)) + LAM_INIT)
    gain = subln_ref[...] * (1.0 - LAM_INIT)
    row = lax.broadcasted_iota(jnp.int32, (LANES, t), 0)
    lo = row < A_HEAD_DIM

    maps = []
    for hd in range(A_HEADS):
        q_t = qa_ref[0, hd]
        zero = jnp.zeros_like(q_t)
        for half in range(2):
            q_m = jnp.where(lo, q_t, zero) if half == 0 else jnp.where(lo, zero, q_t)
            maps.append((q_m, ka_ref.at[0, hd], va_ref.at[0, hd], LANES))
    for hd in range(B_HEADS):
        maps.append((qb_ref[0, hd], kb_ref.at[0, hd], vb_ref.at[0, hd], B_VDIM))

    bounds = _score_bound(qn2_ref[0], kn2_ref[0]) if use_bound else None

    def shift_in_matmul(i):
        return use_bound and i >= 2 * A_HEADS

    def scores_t(i):
        q_m, k_ref, _, _ = maps[i]
        if shift_in_matmul(i):
            q_m = jnp.where(row == ONE_LANE, (-bounds[i:i + 1]).astype(_BF16), q_m)
        return _dot(k_ref[...], q_m)

    def finish(i, o_t, acc):
        if i < 2 * A_HEADS:
            hd, half = divmod(i, 2)
            if half == 0:
                acc[hd] = o_t
                return
            o = acc.pop(hd) - lam * o_t
            o = _rms(o, 0) * gain
            o_ref[0, :, hd * LANES:(hd + 1) * LANES] = o.T.astype(o_ref.dtype)
        else:
            hd = i - 2 * A_HEADS
            if hd % 2 == 0:
                acc[hd] = o_t
                return
            o = jnp.concatenate([acc.pop(hd - 1), o_t], axis=0)
            p = hd // 2
            o_ref[0, :, A_WIDTH + p * LANES:A_WIDTH + (p + 1) * LANES] = o.T.astype(o_ref.dtype)

    acc = {}

    def softmax_pv(i, s_t):
        _, _, v_ref, dv = maps[i]
        if shift_in_matmul(i):
            p = jnp.exp2(s_t)
        else:
            m = bounds[i:i + 1] if use_bound else jnp.max(s_t, axis=0, keepdims=True)
            p = jnp.exp2(s_t - m)
        if v_ref.shape[0] == dv:
            o = _dot(v_ref[...], p.astype(_BF16)) / jnp.sum(p, axis=0, keepdims=True)
        else:
            o = _dot(v_ref[...], p.astype(_BF16))
            o = o[:dv] / o[dv:dv + 1]
        finish(i, o, acc)

    pending = []
    for i in range(len(maps)):
        pending.append((i, scores_t(i)))
        if len(pending) > depth:
            softmax_pv(*pending.pop(0))
    for item in pending:
        softmax_pv(*item)


def _attend_kernel(flags_ref, *refs):
    fast_ok = flags_ref[pl.program_id(0), pl.program_id(1)] != 0

    @pl.when(fast_ok)
    def _():
        _attend_body(True, FAST_DEPTH, *refs)

    @pl.when(jnp.logical_not(fast_ok))
    def _():
        _attend_body(False, EXACT_DEPTH, *refs)


def _attend(lam_p, subln_col, qa, ka, va, qb, kb, vb, kn2, qn2):
    nb, _, _, seq = qa.shape
    t = ATT_TILE

    def q_spec(nh):
        return pl.BlockSpec((1, nh, LANES, t), lambda b, i, flags: (b, 0, 0, i))

    def kv_spec(a):
        return pl.BlockSpec((1,) + a.shape[1:], lambda b, i, flags: (b,) + (0,) * (a.ndim - 1))

    grid_spec = pltpu.PrefetchScalarGridSpec(
        num_scalar_prefetch=1,
        grid=(nb, seq // t),
        in_specs=[
            pl.BlockSpec(lam_p.shape, lambda b, i, flags: (0, 0)),
            pl.BlockSpec(subln_col.shape, lambda b, i, flags: (0, 0)),
            q_spec(A_HEADS), kv_spec(ka), kv_spec(va),
            q_spec(B_HEADS), kv_spec(kb), kv_spec(vb), kv_spec(kn2),
            pl.BlockSpec((1, N_MAPS, t), lambda b, i, flags: (b, 0, i)),
        ],
        out_specs=pl.BlockSpec((1, t, A_WIDTH + B_WIDTH), lambda b, i, flags: (b, i, 0)),
    )
    return pl.pallas_call(
        _attend_kernel,
        grid_spec=grid_spec,
        out_shape=jax.ShapeDtypeStruct((nb, seq, A_WIDTH + B_WIDTH), _BF16),
        compiler_params=pltpu.CompilerParams(
            dimension_semantics=("arbitrary", "arbitrary"), vmem_limit_bytes=VMEM_LIMIT),
        name="attend",
    )(_flags(kn2, qn2), lam_p, subln_col, qa, ka, va, qb, kb, vb, kn2, qn2)


def _post_kernel(cat_ref, x_ref, mod_ref, wo_ref, wgu_ref, wd_ref, fn_ref, o_ref):
    mod = mod_ref[0]
    g_a = mod[:, 2 * D_MODEL:3 * D_MODEL]
    sh_f = mod[:, 3 * D_MODEL:4 * D_MODEL]
    sc_f = mod[:, 4 * D_MODEL:5 * D_MODEL]
    g_f = mod[:, 5 * D_MODEL:6 * D_MODEL]
    rows = [slice(r, r + POST_SUB) for r in range(0, POST_TILE, POST_SUB)]
    lats = [x_ref[0, r, :] + g_a * _dot(cat_ref[0, r, :], wo_ref[...]) for r in rows]
    hs = [(_rms(lat) * (1.0 + sc_f) + sh_f).astype(_BF16) for lat in lats]
    gates = [_dot(h, wgu_ref[:, :FFN_HIDDEN]) for h in hs]
    ups = [_dot(h, wgu_ref[:, FFN_HIDDEN:]) for h in hs]
    acts = [(g / (1.0 + jnp.exp(-g)) * u).astype(_BF16) for g, u in zip(gates, ups)]
    ffns = [_dot(a, wd_ref[...]) for a in acts]
    for r, lat, ffn in zip(rows, lats, ffns):
        o_ref[0, r, :] = _rms(lat + g_f * ffn) * fn_ref[...]


def _post(cat, x, mod3, wo, wgu, wd, fn):
    nb, seq, _ = x.shape
    t = POST_TILE

    def full(a):
        return pl.BlockSpec(a.shape, lambda b, i: (0,) * a.ndim, pipeline_mode=pl.Buffered(1))

    tok = pl.BlockSpec((1, t, D_MODEL), lambda b, i: (b, i, 0))
    return pl.pallas_call(
        _post_kernel,
        grid=(nb, seq // t),
        in_specs=[tok, tok, pl.BlockSpec((1, 1, 6 * D_MODEL), lambda b, i: (b, 0, 0)),
                  full(wo), full(wgu), full(wd), full(fn)],
        out_specs=tok,
        out_shape=jax.ShapeDtypeStruct((nb, seq, D_MODEL), _F32),
        compiler_params=pltpu.CompilerParams(
            dimension_semantics=("arbitrary", "arbitrary"), vmem_limit_bytes=VMEM_LIMIT),
        name="post",
    )(cat, x, mod3, wo, wgu, wd, fn)


def _rope_tables(seq):
    pos = np.arange(seq)
    row = (pos // GRID_W).astype(np.float32)
    col = (pos % GRID_W).astype(np.float32)
    lane = np.arange(LANES)

    def build(d, active, half, q_scale):
        quarter = half // 2
        inv = (ROPE_THETA ** (-(np.arange(quarter, dtype=np.float32)) / quarter)).astype(np.float32)
        p = np.where((d < half)[None, :], row[:, None], col[:, None]).astype(np.float32)
        ang = (p * inv[d % quarter][None, :]).astype(np.float32)
        cos = np.where(active[None, :], np.cos(ang), 1.0).astype(np.float32)
        sin = np.where(active[None, :], np.sin(ang), 0.0).astype(np.float32)
        upper = ((d % half) >= quarter)[None, :]
        s_up = np.where(upper, sin, 0.0).astype(np.float32)
        s_dn = np.where(upper, 0.0, -sin).astype(np.float32)
        ident = np.zeros((CTX_LEN, LANES), np.float32)
        nat = [jnp.asarray(np.concatenate([ident + fill, tab], 0))
               for tab, fill in ((cos, 1.0), (s_up, 0.0), (s_dn, 0.0))]
        trans = [jnp.asarray(np.ascontiguousarray(cos.T * np.float32(q_scale))),
                 jnp.asarray(np.ascontiguousarray((s_up + s_dn).T * np.float32(q_scale)))]
        return nat, trans

    nat_a, trans_a = build(lane % A_HEAD_DIM, np.ones(LANES, bool), A_HEAD_DIM // 2, A_QSCALE)
    in_rope = (lane >= B_NOPE) & (lane < B_NOPE + B_ROPE)
    nat_b, trans_b = build(np.where(in_rope, lane - B_NOPE, 0), in_rope, B_ROPE // 2, B_QSCALE)
    return nat_a + nat_b, trans_a + trans_b


def kernel(x, c, ctx, c_ctx, w_ada, b_ada, w_in, q_a_norm, kv_a_norm, w_q_up, w_kv_up,
           diff_lambda, diff_subln, w_out, w_ffn_in, w_ffn_out, final_norm):
    nb, seq, _ = x.shape
    assert w_ada.shape[0] == 1 and seq % ATT_TILE == 0 and ctx.shape[1] == CTX_LEN == TOKEN_TILE

    pad = (-(nb + 1)) % SUBLANES
    c_all = jnp.concatenate([c, c_ctx[None, :], jnp.zeros((pad, D_MODEL), _F32)], axis=0)
    mod = _adaln(c_all, w_ada[0], b_ada[0][None, :])
    mod3 = mod[:, None, :]

    w = w_in[0]
    o_k, o_v, o_cq = A_WIDTH, 2 * A_WIDTH, 3 * A_WIDTH
    o_ckv, o_kr = o_cq + B_Q_RANK, o_cq + B_Q_RANK + B_KV_RANK
    zeros = lambda n: jnp.zeros((D_MODEL, n), w.dtype)
    w_c = w[:, o_cq:o_kr].T
    w_qv = jnp.concatenate([w[:, :o_k], w[:, o_v:o_cq]], axis=1).T
    w_n = jnp.concatenate([w[:, o_k:o_v], w[:, o_ckv:o_kr], zeros(B_NOPE), w[:, o_kr:],
                           zeros(LANES - B_NOPE - B_ROPE)], axis=1)
    wq = jnp.pad(w_q_up[0].reshape(B_Q_RANK, B_HEADS, B_NOPE + B_ROPE),
                 ((0, 0), (0, 0), (0, LANES - B_NOPE - B_ROPE))).reshape(B_Q_RANK, B_HEADS * LANES)
    wkv3 = w_kv_up[0].reshape(B_KV_RANK, B_HEADS, B_NOPE + B_VDIM)
    wk = jnp.pad(wkv3[:, :, :B_NOPE], ((0, 0), (0, 0), (0, LANES - B_NOPE))).reshape(B_KV_RANK, B_HEADS * LANES)
    wv = wkv3[:, :, B_NOPE:].reshape(B_KV_RANK, B_WIDTH)
    sel = np.zeros((N_MAPS, (A_HEADS + B_HEADS) * LANES), np.float32)
    for hd in range(A_HEADS):
        sel[2 * hd, hd * LANES:hd * LANES + A_HEAD_DIM] = 1.0
        sel[2 * hd + 1, hd * LANES + A_HEAD_DIM:(hd + 1) * LANES] = 1.0
    for hd in range(B_HEADS):
        sel[2 * A_HEADS + hd, (A_HEADS + hd) * LANES:(A_HEADS + hd) * LANES + ONE_LANE] = 1.0
    weights = (w_c.astype(_BF16), w_n.astype(_BF16), w_qv.astype(_BF16),
               q_a_norm.reshape(B_Q_RANK, 1), kv_a_norm.reshape(B_KV_RANK, 1),
               kv_a_norm.reshape(1, B_KV_RANK),
               wq.T.astype(_BF16), wv.T.astype(_BF16), wk.astype(_BF16), jnp.asarray(sel, _BF16))

    tables_nat, tables_t = _rope_tables(seq)
    qa, ka, va, qb, kb, vb, kn2, qn2 = _proj(x, ctx, mod3, weights, tables_nat, tables_t)

    cat = _attend(diff_lambda[0], diff_subln.reshape(LANES, 1), qa, ka, va, qb, kb, vb, kn2, qn2)

    return _post(cat, x, mod3, w_out[0].astype(_BF16), w_ffn_in[0].astype(_BF16),
                 w_ffn_out[0].astype(_BF16), final_norm[None, :])
```

```python
import math

import numpy as np
import jax
import jax.numpy as jnp
from jax import lax
from jax.experimental import pallas as pl
from jax.experimental.pallas import tpu as pltpu

D_MODEL = 1024
CTX_LEN = 256
GRID_W = 64
ROPE_THETA = 10000.0
NORM_EPS = 1e-6

A_HEADS = 4
A_HEAD_DIM = 64
A_WIDTH = A_HEADS * 2 * A_HEAD_DIM
B_HEADS = 8
B_NOPE = 64
B_ROPE = 32
B_VDIM = 64
B_Q_RANK = 256
B_KV_RANK = 128
B_WIDTH = B_HEADS * B_VDIM
FFN_HIDDEN = 2816

LANES = 128
SUBLANES = 8
TOKEN_TILE = 256
ATT_TILE = 256
ADA_TILE = 2048
PROJ_ROWS = 4
POST_TILE = 512
POST_SUB = 256
V_PAD = 16
N_MAPS = 2 * A_HEADS + B_HEADS
V7X_VMEM_BYTES = 64 * 1024 * 1024
VMEM_LIMIT = V7X_VMEM_BYTES - 8 * 1024 * 1024

LAM_INIT = 0.8 - 0.6 * math.exp(-0.3 * 0)
LOG2E = 1.4426950408889634
A_QSCALE = A_HEAD_DIM ** -0.5 * LOG2E
B_QSCALE = (B_NOPE + B_ROPE) ** -0.5 * LOG2E

FAST_LIMIT = 50.0
BOUND_MARGIN = 1.02
FAST_DEPTH = 1
EXACT_DEPTH = 2

_F32 = jnp.float32
_BF16 = jnp.bfloat16


def _dot(a, b):
    return jnp.dot(a, b, preferred_element_type=_F32)


def _dot_nt(a, b):
    return lax.dot_general(a, b, (((1,), (1,)), ((), ())), preferred_element_type=_F32)


def _rms(x, axis=-1):
    return x * lax.rsqrt(jnp.mean(x * x, axis=axis, keepdims=True) + NORM_EPS)


def _adaln_kernel(c_ref, w_ref, b_ref, o_ref):
    c = c_ref[...]
    s = c / (1.0 + jnp.exp(-c))
    o_ref[...] = _dot(s.astype(_BF16), w_ref[...].astype(_BF16)) + b_ref[...]


def _adaln(c_all, w_ada, b_ada):
    rows = c_all.shape[0]
    n = w_ada.shape[1]
    return pl.pallas_call(
        _adaln_kernel,
        grid=(n // ADA_TILE,),
        in_specs=[
            pl.BlockSpec((rows, D_MODEL), lambda i: (0, 0)),
            pl.BlockSpec((D_MODEL, ADA_TILE), lambda i: (0, i)),
            pl.BlockSpec((1, ADA_TILE), lambda i: (0, i)),
        ],
        out_specs=pl.BlockSpec((rows, ADA_TILE), lambda i: (0, i)),
        out_shape=jax.ShapeDtypeStruct((rows, n), _F32),
        name="adaln",
    )(c_all, w_ada, b_ada)


def _rope(x, cos, s_up, s_dn, quarter):
    return (x * cos + pltpu.roll(x, quarter, 1) * s_up
            + pltpu.roll(x, LANES - quarter, 1) * s_dn)


def _rope_t(x, cos_t, sin_t, lo, hi, quarter):
    parts = [x[:lo]] if lo else []
    for r in range(lo, hi, 2 * quarter):
        parts += [x[r + quarter:r + 2 * quarter], x[r:r + quarter]]
    if hi < x.shape[0]:
        parts.append(x[hi:])
    return x * cos_t + jnp.concatenate(parts, axis=0) * sin_t


def _proj_kernel(x_ref, ctx_ref, mod_ref, modc_ref, wc_ref, wn_ref, wqv_ref, qn_ref, kvn_col_ref,
                 kvn_row_ref, wqt_ref, wvt_ref, wk_ref, sel_ref,
                 ca_ref, ua_ref, da_ref, cb_ref, ub_ref, db_ref,
                 cat_ref, sat_ref, cbt_ref, sbt_ref,
                 qa_ref, ka_ref, va_ref, qb_ref, kb_ref, vb_ref, kn2_ref, qn2_ref):
    t = TOKEN_TILE
    chains = range(PROJ_ROWS)
    is_ctx = pl.program_id(1) == 0

    hs, hts = [], []
    for i in chains:
        xin = jnp.where(is_ctx, ctx_ref[i], x_ref[i])
        mod = jnp.where(is_ctx, modc_ref[0], mod_ref[i])
        shift = mod[:, 0:D_MODEL]
        scale = mod[:, D_MODEL:2 * D_MODEL]
        h32 = _rms(xin) * (1.0 + scale) + shift
        hs.append(h32.astype(_BF16))
        hts.append(hs[-1].T)

    ycs = [_dot(wc_ref[...], ht) for ht in hts]
    ys = [_dot(h, wn_ref[...]) for h in hs]
    yqvs = [_dot(wqv_ref[...], ht) for ht in hts]
    qbs = [_dot(wqt_ref[...], (_rms(yc[:B_Q_RANK], 0) * qn_ref[...]).astype(_BF16))
           for yc in ycs]
    vbs = [_dot(wvt_ref[...], (_rms(yc[B_Q_RANK:], 0) * kvn_col_ref[...]).astype(_BF16))
           for yc in ycs]
    kns = [_dot((_rms(y[:, A_WIDTH:A_WIDTH + B_KV_RANK]) * kvn_row_ref[...]).astype(_BF16), wk_ref[...])
           for y in ys]

    row16 = lax.broadcasted_iota(jnp.int32, (V_PAD, t), 0)
    ones_rows = jnp.where(row16 == 0, 1.0, 0.0).astype(_BF16)
    lane = lax.broadcasted_iota(jnp.int32, (t, LANES), 1)
    cat, sat = cat_ref[...], sat_ref[...]
    cbt, sbt = cbt_ref[...], sbt_ref[...]
    ca, ua, da = ca_ref[...], ua_ref[...], da_ref[...]
    cb, ub, db = cb_ref[...], ub_ref[...], db_ref[...]

    def col_sq_norm(q):
        return jnp.sum(q * q, axis=0, keepdims=True)

    def squares(k):
        return (k * k).astype(_BF16)

    for i in chains:
        yqv_t, qb_t, vb_t, y, kn = yqvs[i], qbs[i], vbs[i], ys[i], kns[i]
        qn2_rows = []
        for hd in range(A_HEADS):
            q_t = yqv_t[hd * LANES:(hd + 1) * LANES]
            q = _rope_t(q_t, cat, sat, 0, LANES, A_HEAD_DIM // 4)
            qa_ref[i, hd] = q.astype(_BF16)
            qn2_rows += [col_sq_norm(q[:A_HEAD_DIM]), col_sq_norm(q[A_HEAD_DIM:])]
            va_ref[i, hd] = yqv_t[A_WIDTH + hd * LANES:A_WIDTH + (hd + 1) * LANES].astype(_BF16)
        for hd in range(B_HEADS):
            blk = qb_t[hd * LANES:(hd + 1) * LANES]
            q = _rope_t(blk, cbt, sbt, B_NOPE, B_NOPE + B_ROPE, B_ROPE // 4)
            qb_ref[i, hd] = q.astype(_BF16)
            qn2_rows.append(col_sq_norm(q))
            vb_ref[i, hd, :B_VDIM, :] = vb_t[hd * B_VDIM:(hd + 1) * B_VDIM].astype(_BF16)
            vb_ref[i, hd, B_VDIM:, :] = ones_rows
        qn2_ref[i] = jnp.concatenate(qn2_rows, axis=0)

        ksq = []
        for hd in range(A_HEADS):
            k = _rope(y[:, hd * LANES:(hd + 1) * LANES], ca, ua, da, A_HEAD_DIM // 4)
            ka_ref[i, hd] = k.astype(_BF16)
            ksq.append(squares(k))
        kr_rot = _rope(y[:, A_WIDTH + B_KV_RANK:], cb, ub, db, B_ROPE // 4)
        for hd in range(B_HEADS):
            k = jnp.where(lane < B_NOPE, kn[:, hd * LANES:(hd + 1) * LANES], kr_rot)
            kb_ref[i, hd] = k.astype(_BF16)
            ksq.append(squares(k))
        kn2_ref[i] = _dot_nt(sel_ref[...], jnp.concatenate(ksq, axis=1))


def _proj(x, ctx, mod3, weights, tables_nat, tables_t):
    nb, seq, _ = x.shape
    n_tok = CTX_LEN + seq
    n_tiles = n_tok // TOKEN_TILE
    t = TOKEN_TILE

    def full(a):
        return pl.BlockSpec(a.shape, lambda b, j: (0,) * a.ndim)

    lat = lambda j: jnp.maximum(j - 1, 0)
    out_shapes = (
        jax.ShapeDtypeStruct((nb, A_HEADS, LANES, seq), _BF16),
        jax.ShapeDtypeStruct((nb, A_HEADS, n_tok, LANES), _BF16),
        jax.ShapeDtypeStruct((nb, A_HEADS, LANES, n_tok), _BF16),
        jax.ShapeDtypeStruct((nb, B_HEADS, LANES, seq), _BF16),
        jax.ShapeDtypeStruct((nb, B_HEADS, n_tok, LANES), _BF16),
        jax.ShapeDtypeStruct((nb, B_HEADS, B_VDIM + V_PAD, n_tok), _BF16),
        jax.ShapeDtypeStruct((nb, N_MAPS, n_tok), _F32),
        jax.ShapeDtypeStruct((nb, N_MAPS, seq), _F32),
    )
    r = PROJ_ROWS
    out_specs = (
        pl.BlockSpec((r, A_HEADS, LANES, t), lambda b, j: (b, 0, 0, lat(j))),
        pl.BlockSpec((r, A_HEADS, t, LANES), lambda b, j: (b, 0, j, 0)),
        pl.BlockSpec((r, A_HEADS, LANES, t), lambda b, j: (b, 0, 0, j)),
        pl.BlockSpec((r, B_HEADS, LANES, t), lambda b, j: (b, 0, 0, lat(j))),
        pl.BlockSpec((r, B_HEADS, t, LANES), lambda b, j: (b, 0, j, 0)),
        pl.BlockSpec((r, B_HEADS, B_VDIM + V_PAD, t), lambda b, j: (b, 0, 0, j)),
        pl.BlockSpec((r, N_MAPS, t), lambda b, j: (b, 0, j)),
        pl.BlockSpec((r, N_MAPS, t), lambda b, j: (b, 0, lat(j))),
    )
    return pl.pallas_call(
        _proj_kernel,
        grid=(nb // r, n_tiles),
        in_specs=[
            pl.BlockSpec((r, t, D_MODEL), lambda b, j: (b, lat(j), 0)),
            pl.BlockSpec((r, t, D_MODEL), lambda b, j: (b, 0, 0)),
            pl.BlockSpec((r, 1, 6 * D_MODEL), lambda b, j: (b, 0, 0)),
            pl.BlockSpec((1, 1, 6 * D_MODEL), lambda b, j: (nb, 0, 0)),
        ] + [full(w) for w in weights]
          + [pl.BlockSpec((t, LANES), lambda b, j: (j, 0))] * len(tables_nat)
          + [pl.BlockSpec((LANES, t), lambda b, j: (0, lat(j)))] * len(tables_t),
        out_specs=out_specs,
        out_shape=out_shapes,
        compiler_params=pltpu.CompilerParams(
            dimension_semantics=("arbitrary", "arbitrary"), vmem_limit_bytes=VMEM_LIMIT),
        name="proj",
    )(x, ctx, mod3, mod3, *weights, *tables_nat, *tables_t)


def _score_bound(qn2, kn2):
    return jnp.sqrt(qn2 * jnp.max(kn2, axis=-1, keepdims=True)) * BOUND_MARGIN


def _flags_kernel(kn2_ref, qn2_ref, o_ref):
    kn2 = kn2_ref[...]
    cols = []
    for i in range(qn2_ref.shape[2] // ATT_TILE):
        bound = _score_bound(qn2_ref[:, :, i * ATT_TILE:(i + 1) * ATT_TILE], kn2)
        worst = jnp.max(jnp.max(bound, axis=2), axis=1, keepdims=True)
        cols.append(jnp.where(worst <= FAST_LIMIT, 1, 0).astype(jnp.int32))
    o_ref[...] = jnp.concatenate(cols, axis=1)


def _flags(kn2, qn2):
    nb, _, seq = qn2.shape
    return pl.pallas_call(
        _flags_kernel,
        out_shape=jax.ShapeDtypeStruct((nb, seq // ATT_TILE), jnp.int32),
        name="flags",
    )(kn2, qn2)


def _attend_body(use_bound, depth, lam_ref, subln_ref, qa_ref, ka_ref, va_ref, qb_ref, kb_ref, vb_ref,
                 kn2_ref, qn2_ref, o_ref):
    t = ATT_TILE
    lp = lam_ref[...]
    lam = (jnp.exp(jnp.sum(lp[0:1] * lp[1:2], axis=-1, keepdims=True))
           - jnp.exp(jnp.sum(lp[2:3] * lp[3:4], axis=-1, keepdims=True)) + LAM_INIT)
    gain = subln_ref[...] * (1.0 - LAM_INIT)
    row = lax.broadcasted_iota(jnp.int32, (LANES, t), 0)
    lo = row < A_HEAD_DIM

    maps = []
    for hd in range(A_HEADS):
        q_t = qa_ref[0, hd]
        zero = jnp.zeros_like(q_t)
        for half in range(2):
            q_m = jnp.where(lo, q_t, zero) if half == 0 else jnp.where(lo, zero, q_t)
            maps.append((q_m, ka_ref.at[0, hd], va_ref.at[0, hd], LANES))
    for hd in range(B_HEADS):
        maps.append((qb_ref[0, hd], kb_ref.at[0, hd], vb_ref.at[0, hd], B_VDIM))

    def scores_t(i):
        q_m, k_ref, _, _ = maps[i]
        return _dot(k_ref[...], q_m)

    bounds = _score_bound(qn2_ref[0], kn2_ref[0]) if use_bound else None

    def finish(i, o_t, acc):
        if i < 2 * A_HEADS:
            hd, half = divmod(i, 2)
            if half == 0:
                acc[hd] = o_t
                return
            o = acc.pop(hd) - lam * o_t
            o = _rms(o, 0) * gain
            o_ref[0, :, hd * LANES:(hd + 1) * LANES] = o.T.astype(o_ref.dtype)
        else:
            hd = i - 2 * A_HEADS
            if hd % 2 == 0:
                acc[hd] = o_t
                return
            o = jnp.concatenate([acc.pop(hd - 1), o_t], axis=0)
            p = hd // 2
            o_ref[0, :, A_WIDTH + p * LANES:A_WIDTH + (p + 1) * LANES] = o.T.astype(o_ref.dtype)

    acc = {}

    def softmax_pv(i, s_t):
        _, _, v_ref, dv = maps[i]
        m = bounds[i:i + 1] if use_bound else jnp.max(s_t, axis=0, keepdims=True)
        p = jnp.exp2(s_t - m)
        if v_ref.shape[0] == dv:
            o = _dot(v_ref[...], p.astype(_BF16)) / jnp.sum(p, axis=0, keepdims=True)
        else:
            o = _dot(v_ref[...], p.astype(_BF16))
            o = o[:dv] / o[dv:dv + 1]
        finish(i, o, acc)

    pending = []
    for i in range(len(maps)):
        pending.append((i, scores_t(i)))
        if len(pending) > depth:
            softmax_pv(*pending.pop(0))
    for item in pending:
        softmax_pv(*item)


def _attend_kernel(flags_ref, *refs):
    fast_ok = flags_ref[pl.program_id(0), pl.program_id(1)] != 0

    @pl.when(fast_ok)
    def _():
        _attend_body(True, FAST_DEPTH, *refs)

    @pl.when(jnp.logical_not(fast_ok))
    def _():
        _attend_body(False, EXACT_DEPTH, *refs)


def _attend(lam_p, subln_col, qa, ka, va, qb, kb, vb, kn2, qn2):
    nb, _, _, seq = qa.shape
    t = ATT_TILE

    def q_spec(nh):
        return pl.BlockSpec((1, nh, LANES, t), lambda b, i, flags: (b, 0, 0, i))

    def kv_spec(a):
        return pl.BlockSpec((1,) + a.shape[1:], lambda b, i, flags: (b,) + (0,) * (a.ndim - 1))

    grid_spec = pltpu.PrefetchScalarGridSpec(
        num_scalar_prefetch=1,
        grid=(nb, seq // t),
        in_specs=[
            pl.BlockSpec(lam_p.shape, lambda b, i, flags: (0, 0)),
            pl.BlockSpec(subln_col.shape, lambda b, i, flags: (0, 0)),
            q_spec(A_HEADS), kv_spec(ka), kv_spec(va),
            q_spec(B_HEADS), kv_spec(kb), kv_spec(vb), kv_spec(kn2),
            pl.BlockSpec((1, N_MAPS, t), lambda b, i, flags: (b, 0, i)),
        ],
        out_specs=pl.BlockSpec((1, t, A_WIDTH + B_WIDTH), lambda b, i, flags: (b, i, 0)),
    )
    return pl.pallas_call(
        _attend_kernel,
        grid_spec=grid_spec,
        out_shape=jax.ShapeDtypeStruct((nb, seq, A_WIDTH + B_WIDTH), _BF16),
        compiler_params=pltpu.CompilerParams(
            dimension_semantics=("arbitrary", "arbitrary"), vmem_limit_bytes=VMEM_LIMIT),
        name="attend",
    )(_flags(kn2, qn2), lam_p, subln_col, qa, ka, va, qb, kb, vb, kn2, qn2)


def _post_kernel(cat_ref, x_ref, mod_ref, wo_ref, wgu_ref, wd_ref, fn_ref, o_ref):
    mod = mod_ref[0]
    g_a = mod[:, 2 * D_MODEL:3 * D_MODEL]
    sh_f = mod[:, 3 * D_MODEL:4 * D_MODEL]
    sc_f = mod[:, 4 * D_MODEL:5 * D_MODEL]
    g_f = mod[:, 5 * D_MODEL:6 * D_MODEL]
    rows = [slice(r, r + POST_SUB) for r in range(0, POST_TILE, POST_SUB)]
    lats = [x_ref[0, r, :] + g_a * _dot(cat_ref[0, r, :], wo_ref[...]) for r in rows]
    hs = [(_rms(lat) * (1.0 + sc_f) + sh_f).astype(_BF16) for lat in lats]
    gu = [(_dot(h, wgu_ref[:, :FFN_HIDDEN]), _dot(h, wgu_ref[:, FFN_HIDDEN:])) for h in hs]
    acts = [(g / (1.0 + jnp.exp(-g)) * u).astype(_BF16) for g, u in gu]
    ffns = [_dot(a, wd_ref[...]) for a in acts]
    for r, lat, ffn in zip(rows, lats, ffns):
        o_ref[0, r, :] = _rms(lat + g_f * ffn) * fn_ref[...]


def _post(cat, x, mod3, wo, wgu, wd, fn):
    nb, seq, _ = x.shape
    t = POST_TILE

    def full(a):
        return pl.BlockSpec(a.shape, lambda b, i: (0,) * a.ndim, pipeline_mode=pl.Buffered(1))

    tok = pl.BlockSpec((1, t, D_MODEL), lambda b, i: (b, i, 0))
    return pl.pallas_call(
        _post_kernel,
        grid=(nb, seq // t),
        in_specs=[tok, tok, pl.BlockSpec((1, 1, 6 * D_MODEL), lambda b, i: (b, 0, 0)),
                  full(wo), full(wgu), full(wd), full(fn)],
        out_specs=tok,
        out_shape=jax.ShapeDtypeStruct((nb, seq, D_MODEL), _F32),
        compiler_params=pltpu.CompilerParams(
            dimension_semantics=("arbitrary", "arbitrary"), vmem_limit_bytes=VMEM_LIMIT),
        name="post",
    )(cat, x, mod3, wo, wgu, wd, fn)


def _rope_tables(seq):
    pos = np.arange(seq)
    row = (pos // GRID_W).astype(np.float32)
    col = (pos % GRID_W).astype(np.float32)
    lane = np.arange(LANES)

    def build(d, active, half, q_scale):
        quarter = half // 2
        inv = (ROPE_THETA ** (-(np.arange(quarter, dtype=np.float32)) / quarter)).astype(np.float32)
        p = np.where((d < half)[None, :], row[:, None], col[:, None]).astype(np.float32)
        ang = (p * inv[d % quarter][None, :]).astype(np.float32)
        cos = np.where(active[None, :], np.cos(ang), 1.0).astype(np.float32)
        sin = np.where(active[None, :], np.sin(ang), 0.0).astype(np.float32)
        upper = ((d % half) >= quarter)[None, :]
        s_up = np.where(upper, sin, 0.0).astype(np.float32)
        s_dn = np.where(upper, 0.0, -sin).astype(np.float32)
        ident = np.zeros((CTX_LEN, LANES), np.float32)
        nat = [jnp.asarray(np.concatenate([ident + fill, tab], 0))
               for tab, fill in ((cos, 1.0), (s_up, 0.0), (s_dn, 0.0))]
        trans = [jnp.asarray(np.ascontiguousarray(cos.T * np.float32(q_scale))),
                 jnp.asarray(np.ascontiguousarray((s_up + s_dn).T * np.float32(q_scale)))]
        return nat, trans

    nat_a, trans_a = build(lane % A_HEAD_DIM, np.ones(LANES, bool), A_HEAD_DIM // 2, A_QSCALE)
    in_rope = (lane >= B_NOPE) & (lane < B_NOPE + B_ROPE)
    nat_b, trans_b = build(np.where(in_rope, lane - B_NOPE, 0), in_rope, B_ROPE // 2, B_QSCALE)
    return nat_a + nat_b, trans_a + trans_b


def kernel(x, c, ctx, c_ctx, w_ada, b_ada, w_in, q_a_norm, kv_a_norm, w_q_up, w_kv_up,
           diff_lambda, diff_subln, w_out, w_ffn_in, w_ffn_out, final_norm):
    nb, seq, _ = x.shape
    assert w_ada.shape[0] == 1 and seq % ATT_TILE == 0 and ctx.shape[1] == CTX_LEN == TOKEN_TILE

    pad = (-(nb + 1)) % SUBLANES
    c_all = jnp.concatenate([c, c_ctx[None, :], jnp.zeros((pad, D_MODEL), _F32)], axis=0)
    mod = _adaln(c_all, w_ada[0], b_ada[0][None, :])
    mod3 = mod[:, None, :]

    w = w_in[0]
    o_k, o_v, o_cq = A_WIDTH, 2 * A_WIDTH, 3 * A_WIDTH
    o_ckv, o_kr = o_cq + B_Q_RANK, o_cq + B_Q_RANK + B_KV_RANK
    zeros = lambda n: jnp.zeros((D_MODEL, n), w.dtype)
    w_c = w[:, o_cq:o_kr].T
    w_qv = jnp.concatenate([w[:, :o_k], w[:, o_v:o_cq]], axis=1).T
    w_n = jnp.concatenate([w[:, o_k:o_v], w[:, o_ckv:o_kr], zeros(B_NOPE), w[:, o_kr:],
                           zeros(LANES - B_NOPE - B_ROPE)], axis=1)
    wq = jnp.pad(w_q_up[0].reshape(B_Q_RANK, B_HEADS, B_NOPE + B_ROPE),
                 ((0, 0), (0, 0), (0, LANES - B_NOPE - B_ROPE))).reshape(B_Q_RANK, B_HEADS * LANES)
    wkv3 = w_kv_up[0].reshape(B_KV_RANK, B_HEADS, B_NOPE + B_VDIM)
    wk = jnp.pad(wkv3[:, :, :B_NOPE], ((0, 0), (0, 0), (0, LANES - B_NOPE))).reshape(B_KV_RANK, B_HEADS * LANES)
    wv = wkv3[:, :, B_NOPE:].reshape(B_KV_RANK, B_WIDTH)
    sel = np.zeros((N_MAPS, (A_HEADS + B_HEADS) * LANES), np.float32)
    for hd in range(A_HEADS):
        sel[2 * hd, hd * LANES:hd * LANES + A_HEAD_DIM] = 1.0
        sel[2 * hd + 1, hd * LANES + A_HEAD_DIM:(hd + 1) * LANES] = 1.0
    for hd in range(B_HEADS):
        sel[2 * A_HEADS + hd, (A_HEADS + hd) * LANES:(A_HEADS + hd + 1) * LANES] = 1.0
    weights = (w_c.astype(_BF16), w_n.astype(_BF16), w_qv.astype(_BF16),
               q_a_norm.reshape(B_Q_RANK, 1), kv_a_norm.reshape(B_KV_RANK, 1),
               kv_a_norm.reshape(1, B_KV_RANK),
               wq.T.astype(_BF16), wv.T.astype(_BF16), wk.astype(_BF16), jnp.asarray(sel, _BF16))

    tables_nat, tables_t = _rope_tables(seq)
    qa, ka, va, qb, kb, vb, kn2, qn2 = _proj(x, ctx, mod3, weights, tables_nat, tables_t)

    cat = _attend(diff_lambda[0], diff_subln.reshape(LANES, 1), qa, ka, va, qb, kb, vb, kn2, qn2)

    return _post(cat, x, mod3, w_out[0].astype(_BF16), w_ffn_in[0].astype(_BF16),
                 w_ffn_out[0].astype(_BF16), final_norm[None, :])
```

```python
import math

import numpy as np
import jax
import jax.numpy as jnp
from jax import lax
from jax.experimental import pallas as pl
from jax.experimental.pallas import tpu as pltpu

D_MODEL = 1024
CTX_LEN = 256
GRID_W = 64
ROPE_THETA = 10000.0
NORM_EPS = 1e-6

A_HEADS = 4
A_HEAD_DIM = 64
A_WIDTH = A_HEADS * 2 * A_HEAD_DIM
B_HEADS = 8
B_NOPE = 64
B_ROPE = 32
B_VDIM = 64
B_Q_RANK = 256
B_KV_RANK = 128
B_WIDTH = B_HEADS * B_VDIM
FFN_HIDDEN = 2816

LANES = 128
SUBLANES = 8
TOKEN_TILE = 256
ATT_TILE = 256
ADA_TILE = 2048
PROJ_ROWS = 4
POST_TILE = 512
POST_SUB = 256
V_PAD = 16
N_MAPS = 2 * A_HEADS + B_HEADS
V7X_VMEM_BYTES = 64 * 1024 * 1024
VMEM_LIMIT = V7X_VMEM_BYTES - 8 * 1024 * 1024

LAM_INIT = 0.8 - 0.6 * math.exp(-0.3 * 0)
LOG2E = 1.4426950408889634
A_QSCALE = A_HEAD_DIM ** -0.5 * LOG2E
B_QSCALE = (B_NOPE + B_ROPE) ** -0.5 * LOG2E

FAST_LIMIT = -1.0
BOUND_MARGIN = 1.02
FAST_DEPTH = 1
EXACT_DEPTH = 2

_F32 = jnp.float32
_BF16 = jnp.bfloat16


def _dot(a, b):
    return jnp.dot(a, b, preferred_element_type=_F32)


def _dot_nt(a, b):
    return lax.dot_general(a, b, (((1,), (1,)), ((), ())), preferred_element_type=_F32)


def _rms(x, axis=-1):
    return x * lax.rsqrt(jnp.mean(x * x, axis=axis, keepdims=True) + NORM_EPS)


def _adaln_kernel(c_ref, w_ref, b_ref, o_ref):
    c = c_ref[...]
    s = c / (1.0 + jnp.exp(-c))
    o_ref[...] = _dot(s.astype(_BF16), w_ref[...].astype(_BF16)) + b_ref[...]


def _adaln(c_all, w_ada, b_ada):
    rows = c_all.shape[0]
    n = w_ada.shape[1]
    return pl.pallas_call(
        _adaln_kernel,
        grid=(n // ADA_TILE,),
        in_specs=[
            pl.BlockSpec((rows, D_MODEL), lambda i: (0, 0)),
            pl.BlockSpec((D_MODEL, ADA_TILE), lambda i: (0, i)),
            pl.BlockSpec((1, ADA_TILE), lambda i: (0, i)),
        ],
        out_specs=pl.BlockSpec((rows, ADA_TILE), lambda i: (0, i)),
        out_shape=jax.ShapeDtypeStruct((rows, n), _F32),
        name="adaln",
    )(c_all, w_ada, b_ada)


def _rope(x, cos, s_up, s_dn, quarter):
    return (x * cos + pltpu.roll(x, quarter, 1) * s_up
            + pltpu.roll(x, LANES - quarter, 1) * s_dn)


def _rope_t(x, cos_t, sin_t, lo, hi, quarter):
    parts = [x[:lo]] if lo else []
    for r in range(lo, hi, 2 * quarter):
        parts += [x[r + quarter:r + 2 * quarter], x[r:r + quarter]]
    if hi < x.shape[0]:
        parts.append(x[hi:])
    return x * cos_t + jnp.concatenate(parts, axis=0) * sin_t


def _proj_kernel(x_ref, ctx_ref, mod_ref, modc_ref, wc_ref, wn_ref, wqv_ref, qn_ref, kvn_col_ref,
                 kvn_row_ref, wqt_ref, wvt_ref, wk_ref, sel_ref,
                 ca_ref, ua_ref, da_ref, cb_ref, ub_ref, db_ref,
                 cat_ref, sat_ref, cbt_ref, sbt_ref,
                 qa_ref, ka_ref, va_ref, qb_ref, kb_ref, vb_ref, kn2_ref, qn2_ref):
    t = TOKEN_TILE
    chains = range(PROJ_ROWS)
    is_ctx = pl.program_id(1) == 0

    hs, hts = [], []
    for i in chains:
        xin = jnp.where(is_ctx, ctx_ref[i], x_ref[i])
        mod = jnp.where(is_ctx, modc_ref[0], mod_ref[i])
        shift = mod[:, 0:D_MODEL]
        scale = mod[:, D_MODEL:2 * D_MODEL]
        h32 = _rms(xin) * (1.0 + scale) + shift
        hs.append(h32.astype(_BF16))
        hts.append(hs[-1].T)

    ycs = [_dot(wc_ref[...], ht) for ht in hts]
    ys = [_dot(h, wn_ref[...]) for h in hs]
    yqvs = [_dot(wqv_ref[...], ht) for ht in hts]
    qbs = [_dot(wqt_ref[...], (_rms(yc[:B_Q_RANK], 0) * qn_ref[...]).astype(_BF16))
           for yc in ycs]
    vbs = [_dot(wvt_ref[...], (_rms(yc[B_Q_RANK:], 0) * kvn_col_ref[...]).astype(_BF16))
           for yc in ycs]
    kns = [_dot((_rms(y[:, A_WIDTH:A_WIDTH + B_KV_RANK]) * kvn_row_ref[...]).astype(_BF16), wk_ref[...])
           for y in ys]

    row16 = lax.broadcasted_iota(jnp.int32, (V_PAD, t), 0)
    ones_rows = jnp.where(row16 == 0, 1.0, 0.0).astype(_BF16)
    lane = lax.broadcasted_iota(jnp.int32, (t, LANES), 1)
    cat, sat = cat_ref[...], sat_ref[...]
    cbt, sbt = cbt_ref[...], sbt_ref[...]
    ca, ua, da = ca_ref[...], ua_ref[...], da_ref[...]
    cb, ub, db = cb_ref[...], ub_ref[...], db_ref[...]

    def col_sq_norm(q):
        return jnp.sum(q * q, axis=0, keepdims=True)

    def squares(k):
        return (k * k).astype(_BF16)

    for i in chains:
        yqv_t, qb_t, vb_t, y, kn = yqvs[i], qbs[i], vbs[i], ys[i], kns[i]
        qn2_rows = []
        for hd in range(A_HEADS):
            q_t = yqv_t[hd * LANES:(hd + 1) * LANES]
            q = _rope_t(q_t, cat, sat, 0, LANES, A_HEAD_DIM // 4)
            qa_ref[i, hd] = q.astype(_BF16)
            qn2_rows += [col_sq_norm(q[:A_HEAD_DIM]), col_sq_norm(q[A_HEAD_DIM:])]
            va_ref[i, hd] = yqv_t[A_WIDTH + hd * LANES:A_WIDTH + (hd + 1) * LANES].astype(_BF16)
        for hd in range(B_HEADS):
            blk = qb_t[hd * LANES:(hd + 1) * LANES]
            q = _rope_t(blk, cbt, sbt, B_NOPE, B_NOPE + B_ROPE, B_ROPE // 4)
            qb_ref[i, hd] = q.astype(_BF16)
            qn2_rows.append(col_sq_norm(q))
            vb_ref[i, hd, :B_VDIM, :] = vb_t[hd * B_VDIM:(hd + 1) * B_VDIM].astype(_BF16)
            vb_ref[i, hd, B_VDIM:, :] = ones_rows
        qn2_ref[i] = jnp.concatenate(qn2_rows, axis=0)

        ksq = []
        for hd in range(A_HEADS):
            k = _rope(y[:, hd * LANES:(hd + 1) * LANES], ca, ua, da, A_HEAD_DIM // 4)
            ka_ref[i, hd] = k.astype(_BF16)
            ksq.append(squares(k))
        kr_rot = _rope(y[:, A_WIDTH + B_KV_RANK:], cb, ub, db, B_ROPE // 4)
        for hd in range(B_HEADS):
            k = jnp.where(lane < B_NOPE, kn[:, hd * LANES:(hd + 1) * LANES], kr_rot)
            kb_ref[i, hd] = k.astype(_BF16)
            ksq.append(squares(k))
        kn2_ref[i] = _dot_nt(sel_ref[...], jnp.concatenate(ksq, axis=1))


def _proj(x, ctx, mod3, weights, tables_nat, tables_t):
    nb, seq, _ = x.shape
    n_tok = CTX_LEN + seq
    n_tiles = n_tok // TOKEN_TILE
    t = TOKEN_TILE

    def full(a):
        return pl.BlockSpec(a.shape, lambda b, j: (0,) * a.ndim)

    lat = lambda j: jnp.maximum(j - 1, 0)
    out_shapes = (
        jax.ShapeDtypeStruct((nb, A_HEADS, LANES, seq), _BF16),
        jax.ShapeDtypeStruct((nb, A_HEADS, n_tok, LANES), _BF16),
        jax.ShapeDtypeStruct((nb, A_HEADS, LANES, n_tok), _BF16),
        jax.ShapeDtypeStruct((nb, B_HEADS, LANES, seq), _BF16),
        jax.ShapeDtypeStruct((nb, B_HEADS, n_tok, LANES), _BF16),
        jax.ShapeDtypeStruct((nb, B_HEADS, B_VDIM + V_PAD, n_tok), _BF16),
        jax.ShapeDtypeStruct((nb, N_MAPS, n_tok), _F32),
        jax.ShapeDtypeStruct((nb, N_MAPS, seq), _F32),
    )
    r = PROJ_ROWS
    out_specs = (
        pl.BlockSpec((r, A_HEADS, LANES, t), lambda b, j: (b, 0, 0, lat(j))),
        pl.BlockSpec((r, A_HEADS, t, LANES), lambda b, j: (b, 0, j, 0)),
        pl.BlockSpec((r, A_HEADS, LANES, t), lambda b, j: (b, 0, 0, j)),
        pl.BlockSpec((r, B_HEADS, LANES, t), lambda b, j: (b, 0, 0, lat(j))),
        pl.BlockSpec((r, B_HEADS, t, LANES), lambda b, j: (b, 0, j, 0)),
        pl.BlockSpec((r, B_HEADS, B_VDIM + V_PAD, t), lambda b, j: (b, 0, 0, j)),
        pl.BlockSpec((r, N_MAPS, t), lambda b, j: (b, 0, j)),
        pl.BlockSpec((r, N_MAPS, t), lambda b, j: (b, 0, lat(j))),
    )
    return pl.pallas_call(
        _proj_kernel,
        grid=(nb // r, n_tiles),
        in_specs=[
            pl.BlockSpec((r, t, D_MODEL), lambda b, j: (b, lat(j), 0)),
            pl.BlockSpec((r, t, D_MODEL), lambda b, j: (b, 0, 0)),
            pl.BlockSpec((r, 1, 6 * D_MODEL), lambda b, j: (b, 0, 0)),
            pl.BlockSpec((1, 1, 6 * D_MODEL), lambda b, j: (nb, 0, 0)),
        ] + [full(w) for w in weights]
          + [pl.BlockSpec((t, LANES), lambda b, j: (j, 0))] * len(tables_nat)
          + [pl.BlockSpec((LANES, t), lambda b, j: (0, lat(j)))] * len(tables_t),
        out_specs=out_specs,
        out_shape=out_shapes,
        compiler_params=pltpu.CompilerParams(
            dimension_semantics=("arbitrary", "arbitrary"), vmem_limit_bytes=VMEM_LIMIT),
        name="proj",
    )(x, ctx, mod3, mod3, *weights, *tables_nat, *tables_t)


def _score_bound(qn2, kn2):
    return jnp.sqrt(qn2 * jnp.max(kn2, axis=-1, keepdims=True)) * BOUND_MARGIN


def _flags_kernel(kn2_ref, qn2_ref, o_ref):
    kn2 = kn2_ref[...]
    cols = []
    for i in range(qn2_ref.shape[2] // ATT_TILE):
        bound = _score_bound(qn2_ref[:, :, i * ATT_TILE:(i + 1) * ATT_TILE], kn2)
        worst = jnp.max(jnp.max(bound, axis=2), axis=1, keepdims=True)
        cols.append(jnp.where(worst <= FAST_LIMIT, 1, 0).astype(jnp.int32))
    o_ref[...] = jnp.concatenate(cols, axis=1)


def _flags(kn2, qn2):
    nb, _, seq = qn2.shape
    return pl.pallas_call(
        _flags_kernel,
        out_shape=jax.ShapeDtypeStruct((nb, seq // ATT_TILE), jnp.int32),
        name="flags",
    )(kn2, qn2)


def _attend_body(use_bound, depth, lam_ref, subln_ref, qa_ref, ka_ref, va_ref, qb_ref, kb_ref, vb_ref,
                 kn2_ref, qn2_ref, o_ref):
    t = ATT_TILE
    lp = lam_ref[...]
    lam = (jnp.exp(jnp.sum(lp[0:1] * lp[1:2], axis=-1, keepdims=True))
           - jnp.exp(jnp.sum(lp[2:3] * lp[3:4], axis=-1, keepdims=True)) + LAM_INIT)
    gain = subln_ref[...] * (1.0 - LAM_INIT)
    row = lax.broadcasted_iota(jnp.int32, (LANES, t), 0)
    lo = row < A_HEAD_DIM

    maps = []
    for hd in range(A_HEADS):
        q_t = qa_ref[0, hd]
        zero = jnp.zeros_like(q_t)
        for half in range(2):
            q_m = jnp.where(lo, q_t, zero) if half == 0 else jnp.where(lo, zero, q_t)
            maps.append((q_m, ka_ref.at[0, hd], va_ref.at[0, hd], LANES))
    for hd in range(B_HEADS):
        maps.append((qb_ref[0, hd], kb_ref.at[0, hd], vb_ref.at[0, hd], B_VDIM))

    def scores_t(i):
        q_m, k_ref, _, _ = maps[i]
        return _dot(k_ref[...], q_m)

    bounds = _score_bound(qn2_ref[0], kn2_ref[0]) if use_bound else None

    def finish(i, o_t, acc):
        if i < 2 * A_HEADS:
            hd, half = divmod(i, 2)
            if half == 0:
                acc[hd] = o_t
                return
            o = acc.pop(hd) - lam * o_t
            o = _rms(o, 0) * gain
            o_ref[0, :, hd * LANES:(hd + 1) * LANES] = o.T.astype(o_ref.dtype)
        else:
            hd = i - 2 * A_HEADS
            if hd % 2 == 0:
                acc[hd] = o_t
                return
            o = jnp.concatenate([acc.pop(hd - 1), o_t], axis=0)
            p = hd // 2
            o_ref[0, :, A_WIDTH + p * LANES:A_WIDTH + (p + 1) * LANES] = o.T.astype(o_ref.dtype)

    acc = {}

    def softmax_pv(i, s_t):
        _, _, v_ref, dv = maps[i]
        m = bounds[i:i + 1] if use_bound else jnp.max(s_t, axis=0, keepdims=True)
        p = jnp.exp2(s_t - m)
        if v_ref.shape[0] == dv:
            o = _dot(v_ref[...], p.astype(_BF16)) / jnp.sum(p, axis=0, keepdims=True)
        else:
            o = _dot(v_ref[...], p.astype(_BF16))
            o = o[:dv] / o[dv:dv + 1]
        finish(i, o, acc)

    pending = []
    for i in range(len(maps)):
        pending.append((i, scores_t(i)))
        if len(pending) > depth:
            softmax_pv(*pending.pop(0))
    for item in pending:
        softmax_pv(*item)


def _attend_kernel(flags_ref, *refs):
    fast_ok = flags_ref[pl.program_id(0), pl.program_id(1)] != 0

    @pl.when(fast_ok)
    def _():
        _attend_body(True, FAST_DEPTH, *refs)

    @pl.when(jnp.logical_not(fast_ok))
    def _():
        _attend_body(False, EXACT_DEPTH, *refs)


def _attend(lam_p, subln_col, qa, ka, va, qb, kb, vb, kn2, qn2):
    nb, _, _, seq = qa.shape
    t = ATT_TILE

    def q_spec(nh):
        return pl.BlockSpec((1, nh, LANES, t), lambda b, i, flags: (b, 0, 0, i))

    def kv_spec(a):
        return pl.BlockSpec((1,) + a.shape[1:], lambda b, i, flags: (b,) + (0,) * (a.ndim - 1))

    grid_spec = pltpu.PrefetchScalarGridSpec(
        num_scalar_prefetch=1,
        grid=(nb, seq // t),
        in_specs=[
            pl.BlockSpec(lam_p.shape, lambda b, i, flags: (0, 0)),
            pl.BlockSpec(subln_col.shape, lambda b, i, flags: (0, 0)),
            q_spec(A_HEADS), kv_spec(ka), kv_spec(va),
            q_spec(B_HEADS), kv_spec(kb), kv_spec(vb), kv_spec(kn2),
            pl.BlockSpec((1, N_MAPS, t), lambda b, i, flags: (b, 0, i)),
        ],
        out_specs=pl.BlockSpec((1, t, A_WIDTH + B_WIDTH), lambda b, i, flags: (b, i, 0)),
    )
    return pl.pallas_call(
        _attend_kernel,
        grid_spec=grid_spec,
        out_shape=jax.ShapeDtypeStruct((nb, seq, A_WIDTH + B_WIDTH), _BF16),
        compiler_params=pltpu.CompilerParams(
            dimension_semantics=("arbitrary", "arbitrary"), vmem_limit_bytes=VMEM_LIMIT),
        name="attend",
    )(_flags(kn2, qn2), lam_p, subln_col, qa, ka, va, qb, kb, vb, kn2, qn2)


def _post_kernel(cat_ref, x_ref, mod_ref, wo_ref, wgu_ref, wd_ref, fn_ref, o_ref):
    mod = mod_ref[0]
    g_a = mod[:, 2 * D_MODEL:3 * D_MODEL]
    sh_f = mod[:, 3 * D_MODEL:4 * D_MODEL]
    sc_f = mod[:, 4 * D_MODEL:5 * D_MODEL]
    g_f = mod[:, 5 * D_MODEL:6 * D_MODEL]
    rows = [slice(r, r + POST_SUB) for r in range(0, POST_TILE, POST_SUB)]
    lats = [x_ref[0, r, :] + g_a * _dot(cat_ref[0, r, :], wo_ref[...]) for r in rows]
    hs = [(_rms(lat) * (1.0 + sc_f) + sh_f).astype(_BF16) for lat in lats]
    gu = [(_dot(h, wgu_ref[:, :FFN_HIDDEN]), _dot(h, wgu_ref[:, FFN_HIDDEN:])) for h in hs]
    acts = [(g / (1.0 + jnp.exp(-g)) * u).astype(_BF16) for g, u in gu]
    ffns = [_dot(a, wd_ref[...]) for a in acts]
    for r, lat, ffn in zip(rows, lats, ffns):
        o_ref[0, r, :] = _rms(lat + g_f * ffn) * fn_ref[...]


def _post(cat, x, mod3, wo, wgu, wd, fn):
    nb, seq, _ = x.shape
    t = POST_TILE

    def full(a):
        return pl.BlockSpec(a.shape, lambda b, i: (0,) * a.ndim, pipeline_mode=pl.Buffered(1))

    tok = pl.BlockSpec((1, t, D_MODEL), lambda b, i: (b, i, 0))
    return pl.pallas_call(
        _post_kernel,
        grid=(nb, seq // t),
        in_specs=[tok, tok, pl.BlockSpec((1, 1, 6 * D_MODEL), lambda b, i: (b, 0, 0)),
                  full(wo), full(wgu), full(wd), full(fn)],
        out_specs=tok,
        out_shape=jax.ShapeDtypeStruct((nb, seq, D_MODEL), _F32),
        compiler_params=pltpu.CompilerParams(
            dimension_semantics=("arbitrary", "arbitrary"), vmem_limit_bytes=VMEM_LIMIT),
        name="post",
    )(cat, x, mod3, wo, wgu, wd, fn)


def _rope_tables(seq):
    pos = np.arange(seq)
    row = (pos // GRID_W).astype(np.float32)
    col = (pos % GRID_W).astype(np.float32)
    lane = np.arange(LANES)

    def build(d, active, half, q_scale):
        quarter = half // 2
        inv = (ROPE_THETA ** (-(np.arange(quarter, dtype=np.float32)) / quarter)).astype(np.float32)
        p = np.where((d < half)[None, :], row[:, None], col[:, None]).astype(np.float32)
        ang = (p * inv[d % quarter][None, :]).astype(np.float32)
        cos = np.where(active[None, :], np.cos(ang), 1.0).astype(np.float32)
        sin = np.where(active[None, :], np.sin(ang), 0.0).astype(np.float32)
        upper = ((d % half) >= quarter)[None, :]
        s_up = np.where(upper, sin, 0.0).astype(np.float32)
        s_dn = np.where(upper, 0.0, -sin).astype(np.float32)
        ident = np.zeros((CTX_LEN, LANES), np.float32)
        nat = [jnp.asarray(np.concatenate([ident + fill, tab], 0))
               for tab, fill in ((cos, 1.0), (s_up, 0.0), (s_dn, 0.0))]
        trans = [jnp.asarray(np.ascontiguousarray(cos.T * np.float32(q_scale))),
                 jnp.asarray(np.ascontiguousarray((s_up + s_dn).T * np.float32(q_scale)))]
        return nat, trans

    nat_a, trans_a = build(lane % A_HEAD_DIM, np.ones(LANES, bool), A_HEAD_DIM // 2, A_QSCALE)
    in_rope = (lane >= B_NOPE) & (lane < B_NOPE + B_ROPE)
    nat_b, trans_b = build(np.where(in_rope, lane - B_NOPE, 0), in_rope, B_ROPE // 2, B_QSCALE)
    return nat_a + nat_b, trans_a + trans_b


def kernel(x, c, ctx, c_ctx, w_ada, b_ada, w_in, q_a_norm, kv_a_norm, w_q_up, w_kv_up,
           diff_lambda, diff_subln, w_out, w_ffn_in, w_ffn_out, final_norm):
    nb, seq, _ = x.shape
    assert w_ada.shape[0] == 1 and seq % ATT_TILE == 0 and ctx.shape[1] == CTX_LEN == TOKEN_TILE

    pad = (-(nb + 1)) % SUBLANES
    c_all = jnp.concatenate([c, c_ctx[None, :], jnp.zeros((pad, D_MODEL), _F32)], axis=0)
    mod = _adaln(c_all, w_ada[0], b_ada[0][None, :])
    mod3 = mod[:, None, :]

    w = w_in[0]
    o_k, o_v, o_cq = A_WIDTH, 2 * A_WIDTH, 3 * A_WIDTH
    o_ckv, o_kr = o_cq + B_Q_RANK, o_cq + B_Q_RANK + B_KV_RANK
    zeros = lambda n: jnp.zeros((D_MODEL, n), w.dtype)
    w_c = w[:, o_cq:o_kr].T
    w_qv = jnp.concatenate([w[:, :o_k], w[:, o_v:o_cq]], axis=1).T
    w_n = jnp.concatenate([w[:, o_k:o_v], w[:, o_ckv:o_kr], zeros(B_NOPE), w[:, o_kr:],
                           zeros(LANES - B_NOPE - B_ROPE)], axis=1)
    wq = jnp.pad(w_q_up[0].reshape(B_Q_RANK, B_HEADS, B_NOPE + B_ROPE),
                 ((0, 0), (0, 0), (0, LANES - B_NOPE - B_ROPE))).reshape(B_Q_RANK, B_HEADS * LANES)
    wkv3 = w_kv_up[0].reshape(B_KV_RANK, B_HEADS, B_NOPE + B_VDIM)
    wk = jnp.pad(wkv3[:, :, :B_NOPE], ((0, 0), (0, 0), (0, LANES - B_NOPE))).reshape(B_KV_RANK, B_HEADS * LANES)
    wv = wkv3[:, :, B_NOPE:].reshape(B_KV_RANK, B_WIDTH)
    sel = np.zeros((N_MAPS, (A_HEADS + B_HEADS) * LANES), np.float32)
    for hd in range(A_HEADS):
        sel[2 * hd, hd * LANES:hd * LANES + A_HEAD_DIM] = 1.0
        sel[2 * hd + 1, hd * LANES + A_HEAD_DIM:(hd + 1) * LANES] = 1.0
    for hd in range(B_HEADS):
        sel[2 * A_HEADS + hd, (A_HEADS + hd) * LANES:(A_HEADS + hd + 1) * LANES] = 1.0
    weights = (w_c.astype(_BF16), w_n.astype(_BF16), w_qv.astype(_BF16),
               q_a_norm.reshape(B_Q_RANK, 1), kv_a_norm.reshape(B_KV_RANK, 1),
               kv_a_norm.reshape(1, B_KV_RANK),
               wq.T.astype(_BF16), wv.T.astype(_BF16), wk.astype(_BF16), jnp.asarray(sel, _BF16))

    tables_nat, tables_t = _rope_tables(seq)
    qa, ka, va, qb, kb, vb, kn2, qn2 = _proj(x, ctx, mod3, weights, tables_nat, tables_t)

    cat = _attend(diff_lambda[0], diff_subln.reshape(LANES, 1), qa, ka, va, qb, kb, vb, kn2, qn2)

    return _post(cat, x, mod3, w_out[0].astype(_BF16), w_ffn_in[0].astype(_BF16),
                 w_ffn_out[0].astype(_BF16), final_norm[None, :])
```

```python
import math

import numpy as np
import jax
import jax.numpy as jnp
from jax import lax
from jax.experimental import pallas as pl
from jax.experimental.pallas import tpu as pltpu

D_MODEL = 1024
CTX_LEN = 256
GRID_W = 64
ROPE_THETA = 10000.0
NORM_EPS = 1e-6

A_HEADS = 4
A_HEAD_DIM = 64
A_WIDTH = A_HEADS * 2 * A_HEAD_DIM
B_HEADS = 8
B_NOPE = 64
B_ROPE = 32
B_VDIM = 64
B_Q_RANK = 256
B_KV_RANK = 128
B_WIDTH = B_HEADS * B_VDIM
FFN_HIDDEN = 2816

LANES = 128
SUBLANES = 8
TOKEN_TILE = 256
ATT_TILE = 256
ADA_TILE = 2048
PROJ_ROWS = 4
PROJ_LAG = 1
POST_TILE = 512
POST_SUB = 256
V_PAD = 16
N_MAPS = 2 * A_HEADS + B_HEADS
V7X_VMEM_BYTES = 64 * 1024 * 1024
VMEM_LIMIT = V7X_VMEM_BYTES - 8 * 1024 * 1024

LAM_INIT = 0.8 - 0.6 * math.exp(-0.3 * 0)
LOG2E = 1.4426950408889634
A_QSCALE = A_HEAD_DIM ** -0.5 * LOG2E
B_QSCALE = (B_NOPE + B_ROPE) ** -0.5 * LOG2E

FAST_LIMIT = 50.0
BOUND_MARGIN = 1.02
FAST_DEPTH = 1
EXACT_DEPTH = 2

_F32 = jnp.float32
_BF16 = jnp.bfloat16


def _dot(a, b):
    return jnp.dot(a, b, preferred_element_type=_F32)


def _dot_nt(a, b):
    return lax.dot_general(a, b, (((1,), (1,)), ((), ())), preferred_element_type=_F32)


def _rms(x, axis=-1):
    return x * lax.rsqrt(jnp.mean(x * x, axis=axis, keepdims=True) + NORM_EPS)


def _adaln_kernel(c_ref, w_ref, b_ref, o_ref):
    c = c_ref[...]
    s = c / (1.0 + jnp.exp(-c))
    o_ref[...] = _dot(s.astype(_BF16), w_ref[...].astype(_BF16)) + b_ref[...]


def _adaln(c_all, w_ada, b_ada):
    rows = c_all.shape[0]
    n = w_ada.shape[1]
    return pl.pallas_call(
        _adaln_kernel,
        grid=(n // ADA_TILE,),
        in_specs=[
            pl.BlockSpec((rows, D_MODEL), lambda i: (0, 0)),
            pl.BlockSpec((D_MODEL, ADA_TILE), lambda i: (0, i)),
            pl.BlockSpec((1, ADA_TILE), lambda i: (0, i)),
        ],
        out_specs=pl.BlockSpec((rows, ADA_TILE), lambda i: (0, i)),
        out_shape=jax.ShapeDtypeStruct((rows, n), _F32),
        name="adaln",
    )(c_all, w_ada, b_ada)


def _rope(x, cos, s_up, s_dn, quarter):
    return (x * cos + pltpu.roll(x, quarter, 1) * s_up
            + pltpu.roll(x, LANES - quarter, 1) * s_dn)


def _rope_t(x, cos_t, sin_t, lo, hi, quarter):
    parts = [x[:lo]] if lo else []
    for r in range(lo, hi, 2 * quarter):
        parts += [x[r + quarter:r + 2 * quarter], x[r:r + quarter]]
    if hi < x.shape[0]:
        parts.append(x[hi:])
    return x * cos_t + jnp.concatenate(parts, axis=0) * sin_t


def _proj_kernel(x_ref, ctx_ref, mod_ref, modc_ref, wc_ref, wn_ref, wqv_ref, qn_ref, kvn_col_ref,
                 kvn_row_ref, wqt_ref, wvt_ref, wk_ref, sel_ref,
                 ca_ref, ua_ref, da_ref, cb_ref, ub_ref, db_ref,
                 cat_ref, sat_ref, cbt_ref, sbt_ref,
                 qa_ref, ka_ref, va_ref, qb_ref, kb_ref, vb_ref, kn2_ref, qn2_ref):
    t = TOKEN_TILE
    chains = range(PROJ_ROWS)
    is_ctx = pl.program_id(1) == 0

    hs, hts = [], []
    for i in chains:
        xin = jnp.where(is_ctx, ctx_ref[i], x_ref[i])
        mod = jnp.where(is_ctx, modc_ref[0], mod_ref[i])
        shift = mod[:, 0:D_MODEL]
        scale = mod[:, D_MODEL:2 * D_MODEL]
        h32 = _rms(xin) * (1.0 + scale) + shift
        hs.append(h32.astype(_BF16))
        hts.append(hs[-1].T)

    ycs, ys, yqvs, qbs, vbs, kns = [], [], [], [], [], []

    def first_level(i):
        ycs.append(_dot(wc_ref[...], hts[i]))
        ys.append(_dot(hs[i], wn_ref[...]))
        yqvs.append(_dot(wqv_ref[...], hts[i]))

    def second_level(i):
        yc, y = ycs[i], ys[i]
        qbs.append(_dot(wqt_ref[...], (_rms(yc[:B_Q_RANK], 0) * qn_ref[...]).astype(_BF16)))
        vbs.append(_dot(wvt_ref[...], (_rms(yc[B_Q_RANK:], 0) * kvn_col_ref[...]).astype(_BF16)))
        kns.append(_dot((_rms(y[:, A_WIDTH:A_WIDTH + B_KV_RANK]) * kvn_row_ref[...]).astype(_BF16),
                        wk_ref[...]))

    for i in chains:
        first_level(i)
        if i >= PROJ_LAG:
            second_level(i - PROJ_LAG)
    for i in range(PROJ_ROWS - PROJ_LAG, PROJ_ROWS):
        second_level(i)

    row16 = lax.broadcasted_iota(jnp.int32, (V_PAD, t), 0)
    ones_rows = jnp.where(row16 == 0, 1.0, 0.0).astype(_BF16)
    lane = lax.broadcasted_iota(jnp.int32, (t, LANES), 1)
    cat, sat = cat_ref[...], sat_ref[...]
    cbt, sbt = cbt_ref[...], sbt_ref[...]
    ca, ua, da = ca_ref[...], ua_ref[...], da_ref[...]
    cb, ub, db = cb_ref[...], ub_ref[...], db_ref[...]

    def col_sq_norm(q):
        return jnp.sum(q * q, axis=0, keepdims=True)

    def squares(k):
        return (k * k).astype(_BF16)

    for i in chains:
        yqv_t, qb_t, vb_t, y, kn = yqvs[i], qbs[i], vbs[i], ys[i], kns[i]
        qn2_rows = []
        for hd in range(A_HEADS):
            q_t = yqv_t[hd * LANES:(hd + 1) * LANES]
            q = _rope_t(q_t, cat, sat, 0, LANES, A_HEAD_DIM // 4)
            qa_ref[i, hd] = q.astype(_BF16)
            qn2_rows += [col_sq_norm(q[:A_HEAD_DIM]), col_sq_norm(q[A_HEAD_DIM:])]
            va_ref[i, hd] = yqv_t[A_WIDTH + hd * LANES:A_WIDTH + (hd + 1) * LANES].astype(_BF16)
        for hd in range(B_HEADS):
            blk = qb_t[hd * LANES:(hd + 1) * LANES]
            q = _rope_t(blk, cbt, sbt, B_NOPE, B_NOPE + B_ROPE, B_ROPE // 4)
            qb_ref[i, hd] = q.astype(_BF16)
            qn2_rows.append(col_sq_norm(q))
            vb_ref[i, hd, :B_VDIM, :] = vb_t[hd * B_VDIM:(hd + 1) * B_VDIM].astype(_BF16)
            vb_ref[i, hd, B_VDIM:, :] = ones_rows
        qn2_ref[i] = jnp.concatenate(qn2_rows, axis=0)

        ksq = []
        for hd in range(A_HEADS):
            k = _rope(y[:, hd * LANES:(hd + 1) * LANES], ca, ua, da, A_HEAD_DIM // 4)
            ka_ref[i, hd] = k.astype(_BF16)
            ksq.append(squares(k))
        kr_rot = _rope(y[:, A_WIDTH + B_KV_RANK:], cb, ub, db, B_ROPE // 4)
        for hd in range(B_HEADS):
            k = jnp.where(lane < B_NOPE, kn[:, hd * LANES:(hd + 1) * LANES], kr_rot)
            kb_ref[i, hd] = k.astype(_BF16)
            ksq.append(squares(k))
        kn2_ref[i] = _dot_nt(sel_ref[...], jnp.concatenate(ksq, axis=1))


def _proj(x, ctx, mod3, weights, tables_nat, tables_t):
    nb, seq, _ = x.shape
    n_tok = CTX_LEN + seq
    n_tiles = n_tok // TOKEN_TILE
    t = TOKEN_TILE

    def full(a):
        return pl.BlockSpec(a.shape, lambda b, j: (0,) * a.ndim)

    lat = lambda j: jnp.maximum(j - 1, 0)
    out_shapes = (
        jax.ShapeDtypeStruct((nb, A_HEADS, LANES, seq), _BF16),
        jax.ShapeDtypeStruct((nb, A_HEADS, n_tok, LANES), _BF16),
        jax.ShapeDtypeStruct((nb, A_HEADS, LANES, n_tok), _BF16),
        jax.ShapeDtypeStruct((nb, B_HEADS, LANES, seq), _BF16),
        jax.ShapeDtypeStruct((nb, B_HEADS, n_tok, LANES), _BF16),
        jax.ShapeDtypeStruct((nb, B_HEADS, B_VDIM + V_PAD, n_tok), _BF16),
        jax.ShapeDtypeStruct((nb, N_MAPS, n_tok), _F32),
        jax.ShapeDtypeStruct((nb, N_MAPS, seq), _F32),
    )
    r = PROJ_ROWS
    out_specs = (
        pl.BlockSpec((r, A_HEADS, LANES, t), lambda b, j: (b, 0, 0, lat(j))),
        pl.BlockSpec((r, A_HEADS, t, LANES), lambda b, j: (b, 0, j, 0)),
        pl.BlockSpec((r, A_HEADS, LANES, t), lambda b, j: (b, 0, 0, j)),
        pl.BlockSpec((r, B_HEADS, LANES, t), lambda b, j: (b, 0, 0, lat(j))),
        pl.BlockSpec((r, B_HEADS, t, LANES), lambda b, j: (b, 0, j, 0)),
        pl.BlockSpec((r, B_HEADS, B_VDIM + V_PAD, t), lambda b, j: (b, 0, 0, j)),
        pl.BlockSpec((r, N_MAPS, t), lambda b, j: (b, 0, j)),
        pl.BlockSpec((r, N_MAPS, t), lambda b, j: (b, 0, lat(j))),
    )
    return pl.pallas_call(
        _proj_kernel,
        grid=(nb // r, n_tiles),
        in_specs=[
            pl.BlockSpec((r, t, D_MODEL), lambda b, j: (b, lat(j), 0)),
            pl.BlockSpec((r, t, D_MODEL), lambda b, j: (b, 0, 0)),
            pl.BlockSpec((r, 1, 6 * D_MODEL), lambda b, j: (b, 0, 0)),
            pl.BlockSpec((1, 1, 6 * D_MODEL), lambda b, j: (nb, 0, 0)),
        ] + [full(w) for w in weights]
          + [pl.BlockSpec((t, LANES), lambda b, j: (j, 0))] * len(tables_nat)
          + [pl.BlockSpec((LANES, t), lambda b, j: (0, lat(j)))] * len(tables_t),
        out_specs=out_specs,
        out_shape=out_shapes,
        compiler_params=pltpu.CompilerParams(
            dimension_semantics=("arbitrary", "arbitrary"), vmem_limit_bytes=VMEM_LIMIT),
        name="proj",
    )(x, ctx, mod3, mod3, *weights, *tables_nat, *tables_t)


def _score_bound(qn2, kn2):
    return jnp.sqrt(qn2 * jnp.max(kn2, axis=-1, keepdims=True)) * BOUND_MARGIN


def _flags_kernel(kn2_ref, qn2_ref, o_ref):
    kn2 = kn2_ref[...]
    cols = []
    for i in range(qn2_ref.shape[2] // ATT_TILE):
        bound = _score_bound(qn2_ref[:, :, i * ATT_TILE:(i + 1) * ATT_TILE], kn2)
        worst = jnp.max(jnp.max(bound, axis=2), axis=1, keepdims=True)
        cols.append(jnp.where(worst <= FAST_LIMIT, 1, 0).astype(jnp.int32))
    o_ref[...] = jnp.concatenate(cols, axis=1)


def _flags(kn2, qn2):
    nb, _, seq = qn2.shape
    return pl.pallas_call(
        _flags_kernel,
        out_shape=jax.ShapeDtypeStruct((nb, seq // ATT_TILE), jnp.int32),
        name="flags",
    )(kn2, qn2)


def _attend_body(use_bound, depth, lam_ref, subln_ref, qa_ref, ka_ref, va_ref, qb_ref, kb_ref, vb_ref,
                 kn2_ref, qn2_ref, o_ref):
    t = ATT_TILE
    lp = lam_ref[...]
    lam = (jnp.exp(jnp.sum(lp[0:1] * lp[1:2], axis=-1, keepdims=True))
           - jnp.exp(jnp.sum(lp[2:3] * lp[3:4], axis=-1, keepdims=True)) + LAM_INIT)
    gain = subln_ref[...] * (1.0 - LAM_INIT)
    row = lax.broadcasted_iota(jnp.int32, (LANES, t), 0)
    lo = row < A_HEAD_DIM

    maps = []
    for hd in range(A_HEADS):
        q_t = qa_ref[0, hd]
        zero = jnp.zeros_like(q_t)
        for half in range(2):
            q_m = jnp.where(lo, q_t, zero) if half == 0 else jnp.where(lo, zero, q_t)
            maps.append((q_m, ka_ref.at[0, hd], va_ref.at[0, hd], LANES))
    for hd in range(B_HEADS):
        maps.append((qb_ref[0, hd], kb_ref.at[0, hd], vb_ref.at[0, hd], B_VDIM))

    def scores_t(i):
        q_m, k_ref, _, _ = maps[i]
        return _dot(k_ref[...], q_m)

    bounds = _score_bound(qn2_ref[0], kn2_ref[0]) if use_bound else None

    def finish(i, o_t, acc):
        if i < 2 * A_HEADS:
            hd, half = divmod(i, 2)
            if half == 0:
                acc[hd] = o_t
                return
            o = acc.pop(hd) - lam * o_t
            o = _rms(o, 0) * gain
            o_ref[0, :, hd * LANES:(hd + 1) * LANES] = o.T.astype(o_ref.dtype)
        else:
            hd = i - 2 * A_HEADS
            if hd % 2 == 0:
                acc[hd] = o_t
                return
            o = jnp.concatenate([acc.pop(hd - 1), o_t], axis=0)
            p = hd // 2
            o_ref[0, :, A_WIDTH + p * LANES:A_WIDTH + (p + 1) * LANES] = o.T.astype(o_ref.dtype)

    acc = {}

    def softmax_pv(i, s_t):
        _, _, v_ref, dv = maps[i]
        m = bounds[i:i + 1] if use_bound else jnp.max(s_t, axis=0, keepdims=True)
        p = jnp.exp2(s_t - m)
        if v_ref.shape[0] == dv:
            o = _dot(v_ref[...], p.astype(_BF16)) / jnp.sum(p, axis=0, keepdims=True)
        else:
            o = _dot(v_ref[...], p.astype(_BF16))
            o = o[:dv] / o[dv:dv + 1]
        finish(i, o, acc)

    pending = []
    for i in range(len(maps)):
        pending.append((i, scores_t(i)))
        if len(pending) > depth:
            softmax_pv(*pending.pop(0))
    for item in pending:
        softmax_pv(*item)


def _attend_kernel(flags_ref, *refs):
    fast_ok = flags_ref[pl.program_id(0), pl.program_id(1)] != 0

    @pl.when(fast_ok)
    def _():
        _attend_body(True, FAST_DEPTH, *refs)

    @pl.when(jnp.logical_not(fast_ok))
    def _():
        _attend_body(False, EXACT_DEPTH, *refs)


def _attend(lam_p, subln_col, qa, ka, va, qb, kb, vb, kn2, qn2):
    nb, _, _, seq = qa.shape
    t = ATT_TILE

    def q_spec(nh):
        return pl.BlockSpec((1, nh, LANES, t), lambda b, i, flags: (b, 0, 0, i))

    def kv_spec(a):
        return pl.BlockSpec((1,) + a.shape[1:], lambda b, i, flags: (b,) + (0,) * (a.ndim - 1))

    grid_spec = pltpu.PrefetchScalarGridSpec(
        num_scalar_prefetch=1,
        grid=(nb, seq // t),
        in_specs=[
            pl.BlockSpec(lam_p.shape, lambda b, i, flags: (0, 0)),
            pl.BlockSpec(subln_col.shape, lambda b, i, flags: (0, 0)),
            q_spec(A_HEADS), kv_spec(ka), kv_spec(va),
            q_spec(B_HEADS), kv_spec(kb), kv_spec(vb), kv_spec(kn2),
            pl.BlockSpec((1, N_MAPS, t), lambda b, i, flags: (b, 0, i)),
        ],
        out_specs=pl.BlockSpec((1, t, A_WIDTH + B_WIDTH), lambda b, i, flags: (b, i, 0)),
    )
    return pl.pallas_call(
        _attend_kernel,
        grid_spec=grid_spec,
        out_shape=jax.ShapeDtypeStruct((nb, seq, A_WIDTH + B_WIDTH), _BF16),
        compiler_params=pltpu.CompilerParams(
            dimension_semantics=("arbitrary", "arbitrary"), vmem_limit_bytes=VMEM_LIMIT),
        name="attend",
    )(_flags(kn2, qn2), lam_p, subln_col, qa, ka, va, qb, kb, vb, kn2, qn2)


def _post_kernel(cat_ref, x_ref, mod_ref, wo_ref, wgu_ref, wd_ref, fn_ref, o_ref):
    mod = mod_ref[0]
    g_a = mod[:, 2 * D_MODEL:3 * D_MODEL]
    sh_f = mod[:, 3 * D_MODEL:4 * D_MODEL]
    sc_f = mod[:, 4 * D_MODEL:5 * D_MODEL]
    g_f = mod[:, 5 * D_MODEL:6 * D_MODEL]
    rows = [slice(r, r + POST_SUB) for r in range(0, POST_TILE, POST_SUB)]
    lats = [x_ref[0, r, :] + g_a * _dot(cat_ref[0, r, :], wo_ref[...]) for r in rows]
    hs = [(_rms(lat) * (1.0 + sc_f) + sh_f).astype(_BF16) for lat in lats]
    gu = [(_dot(h, wgu_ref[:, :FFN_HIDDEN]), _dot(h, wgu_ref[:, FFN_HIDDEN:])) for h in hs]
    acts = [(g / (1.0 + jnp.exp(-g)) * u).astype(_BF16) for g, u in gu]
    ffns = [_dot(a, wd_ref[...]) for a in acts]
    for r, lat, ffn in zip(rows, lats, ffns):
        o_ref[0, r, :] = _rms(lat + g_f * ffn) * fn_ref[...]


def _post(cat, x, mod3, wo, wgu, wd, fn):
    nb, seq, _ = x.shape
    t = POST_TILE

    def full(a):
        return pl.BlockSpec(a.shape, lambda b, i: (0,) * a.ndim, pipeline_mode=pl.Buffered(1))

    tok = pl.BlockSpec((1, t, D_MODEL), lambda b, i: (b, i, 0))
    return pl.pallas_call(
        _post_kernel,
        grid=(nb, seq // t),
        in_specs=[tok, tok, pl.BlockSpec((1, 1, 6 * D_MODEL), lambda b, i: (b, 0, 0)),
                  full(wo), full(wgu), full(wd), full(fn)],
        out_specs=tok,
        out_shape=jax.ShapeDtypeStruct((nb, seq, D_MODEL), _F32),
        compiler_params=pltpu.CompilerParams(
            dimension_semantics=("arbitrary", "arbitrary"), vmem_limit_bytes=VMEM_LIMIT),
        name="post",
    )(cat, x, mod3, wo, wgu, wd, fn)


def _rope_tables(seq):
    pos = np.arange(seq)
    row = (pos // GRID_W).astype(np.float32)
    col = (pos % GRID_W).astype(np.float32)
    lane = np.arange(LANES)

    def build(d, active, half, q_scale):
        quarter = half // 2
        inv = (ROPE_THETA ** (-(np.arange(quarter, dtype=np.float32)) / quarter)).astype(np.float32)
        p = np.where((d < half)[None, :], row[:, None], col[:, None]).astype(np.float32)
        ang = (p * inv[d % quarter][None, :]).astype(np.float32)
        cos = np.where(active[None, :], np.cos(ang), 1.0).astype(np.float32)
        sin = np.where(active[None, :], np.sin(ang), 0.0).astype(np.float32)
        upper = ((d % half) >= quarter)[None, :]
        s_up = np.where(upper, sin, 0.0).astype(np.float32)
        s_dn = np.where(upper, 0.0, -sin).astype(np.float32)
        ident = np.zeros((CTX_LEN, LANES), np.float32)
        nat = [jnp.asarray(np.concatenate([ident + fill, tab], 0))
               for tab, fill in ((cos, 1.0), (s_up, 0.0), (s_dn, 0.0))]
        trans = [jnp.asarray(np.ascontiguousarray(cos.T * np.float32(q_scale))),
                 jnp.asarray(np.ascontiguousarray((s_up + s_dn).T * np.float32(q_scale)))]
        return nat, trans

    nat_a, trans_a = build(lane % A_HEAD_DIM, np.ones(LANES, bool), A_HEAD_DIM // 2, A_QSCALE)
    in_rope = (lane >= B_NOPE) & (lane < B_NOPE + B_ROPE)
    nat_b, trans_b = build(np.where(in_rope, lane - B_NOPE, 0), in_rope, B_ROPE // 2, B_QSCALE)
    return nat_a + nat_b, trans_a + trans_b


def kernel(x, c, ctx, c_ctx, w_ada, b_ada, w_in, q_a_norm, kv_a_norm, w_q_up, w_kv_up,
           diff_lambda, diff_subln, w_out, w_ffn_in, w_ffn_out, final_norm):
    nb, seq, _ = x.shape
    assert w_ada.shape[0] == 1 and seq % ATT_TILE == 0 and ctx.shape[1] == CTX_LEN == TOKEN_TILE

    pad = (-(nb + 1)) % SUBLANES
    c_all = jnp.concatenate([c, c_ctx[None, :], jnp.zeros((pad, D_MODEL), _F32)], axis=0)
    mod = _adaln(c_all, w_ada[0], b_ada[0][None, :])
    mod3 = mod[:, None, :]

    w = w_in[0]
    o_k, o_v, o_cq = A_WIDTH, 2 * A_WIDTH, 3 * A_WIDTH
    o_ckv, o_kr = o_cq + B_Q_RANK, o_cq + B_Q_RANK + B_KV_RANK
    zeros = lambda n: jnp.zeros((D_MODEL, n), w.dtype)
    w_c = w[:, o_cq:o_kr].T
    w_qv = jnp.concatenate([w[:, :o_k], w[:, o_v:o_cq]], axis=1).T
    w_n = jnp.concatenate([w[:, o_k:o_v], w[:, o_ckv:o_kr], zeros(B_NOPE), w[:, o_kr:],
                           zeros(LANES - B_NOPE - B_ROPE)], axis=1)
    wq = jnp.pad(w_q_up[0].reshape(B_Q_RANK, B_HEADS, B_NOPE + B_ROPE),
                 ((0, 0), (0, 0), (0, LANES - B_NOPE - B_ROPE))).reshape(B_Q_RANK, B_HEADS * LANES)
    wkv3 = w_kv_up[0].reshape(B_KV_RANK, B_HEADS, B_NOPE + B_VDIM)
    wk = jnp.pad(wkv3[:, :, :B_NOPE], ((0, 0), (0, 0), (0, LANES - B_NOPE))).reshape(B_KV_RANK, B_HEADS * LANES)
    wv = wkv3[:, :, B_NOPE:].reshape(B_KV_RANK, B_WIDTH)
    sel = np.zeros((N_MAPS, (A_HEADS + B_HEADS) * LANES), np.float32)
    for hd in range(A_HEADS):
        sel[2 * hd, hd * LANES:hd * LANES + A_HEAD_DIM] = 1.0
        sel[2 * hd + 1, hd * LANES + A_HEAD_DIM:(hd + 1) * LANES] = 1.0
    for hd in range(B_HEADS):
        sel[2 * A_HEADS + hd, (A_HEADS + hd) * LANES:(A_HEADS + hd + 1) * LANES] = 1.0
    weights = (w_c.astype(_BF16), w_n.astype(_BF16), w_qv.astype(_BF16),
               q_a_norm.reshape(B_Q_RANK, 1), kv_a_norm.reshape(B_KV_RANK, 1),
               kv_a_norm.reshape(1, B_KV_RANK),
               wq.T.astype(_BF16), wv.T.astype(_BF16), wk.astype(_BF16), jnp.asarray(sel, _BF16))

    tables_nat, tables_t = _rope_tables(seq)
    qa, ka, va, qb, kb, vb, kn2, qn2 = _proj(x, ctx, mod3, weights, tables_nat, tables_t)

    cat = _attend(diff_lambda[0], diff_subln.reshape(LANES, 1), qa, ka, va, qb, kb, vb, kn2, qn2)

    return _post(cat, x, mod3, w_out[0].astype(_BF16), w_ffn_in[0].astype(_BF16),
                 w_ffn_out[0].astype(_BF16), final_norm[None, :])
```

```python
import math

import numpy as np
import jax
import jax.numpy as jnp
from jax import lax
from jax.experimental import pallas as pl
from jax.experimental.pallas import tpu as pltpu

D_MODEL = 1024
CTX_LEN = 256
GRID_W = 64
ROPE_THETA = 10000.0
NORM_EPS = 1e-6

A_HEADS = 4
A_HEAD_DIM = 64
A_WIDTH = A_HEADS * 2 * A_HEAD_DIM
B_HEADS = 8
B_NOPE = 64
B_ROPE = 32
B_VDIM = 64
B_Q_RANK = 256
B_KV_RANK = 128
B_WIDTH = B_HEADS * B_VDIM
FFN_HIDDEN = 2816

LANES = 128
SUBLANES = 8
TOKEN_TILE = 256
ATT_TILE = 256
ADA_TILE = 2048
PROJ_ROWS = 4
PROJ_LAG = 1
POST_TILE = 1024
POST_SUB = 256
V_PAD = 16
N_MAPS = 2 * A_HEADS + B_HEADS
V7X_VMEM_BYTES = 64 * 1024 * 1024
VMEM_LIMIT = V7X_VMEM_BYTES - 8 * 1024 * 1024

LAM_INIT = 0.8 - 0.6 * math.exp(-0.3 * 0)
LOG2E = 1.4426950408889634
A_QSCALE = A_HEAD_DIM ** -0.5 * LOG2E
B_QSCALE = (B_NOPE + B_ROPE) ** -0.5 * LOG2E

FAST_LIMIT = 50.0
BOUND_MARGIN = 1.02
FAST_DEPTH = 1
EXACT_DEPTH = 2

_F32 = jnp.float32
_BF16 = jnp.bfloat16


def _dot(a, b):
    return jnp.dot(a, b, preferred_element_type=_F32)


def _dot_nt(a, b):
    return lax.dot_general(a, b, (((1,), (1,)), ((), ())), preferred_element_type=_F32)


def _rms(x, axis=-1):
    return x * lax.rsqrt(jnp.mean(x * x, axis=axis, keepdims=True) + NORM_EPS)


def _adaln_kernel(c_ref, w_ref, b_ref, o_ref):
    c = c_ref[...]
    s = c / (1.0 + jnp.exp(-c))
    o_ref[...] = _dot(s.astype(_BF16), w_ref[...].astype(_BF16)) + b_ref[...]


def _adaln(c_all, w_ada, b_ada):
    rows = c_all.shape[0]
    n = w_ada.shape[1]
    return pl.pallas_call(
        _adaln_kernel,
        grid=(n // ADA_TILE,),
        in_specs=[
            pl.BlockSpec((rows, D_MODEL), lambda i: (0, 0)),
            pl.BlockSpec((D_MODEL, ADA_TILE), lambda i: (0, i)),
            pl.BlockSpec((1, ADA_TILE), lambda i: (0, i)),
        ],
        out_specs=pl.BlockSpec((rows, ADA_TILE), lambda i: (0, i)),
        out_shape=jax.ShapeDtypeStruct((rows, n), _F32),
        name="adaln",
    )(c_all, w_ada, b_ada)


def _rope(x, cos, s_up, s_dn, quarter):
    return (x * cos + pltpu.roll(x, quarter, 1) * s_up
            + pltpu.roll(x, LANES - quarter, 1) * s_dn)


def _rope_t(x, cos_t, sin_t, lo, hi, quarter):
    parts = [x[:lo]] if lo else []
    for r in range(lo, hi, 2 * quarter):
        parts += [x[r + quarter:r + 2 * quarter], x[r:r + quarter]]
    if hi < x.shape[0]:
        parts.append(x[hi:])
    return x * cos_t + jnp.concatenate(parts, axis=0) * sin_t


def _proj_kernel(x_ref, ctx_ref, mod_ref, modc_ref, wc_ref, wn_ref, wqv_ref, qn_ref, kvn_col_ref,
                 kvn_row_ref, wqt_ref, wvt_ref, wk_ref, sel_ref,
                 ca_ref, ua_ref, da_ref, cb_ref, ub_ref, db_ref,
                 cat_ref, sat_ref, cbt_ref, sbt_ref,
                 qa_ref, ka_ref, va_ref, qb_ref, kb_ref, vb_ref, kn2_ref, qn2_ref):
    t = TOKEN_TILE
    chains = range(PROJ_ROWS)
    is_ctx = pl.program_id(1) == 0

    hs, hts = [], []
    for i in chains:
        xin = jnp.where(is_ctx, ctx_ref[i], x_ref[i])
        mod = jnp.where(is_ctx, modc_ref[0], mod_ref[i])
        shift = mod[:, 0:D_MODEL]
        scale = mod[:, D_MODEL:2 * D_MODEL]
        h32 = _rms(xin) * (1.0 + scale) + shift
        hs.append(h32.astype(_BF16))
        hts.append(hs[-1].T)

    ycs, ys, yqvs, qbs, vbs, kns = [], [], [], [], [], []

    def first_level(i):
        ycs.append(_dot(wc_ref[...], hts[i]))
        ys.append(_dot(hs[i], wn_ref[...]))
        yqvs.append(_dot(wqv_ref[...], hts[i]))

    def second_level(i):
        yc, y = ycs[i], ys[i]
        qbs.append(_dot(wqt_ref[...], (_rms(yc[:B_Q_RANK], 0) * qn_ref[...]).astype(_BF16)))
        vbs.append(_dot(wvt_ref[...], (_rms(yc[B_Q_RANK:], 0) * kvn_col_ref[...]).astype(_BF16)))
        kns.append(_dot((_rms(y[:, A_WIDTH:A_WIDTH + B_KV_RANK]) * kvn_row_ref[...]).astype(_BF16),
                        wk_ref[...]))

    for i in chains:
        first_level(i)
        if i >= PROJ_LAG:
            second_level(i - PROJ_LAG)
    for i in range(PROJ_ROWS - PROJ_LAG, PROJ_ROWS):
        second_level(i)

    row16 = lax.broadcasted_iota(jnp.int32, (V_PAD, t), 0)
    ones_rows = jnp.where(row16 == 0, 1.0, 0.0).astype(_BF16)
    lane = lax.broadcasted_iota(jnp.int32, (t, LANES), 1)
    cat, sat = cat_ref[...], sat_ref[...]
    cbt, sbt = cbt_ref[...], sbt_ref[...]
    ca, ua, da = ca_ref[...], ua_ref[...], da_ref[...]
    cb, ub, db = cb_ref[...], ub_ref[...], db_ref[...]

    def col_sq_norm(q):
        return jnp.sum(q * q, axis=0, keepdims=True)

    def squares(k):
        return (k * k).astype(_BF16)

    for i in chains:
        yqv_t, qb_t, vb_t, y, kn = yqvs[i], qbs[i], vbs[i], ys[i], kns[i]
        qn2_rows = []
        for hd in range(A_HEADS):
            q_t = yqv_t[hd * LANES:(hd + 1) * LANES]
            q = _rope_t(q_t, cat, sat, 0, LANES, A_HEAD_DIM // 4)
            qa_ref[i, hd] = q.astype(_BF16)
            qn2_rows += [col_sq_norm(q[:A_HEAD_DIM]), col_sq_norm(q[A_HEAD_DIM:])]
            va_ref[i, hd] = yqv_t[A_WIDTH + hd * LANES:A_WIDTH + (hd + 1) * LANES].astype(_BF16)
        for hd in range(B_HEADS):
            blk = qb_t[hd * LANES:(hd + 1) * LANES]
            q = _rope_t(blk, cbt, sbt, B_NOPE, B_NOPE + B_ROPE, B_ROPE // 4)
            qb_ref[i, hd] = q.astype(_BF16)
            qn2_rows.append(col_sq_norm(q))
            vb_ref[i, hd, :B_VDIM, :] = vb_t[hd * B_VDIM:(hd + 1) * B_VDIM].astype(_BF16)
            vb_ref[i, hd, B_VDIM:, :] = ones_rows
        qn2_ref[i] = jnp.concatenate(qn2_rows, axis=0)

        ksq = []
        for hd in range(A_HEADS):
            k = _rope(y[:, hd * LANES:(hd + 1) * LANES], ca, ua, da, A_HEAD_DIM // 4)
            ka_ref[i, hd] = k.astype(_BF16)
            ksq.append(squares(k))
        kr_rot = _rope(y[:, A_WIDTH + B_KV_RANK:], cb, ub, db, B_ROPE // 4)
        for hd in range(B_HEADS):
            k = jnp.where(lane < B_NOPE, kn[:, hd * LANES:(hd + 1) * LANES], kr_rot)
            kb_ref[i, hd] = k.astype(_BF16)
            ksq.append(squares(k))
        kn2_ref[i] = _dot_nt(sel_ref[...], jnp.concatenate(ksq, axis=1))


def _proj(x, ctx, mod3, weights, tables_nat, tables_t):
    nb, seq, _ = x.shape
    n_tok = CTX_LEN + seq
    n_tiles = n_tok // TOKEN_TILE
    t = TOKEN_TILE

    def full(a):
        return pl.BlockSpec(a.shape, lambda b, j: (0,) * a.ndim)

    lat = lambda j: jnp.maximum(j - 1, 0)
    out_shapes = (
        jax.ShapeDtypeStruct((nb, A_HEADS, LANES, seq), _BF16),
        jax.ShapeDtypeStruct((nb, A_HEADS, n_tok, LANES), _BF16),
        jax.ShapeDtypeStruct((nb, A_HEADS, LANES, n_tok), _BF16),
        jax.ShapeDtypeStruct((nb, B_HEADS, LANES, seq), _BF16),
        jax.ShapeDtypeStruct((nb, B_HEADS, n_tok, LANES), _BF16),
        jax.ShapeDtypeStruct((nb, B_HEADS, B_VDIM + V_PAD, n_tok), _BF16),
        jax.ShapeDtypeStruct((nb, N_MAPS, n_tok), _F32),
        jax.ShapeDtypeStruct((nb, N_MAPS, seq), _F32),
    )
    r = PROJ_ROWS
    out_specs = (
        pl.BlockSpec((r, A_HEADS, LANES, t), lambda b, j: (b, 0, 0, lat(j))),
        pl.BlockSpec((r, A_HEADS, t, LANES), lambda b, j: (b, 0, j, 0)),
        pl.BlockSpec((r, A_HEADS, LANES, t), lambda b, j: (b, 0, 0, j)),
        pl.BlockSpec((r, B_HEADS, LANES, t), lambda b, j: (b, 0, 0, lat(j))),
        pl.BlockSpec((r, B_HEADS, t, LANES), lambda b, j: (b, 0, j, 0)),
        pl.BlockSpec((r, B_HEADS, B_VDIM + V_PAD, t), lambda b, j: (b, 0, 0, j)),
        pl.BlockSpec((r, N_MAPS, t), lambda b, j: (b, 0, j)),
        pl.BlockSpec((r, N_MAPS, t), lambda b, j: (b, 0, lat(j))),
    )
    return pl.pallas_call(
        _proj_kernel,
        grid=(nb // r, n_tiles),
        in_specs=[
            pl.BlockSpec((r, t, D_MODEL), lambda b, j: (b, lat(j), 0)),
            pl.BlockSpec((r, t, D_MODEL), lambda b, j: (b, 0, 0)),
            pl.BlockSpec((r, 1, 6 * D_MODEL), lambda b, j: (b, 0, 0)),
            pl.BlockSpec((1, 1, 6 * D_MODEL), lambda b, j: (nb, 0, 0)),
        ] + [full(w) for w in weights]
          + [pl.BlockSpec((t, LANES), lambda b, j: (j, 0))] * len(tables_nat)
          + [pl.BlockSpec((LANES, t), lambda b, j: (0, lat(j)))] * len(tables_t),
        out_specs=out_specs,
        out_shape=out_shapes,
        compiler_params=pltpu.CompilerParams(
            dimension_semantics=("arbitrary", "arbitrary"), vmem_limit_bytes=VMEM_LIMIT),
        name="proj",
    )(x, ctx, mod3, mod3, *weights, *tables_nat, *tables_t)


def _score_bound(qn2, kn2):
    return jnp.sqrt(qn2 * jnp.max(kn2, axis=-1, keepdims=True)) * BOUND_MARGIN


def _flags_kernel(kn2_ref, qn2_ref, o_ref):
    kn2 = kn2_ref[...]
    cols = []
    for i in range(qn2_ref.shape[2] // ATT_TILE):
        bound = _score_bound(qn2_ref[:, :, i * ATT_TILE:(i + 1) * ATT_TILE], kn2)
        worst = jnp.max(jnp.max(bound, axis=2), axis=1, keepdims=True)
        cols.append(jnp.where(worst <= FAST_LIMIT, 1, 0).astype(jnp.int32))
    o_ref[...] = jnp.concatenate(cols, axis=1)


def _flags(kn2, qn2):
    nb, _, seq = qn2.shape
    return pl.pallas_call(
        _flags_kernel,
        out_shape=jax.ShapeDtypeStruct((nb, seq // ATT_TILE), jnp.int32),
        name="flags",
    )(kn2, qn2)


def _attend_body(use_bound, depth, lam_ref, subln_ref, qa_ref, ka_ref, va_ref, qb_ref, kb_ref, vb_ref,
                 kn2_ref, qn2_ref, o_ref):
    t = ATT_TILE
    lp = lam_ref[...]
    lam = (jnp.exp(jnp.sum(lp[0:1] * lp[1:2], axis=-1, keepdims=True))
           - jnp.exp(jnp.sum(lp[2:3] * lp[3:4], axis=-1, keepdims=True)) + LAM_INIT)
    gain = subln_ref[...] * (1.0 - LAM_INIT)
    row = lax.broadcasted_iota(jnp.int32, (LANES, t), 0)
    lo = row < A_HEAD_DIM

    maps = []
    for hd in range(A_HEADS):
        q_t = qa_ref[0, hd]
        zero = jnp.zeros_like(q_t)
        for half in range(2):
            q_m = jnp.where(lo, q_t, zero) if half == 0 else jnp.where(lo, zero, q_t)
            maps.append((q_m, ka_ref.at[0, hd], va_ref.at[0, hd], LANES))
    for hd in range(B_HEADS):
        maps.append((qb_ref[0, hd], kb_ref.at[0, hd], vb_ref.at[0, hd], B_VDIM))

    def scores_t(i):
        q_m, k_ref, _, _ = maps[i]
        return _dot(k_ref[...], q_m)

    bounds = _score_bound(qn2_ref[0], kn2_ref[0]) if use_bound else None

    def finish(i, o_t, acc):
        if i < 2 * A_HEADS:
            hd, half = divmod(i, 2)
            if half == 0:
                acc[hd] = o_t
                return
            o = acc.pop(hd) - lam * o_t
            o = _rms(o, 0) * gain
            o_ref[0, :, hd * LANES:(hd + 1) * LANES] = o.T.astype(o_ref.dtype)
        else:
            hd = i - 2 * A_HEADS
            if hd % 2 == 0:
                acc[hd] = o_t
                return
            o = jnp.concatenate([acc.pop(hd - 1), o_t], axis=0)
            p = hd // 2
            o_ref[0, :, A_WIDTH + p * LANES:A_WIDTH + (p + 1) * LANES] = o.T.astype(o_ref.dtype)

    acc = {}

    def softmax_pv(i, s_t):
        _, _, v_ref, dv = maps[i]
        m = bounds[i:i + 1] if use_bound else jnp.max(s_t, axis=0, keepdims=True)
        p = jnp.exp2(s_t - m)
        if v_ref.shape[0] == dv:
            o = _dot(v_ref[...], p.astype(_BF16)) / jnp.sum(p, axis=0, keepdims=True)
        else:
            o = _dot(v_ref[...], p.astype(_BF16))
            o = o[:dv] / o[dv:dv + 1]
        finish(i, o, acc)

    pending = []
    for i in range(len(maps)):
        pending.append((i, scores_t(i)))
        if len(pending) > depth:
            softmax_pv(*pending.pop(0))
    for item in pending:
        softmax_pv(*item)


def _attend_kernel(flags_ref, *refs):
    fast_ok = flags_ref[pl.program_id(0), pl.program_id(1)] != 0

    @pl.when(fast_ok)
    def _():
        _attend_body(True, FAST_DEPTH, *refs)

    @pl.when(jnp.logical_not(fast_ok))
    def _():
        _attend_body(False, EXACT_DEPTH, *refs)


def _attend(lam_p, subln_col, qa, ka, va, qb, kb, vb, kn2, qn2):
    nb, _, _, seq = qa.shape
    t = ATT_TILE

    def q_spec(nh):
        return pl.BlockSpec((1, nh, LANES, t), lambda b, i, flags: (b, 0, 0, i))

    def kv_spec(a):
        return pl.BlockSpec((1,) + a.shape[1:], lambda b, i, flags: (b,) + (0,) * (a.ndim - 1))

    grid_spec = pltpu.PrefetchScalarGridSpec(
        num_scalar_prefetch=1,
        grid=(nb, seq // t),
        in_specs=[
            pl.BlockSpec(lam_p.shape, lambda b, i, flags: (0, 0)),
            pl.BlockSpec(subln_col.shape, lambda b, i, flags: (0, 0)),
            q_spec(A_HEADS), kv_spec(ka), kv_spec(va),
            q_spec(B_HEADS), kv_spec(kb), kv_spec(vb), kv_spec(kn2),
            pl.BlockSpec((1, N_MAPS, t), lambda b, i, flags: (b, 0, i)),
        ],
        out_specs=pl.BlockSpec((1, t, A_WIDTH + B_WIDTH), lambda b, i, flags: (b, i, 0)),
    )
    return pl.pallas_call(
        _attend_kernel,
        grid_spec=grid_spec,
        out_shape=jax.ShapeDtypeStruct((nb, seq, A_WIDTH + B_WIDTH), _BF16),
        compiler_params=pltpu.CompilerParams(
            dimension_semantics=("arbitrary", "arbitrary"), vmem_limit_bytes=VMEM_LIMIT),
        name="attend",
    )(_flags(kn2, qn2), lam_p, subln_col, qa, ka, va, qb, kb, vb, kn2, qn2)


def _post_kernel(cat_ref, x_ref, mod_ref, wo_ref, wgu_ref, wd_ref, fn_ref, o_ref):
    mod = mod_ref[0]
    g_a = mod[:, 2 * D_MODEL:3 * D_MODEL]
    sh_f = mod[:, 3 * D_MODEL:4 * D_MODEL]
    sc_f = mod[:, 4 * D_MODEL:5 * D_MODEL]
    g_f = mod[:, 5 * D_MODEL:6 * D_MODEL]
    rows = [slice(r, r + POST_SUB) for r in range(0, POST_TILE, POST_SUB)]
    n = len(rows)
    lats, acts = {}, {}

    def attn_stage(i):
        lats[i] = x_ref[0, rows[i], :] + g_a * _dot(cat_ref[0, rows[i], :], wo_ref[...])

    def gate_up_stage(i):
        h = (_rms(lats[i]) * (1.0 + sc_f) + sh_f).astype(_BF16)
        g, u = _dot(h, wgu_ref[:, :FFN_HIDDEN]), _dot(h, wgu_ref[:, FFN_HIDDEN:])
        acts[i] = (g / (1.0 + jnp.exp(-g)) * u).astype(_BF16)

    def down_stage(i):
        lat = lats.pop(i) + g_f * _dot(acts.pop(i), wd_ref[...])
        o_ref[0, rows[i], :] = _rms(lat) * fn_ref[...]

    for i in range(n + 2):
        if i < n:
            attn_stage(i)
        if 0 <= i - 1 < n:
            gate_up_stage(i - 1)
        if 0 <= i - 2 < n:
            down_stage(i - 2)


def _post(cat, x, mod3, wo, wgu, wd, fn):
    nb, seq, _ = x.shape
    t = POST_TILE

    def full(a):
        return pl.BlockSpec(a.shape, lambda b, i: (0,) * a.ndim, pipeline_mode=pl.Buffered(1))

    tok = pl.BlockSpec((1, t, D_MODEL), lambda b, i: (b, i, 0))
    return pl.pallas_call(
        _post_kernel,
        grid=(nb, seq // t),
        in_specs=[tok, tok, pl.BlockSpec((1, 1, 6 * D_MODEL), lambda b, i: (b, 0, 0)),
                  full(wo), full(wgu), full(wd), full(fn)],
        out_specs=tok,
        out_shape=jax.ShapeDtypeStruct((nb, seq, D_MODEL), _F32),
        compiler_params=pltpu.CompilerParams(
            dimension_semantics=("arbitrary", "arbitrary"), vmem_limit_bytes=VMEM_LIMIT),
        name="post",
    )(cat, x, mod3, wo, wgu, wd, fn)


def _rope_tables(seq):
    pos = np.arange(seq)
    row = (pos // GRID_W).astype(np.float32)
    col = (pos % GRID_W).astype(np.float32)
    lane = np.arange(LANES)

    def build(d, active, half, q_scale):
        quarter = half // 2
        inv = (ROPE_THETA ** (-(np.arange(quarter, dtype=np.float32)) / quarter)).astype(np.float32)
        p = np.where((d < half)[None, :], row[:, None], col[:, None]).astype(np.float32)
        ang = (p * inv[d % quarter][None, :]).astype(np.float32)
        cos = np.where(active[None, :], np.cos(ang), 1.0).astype(np.float32)
        sin = np.where(active[None, :], np.sin(ang), 0.0).astype(np.float32)
        upper = ((d % half) >= quarter)[None, :]
        s_up = np.where(upper, sin, 0.0).astype(np.float32)
        s_dn = np.where(upper, 0.0, -sin).astype(np.float32)
        ident = np.zeros((CTX_LEN, LANES), np.float32)
        nat = [jnp.asarray(np.concatenate([ident + fill, tab], 0))
               for tab, fill in ((cos, 1.0), (s_up, 0.0), (s_dn, 0.0))]
        trans = [jnp.asarray(np.ascontiguousarray(cos.T * np.float32(q_scale))),
                 jnp.asarray(np.ascontiguousarray((s_up + s_dn).T * np.float32(q_scale)))]
        return nat, trans

    nat_a, trans_a = build(lane % A_HEAD_DIM, np.ones(LANES, bool), A_HEAD_DIM // 2, A_QSCALE)
    in_rope = (lane >= B_NOPE) & (lane < B_NOPE + B_ROPE)
    nat_b, trans_b = build(np.where(in_rope, lane - B_NOPE, 0), in_rope, B_ROPE // 2, B_QSCALE)
    return nat_a + nat_b, trans_a + trans_b


def kernel(x, c, ctx, c_ctx, w_ada, b_ada, w_in, q_a_norm, kv_a_norm, w_q_up, w_kv_up,
           diff_lambda, diff_subln, w_out, w_ffn_in, w_ffn_out, final_norm):
    nb, seq, _ = x.shape
    assert w_ada.shape[0] == 1 and seq % ATT_TILE == 0 and ctx.shape[1] == CTX_LEN == TOKEN_TILE

    pad = (-(nb + 1)) % SUBLANES
    c_all = jnp.concatenate([c, c_ctx[None, :], jnp.zeros((pad, D_MODEL), _F32)], axis=0)
    mod = _adaln(c_all, w_ada[0], b_ada[0][None, :])
    mod3 = mod[:, None, :]

    w = w_in[0]
    o_k, o_v, o_cq = A_WIDTH, 2 * A_WIDTH, 3 * A_WIDTH
    o_ckv, o_kr = o_cq + B_Q_RANK, o_cq + B_Q_RANK + B_KV_RANK
    zeros = lambda n: jnp.zeros((D_MODEL, n), w.dtype)
    w_c = w[:, o_cq:o_kr].T
    w_qv = jnp.concatenate([w[:, :o_k], w[:, o_v:o_cq]], axis=1).T
    w_n = jnp.concatenate([w[:, o_k:o_v], w[:, o_ckv:o_kr], zeros(B_NOPE), w[:, o_kr:],
                           zeros(LANES - B_NOPE - B_ROPE)], axis=1)
    wq = jnp.pad(w_q_up[0].reshape(B_Q_RANK, B_HEADS, B_NOPE + B_ROPE),
                 ((0, 0), (0, 0), (0, LANES - B_NOPE - B_ROPE))).reshape(B_Q_RANK, B_HEADS * LANES)
    wkv3 = w_kv_up[0].reshape(B_KV_RANK, B_HEADS, B_NOPE + B_VDIM)
    wk = jnp.pad(wkv3[:, :, :B_NOPE], ((0, 0), (0, 0), (0, LANES - B_NOPE))).reshape(B_KV_RANK, B_HEADS * LANES)
    wv = wkv3[:, :, B_NOPE:].reshape(B_KV_RANK, B_WIDTH)
    sel = np.zeros((N_MAPS, (A_HEADS + B_HEADS) * LANES), np.float32)
    for hd in range(A_HEADS):
        sel[2 * hd, hd * LANES:hd * LANES + A_HEAD_DIM] = 1.0
        sel[2 * hd + 1, hd * LANES + A_HEAD_DIM:(hd + 1) * LANES] = 1.0
    for hd in range(B_HEADS):
        sel[2 * A_HEADS + hd, (A_HEADS + hd) * LANES:(A_HEADS + hd + 1) * LANES] = 1.0
    weights = (w_c.astype(_BF16), w_n.astype(_BF16), w_qv.astype(_BF16),
               q_a_norm.reshape(B_Q_RANK, 1), kv_a_norm.reshape(B_KV_RANK, 1),
               kv_a_norm.reshape(1, B_KV_RANK),
               wq.T.astype(_BF16), wv.T.astype(_BF16), wk.astype(_BF16), jnp.asarray(sel, _BF16))

    tables_nat, tables_t = _rope_tables(seq)
    qa, ka, va, qb, kb, vb, kn2, qn2 = _proj(x, ctx, mod3, weights, tables_nat, tables_t)

    cat = _attend(diff_lambda[0], diff_subln.reshape(LANES, 1), qa, ka, va, qb, kb, vb, kn2, qn2)

    return _post(cat, x, mod3, w_out[0].astype(_BF16), w_ffn_in[0].astype(_BF16),
                 w_ffn_out[0].astype(_BF16), final_norm[None, :])
```

```python
import math

import numpy as np
import jax
import jax.numpy as jnp
from jax import lax
from jax.experimental import pallas as pl
from jax.experimental.pallas import tpu as pltpu

D_MODEL = 1024
CTX_LEN = 256
GRID_W = 64
ROPE_THETA = 10000.0
NORM_EPS = 1e-6

A_HEADS = 4
A_HEAD_DIM = 64
A_WIDTH = A_HEADS * 2 * A_HEAD_DIM
B_HEADS = 8
B_NOPE = 64
B_ROPE = 32
B_VDIM = 64
B_Q_RANK = 256
B_KV_RANK = 128
B_WIDTH = B_HEADS * B_VDIM
FFN_HIDDEN = 2816

LANES = 128
SUBLANES = 8
TOKEN_TILE = 256
ATT_TILE = 256
ADA_TILE = 2048
PROJ_ROWS = 4
PROJ_LAG = 1
POST_TILE = 1024
POST_SUB = 256
V_PAD = 16
N_MAPS = 2 * A_HEADS + B_HEADS
V7X_VMEM_BYTES = 64 * 1024 * 1024
VMEM_LIMIT = V7X_VMEM_BYTES - 8 * 1024 * 1024

LAM_INIT = 0.8 - 0.6 * math.exp(-0.3 * 0)
LOG2E = 1.4426950408889634
A_QSCALE = A_HEAD_DIM ** -0.5 * LOG2E
B_QSCALE = (B_NOPE + B_ROPE) ** -0.5 * LOG2E

FAST_LIMIT = 50.0
BOUND_MARGIN = 1.02
FAST_DEPTH = 2
EXACT_DEPTH = 2

_F32 = jnp.float32
_BF16 = jnp.bfloat16


def _dot(a, b):
    return jnp.dot(a, b, preferred_element_type=_F32)


def _dot_nt(a, b):
    return lax.dot_general(a, b, (((1,), (1,)), ((), ())), preferred_element_type=_F32)


def _rms(x, axis=-1):
    return x * lax.rsqrt(jnp.mean(x * x, axis=axis, keepdims=True) + NORM_EPS)


def _adaln_kernel(c_ref, w_ref, b_ref, o_ref):
    c = c_ref[...]
    s = c / (1.0 + jnp.exp(-c))
    o_ref[...] = _dot(s.astype(_BF16), w_ref[...].astype(_BF16)) + b_ref[...]


def _adaln(c_all, w_ada, b_ada):
    rows = c_all.shape[0]
    n = w_ada.shape[1]
    return pl.pallas_call(
        _adaln_kernel,
        grid=(n // ADA_TILE,),
        in_specs=[
            pl.BlockSpec((rows, D_MODEL), lambda i: (0, 0)),
            pl.BlockSpec((D_MODEL, ADA_TILE), lambda i: (0, i)),
            pl.BlockSpec((1, ADA_TILE), lambda i: (0, i)),
        ],
        out_specs=pl.BlockSpec((rows, ADA_TILE), lambda i: (0, i)),
        out_shape=jax.ShapeDtypeStruct((rows, n), _F32),
        name="adaln",
    )(c_all, w_ada, b_ada)


def _rope(x, cos, s_up, s_dn, quarter):
    return (x * cos + pltpu.roll(x, quarter, 1) * s_up
            + pltpu.roll(x, LANES - quarter, 1) * s_dn)


def _rope_t(x, cos_t, sin_t, lo, hi, quarter):
    parts = [x[:lo]] if lo else []
    for r in range(lo, hi, 2 * quarter):
        parts += [x[r + quarter:r + 2 * quarter], x[r:r + quarter]]
    if hi < x.shape[0]:
        parts.append(x[hi:])
    return x * cos_t + jnp.concatenate(parts, axis=0) * sin_t


def _proj_kernel(x_ref, ctx_ref, mod_ref, modc_ref, wc_ref, wn_ref, wqv_ref, qn_ref, kvn_col_ref,
                 kvn_row_ref, wqt_ref, wvt_ref, wk_ref, sel_ref,
                 ca_ref, ua_ref, da_ref, cb_ref, ub_ref, db_ref,
                 cat_ref, sat_ref, cbt_ref, sbt_ref,
                 qa_ref, ka_ref, va_ref, qb_ref, kb_ref, vb_ref, kn2_ref, qn2_ref):
    t = TOKEN_TILE
    chains = range(PROJ_ROWS)
    is_ctx = pl.program_id(1) == 0

    hs, hts = [], []
    for i in chains:
        xin = jnp.where(is_ctx, ctx_ref[i], x_ref[i])
        mod = jnp.where(is_ctx, modc_ref[0], mod_ref[i])
        shift = mod[:, 0:D_MODEL]
        scale = mod[:, D_MODEL:2 * D_MODEL]
        h32 = _rms(xin) * (1.0 + scale) + shift
        hs.append(h32.astype(_BF16))
        hts.append(hs[-1].T)

    ycs, ys, yqvs, qbs, vbs, kns = [], [], [], [], [], []

    def first_level(i):
        ycs.append(_dot(wc_ref[...], hts[i]))
        ys.append(_dot(hs[i], wn_ref[...]))
        yqvs.append(_dot(wqv_ref[...], hts[i]))

    def second_level(i):
        yc, y = ycs[i], ys[i]
        qbs.append(_dot(wqt_ref[...], (_rms(yc[:B_Q_RANK], 0) * qn_ref[...]).astype(_BF16)))
        vbs.append(_dot(wvt_ref[...], (_rms(yc[B_Q_RANK:], 0) * kvn_col_ref[...]).astype(_BF16)))
        kns.append(_dot((_rms(y[:, A_WIDTH:A_WIDTH + B_KV_RANK]) * kvn_row_ref[...]).astype(_BF16),
                        wk_ref[...]))

    for i in chains:
        first_level(i)
        if i >= PROJ_LAG:
            second_level(i - PROJ_LAG)
    for i in range(PROJ_ROWS - PROJ_LAG, PROJ_ROWS):
        second_level(i)

    row16 = lax.broadcasted_iota(jnp.int32, (V_PAD, t), 0)
    ones_rows = jnp.where(row16 == 0, 1.0, 0.0).astype(_BF16)
    lane = lax.broadcasted_iota(jnp.int32, (t, LANES), 1)
    cat, sat = cat_ref[...], sat_ref[...]
    cbt, sbt = cbt_ref[...], sbt_ref[...]
    ca, ua, da = ca_ref[...], ua_ref[...], da_ref[...]
    cb, ub, db = cb_ref[...], ub_ref[...], db_ref[...]

    def col_sq_norm(q):
        return jnp.sum(q * q, axis=0, keepdims=True)

    def squares(k):
        return (k * k).astype(_BF16)

    for i in chains:
        yqv_t, qb_t, vb_t, y, kn = yqvs[i], qbs[i], vbs[i], ys[i], kns[i]
        qn2_rows = []
        for hd in range(A_HEADS):
            q_t = yqv_t[hd * LANES:(hd + 1) * LANES]
            q = _rope_t(q_t, cat, sat, 0, LANES, A_HEAD_DIM // 4)
            qa_ref[i, hd] = q.astype(_BF16)
            qn2_rows += [col_sq_norm(q[:A_HEAD_DIM]), col_sq_norm(q[A_HEAD_DIM:])]
            va_ref[i, hd] = yqv_t[A_WIDTH + hd * LANES:A_WIDTH + (hd + 1) * LANES].astype(_BF16)
        for hd in range(B_HEADS):
            blk = qb_t[hd * LANES:(hd + 1) * LANES]
            q = _rope_t(blk, cbt, sbt, B_NOPE, B_NOPE + B_ROPE, B_ROPE // 4)
            qb_ref[i, hd] = q.astype(_BF16)
            qn2_rows.append(col_sq_norm(q))
            vb_ref[i, hd, :B_VDIM, :] = vb_t[hd * B_VDIM:(hd + 1) * B_VDIM].astype(_BF16)
            vb_ref[i, hd, B_VDIM:, :] = ones_rows
        qn2_ref[i] = jnp.concatenate(qn2_rows, axis=0)

        ksq = []
        for hd in range(A_HEADS):
            k = _rope(y[:, hd * LANES:(hd + 1) * LANES], ca, ua, da, A_HEAD_DIM // 4)
            ka_ref[i, hd] = k.astype(_BF16)
            ksq.append(squares(k))
        kr_rot = _rope(y[:, A_WIDTH + B_KV_RANK:], cb, ub, db, B_ROPE // 4)
        for hd in range(B_HEADS):
            k = jnp.where(lane < B_NOPE, kn[:, hd * LANES:(hd + 1) * LANES], kr_rot)
            kb_ref[i, hd] = k.astype(_BF16)
            ksq.append(squares(k))
        kn2_ref[i] = _dot_nt(sel_ref[...], jnp.concatenate(ksq, axis=1))


def _proj(x, ctx, mod3, weights, tables_nat, tables_t):
    nb, seq, _ = x.shape
    n_tok = CTX_LEN + seq
    n_tiles = n_tok // TOKEN_TILE
    t = TOKEN_TILE

    def full(a):
        return pl.BlockSpec(a.shape, lambda b, j: (0,) * a.ndim)

    lat = lambda j: jnp.maximum(j - 1, 0)
    out_shapes = (
        jax.ShapeDtypeStruct((nb, A_HEADS, LANES, seq), _BF16),
        jax.ShapeDtypeStruct((nb, A_HEADS, n_tok, LANES), _BF16),
        jax.ShapeDtypeStruct((nb, A_HEADS, LANES, n_tok), _BF16),
        jax.ShapeDtypeStruct((nb, B_HEADS, LANES, seq), _BF16),
        jax.ShapeDtypeStruct((nb, B_HEADS, n_tok, LANES), _BF16),
        jax.ShapeDtypeStruct((nb, B_HEADS, B_VDIM + V_PAD, n_tok), _BF16),
        jax.ShapeDtypeStruct((nb, N_MAPS, n_tok), _F32),
        jax.ShapeDtypeStruct((nb, N_MAPS, seq), _F32),
    )
    r = PROJ_ROWS
    out_specs = (
        pl.BlockSpec((r, A_HEADS, LANES, t), lambda b, j: (b, 0, 0, lat(j))),
        pl.BlockSpec((r, A_HEADS, t, LANES), lambda b, j: (b, 0, j, 0)),
        pl.BlockSpec((r, A_HEADS, LANES, t), lambda b, j: (b, 0, 0, j)),
        pl.BlockSpec((r, B_HEADS, LANES, t), lambda b, j: (b, 0, 0, lat(j))),
        pl.BlockSpec((r, B_HEADS, t, LANES), lambda b, j: (b, 0, j, 0)),
        pl.BlockSpec((r, B_HEADS, B_VDIM + V_PAD, t), lambda b, j: (b, 0, 0, j)),
        pl.BlockSpec((r, N_MAPS, t), lambda b, j: (b, 0, j)),
        pl.BlockSpec((r, N_MAPS, t), lambda b, j: (b, 0, lat(j))),
    )
    return pl.pallas_call(
        _proj_kernel,
        grid=(nb // r, n_tiles),
        in_specs=[
            pl.BlockSpec((r, t, D_MODEL), lambda b, j: (b, lat(j), 0)),
            pl.BlockSpec((r, t, D_MODEL), lambda b, j: (b, 0, 0)),
            pl.BlockSpec((r, 1, 6 * D_MODEL), lambda b, j: (b, 0, 0)),
            pl.BlockSpec((1, 1, 6 * D_MODEL), lambda b, j: (nb, 0, 0)),
        ] + [full(w) for w in weights]
          + [pl.BlockSpec((t, LANES), lambda b, j: (j, 0))] * len(tables_nat)
          + [pl.BlockSpec((LANES, t), lambda b, j: (0, lat(j)))] * len(tables_t),
        out_specs=out_specs,
        out_shape=out_shapes,
        compiler_params=pltpu.CompilerParams(
            dimension_semantics=("arbitrary", "arbitrary"), vmem_limit_bytes=VMEM_LIMIT),
        name="proj",
    )(x, ctx, mod3, mod3, *weights, *tables_nat, *tables_t)


def _score_bound(qn2, kn2):
    return jnp.sqrt(qn2 * jnp.max(kn2, axis=-1, keepdims=True)) * BOUND_MARGIN


def _flags_kernel(kn2_ref, qn2_ref, o_ref):
    kn2 = kn2_ref[...]
    cols = []
    for i in range(qn2_ref.shape[2] // ATT_TILE):
        bound = _score_bound(qn2_ref[:, :, i * ATT_TILE:(i + 1) * ATT_TILE], kn2)
        worst = jnp.max(jnp.max(bound, axis=2), axis=1, keepdims=True)
        cols.append(jnp.where(worst <= FAST_LIMIT, 1, 0).astype(jnp.int32))
    o_ref[...] = jnp.concatenate(cols, axis=1)


def _flags(kn2, qn2):
    nb, _, seq = qn2.shape
    return pl.pallas_call(
        _flags_kernel,
        out_shape=jax.ShapeDtypeStruct((nb, seq // ATT_TILE), jnp.int32),
        name="flags",
    )(kn2, qn2)


def _attend_body(use_bound, depth, lam_ref, subln_ref, qa_ref, ka_ref, va_ref, qb_ref, kb_ref, vb_ref,
                 kn2_ref, qn2_ref, o_ref):
    t = ATT_TILE
    lp = lam_ref[...]
    lam = (jnp.exp(jnp.sum(lp[0:1] * lp[1:2], axis=-1, keepdims=True))
           - jnp.exp(jnp.sum(lp[2:3] * lp[3:4], axis=-1, keepdims=True)) + LAM_INIT)
    gain = subln_ref[...] * (1.0 - LAM_INIT)
    row = lax.broadcasted_iota(jnp.int32, (LANES, t), 0)
    lo = row < A_HEAD_DIM

    maps = []
    for hd in range(A_HEADS):
        q_t = qa_ref[0, hd]
        zero = jnp.zeros_like(q_t)
        for half in range(2):
            q_m = jnp.where(lo, q_t, zero) if half == 0 else jnp.where(lo, zero, q_t)
            maps.append((q_m, ka_ref.at[0, hd], va_ref.at[0, hd], LANES))
    for hd in range(B_HEADS):
        maps.append((qb_ref[0, hd], kb_ref.at[0, hd], vb_ref.at[0, hd], B_VDIM))

    def scores_t(i):
        q_m, k_ref, _, _ = maps[i]
        return _dot(k_ref[...], q_m)

    bounds = _score_bound(qn2_ref[0], kn2_ref[0]) if use_bound else None

    def finish(i, o_t, acc):
        if i < 2 * A_HEADS:
            hd, half = divmod(i, 2)
            if half == 0:
                acc[hd] = o_t
                return
            o = acc.pop(hd) - lam * o_t
            o = _rms(o, 0) * gain
            o_ref[0, :, hd * LANES:(hd + 1) * LANES] = o.T.astype(o_ref.dtype)
        else:
            hd = i - 2 * A_HEADS
            if hd % 2 == 0:
                acc[hd] = o_t
                return
            o = jnp.concatenate([acc.pop(hd - 1), o_t], axis=0)
            p = hd // 2
            o_ref[0, :, A_WIDTH + p * LANES:A_WIDTH + (p + 1) * LANES] = o.T.astype(o_ref.dtype)

    acc = {}

    def softmax_pv(i, s_t):
        _, _, v_ref, dv = maps[i]
        m = bounds[i:i + 1] if use_bound else jnp.max(s_t, axis=0, keepdims=True)
        p = jnp.exp2(s_t - m)
        if v_ref.shape[0] == dv:
            o = _dot(v_ref[...], p.astype(_BF16)) / jnp.sum(p, axis=0, keepdims=True)
        else:
            o = _dot(v_ref[...], p.astype(_BF16))
            o = o[:dv] / o[dv:dv + 1]
        finish(i, o, acc)

    pending = []
    for i in range(len(maps)):
        pending.append((i, scores_t(i)))
        if len(pending) > depth:
            softmax_pv(*pending.pop(0))
    for item in pending:
        softmax_pv(*item)


def _attend_kernel(flags_ref, *refs):
    fast_ok = flags_ref[pl.program_id(0), pl.program_id(1)] != 0

    @pl.when(fast_ok)
    def _():
        _attend_body(True, FAST_DEPTH, *refs)

    @pl.when(jnp.logical_not(fast_ok))
    def _():
        _attend_body(False, EXACT_DEPTH, *refs)


def _attend(lam_p, subln_col, qa, ka, va, qb, kb, vb, kn2, qn2):
    nb, _, _, seq = qa.shape
    t = ATT_TILE

    def q_spec(nh):
        return pl.BlockSpec((1, nh, LANES, t), lambda b, i, flags: (b, 0, 0, i))

    def kv_spec(a):
        return pl.BlockSpec((1,) + a.shape[1:], lambda b, i, flags: (b,) + (0,) * (a.ndim - 1))

    grid_spec = pltpu.PrefetchScalarGridSpec(
        num_scalar_prefetch=1,
        grid=(nb, seq // t),
        in_specs=[
            pl.BlockSpec(lam_p.shape, lambda b, i, flags: (0, 0)),
            pl.BlockSpec(subln_col.shape, lambda b, i, flags: (0, 0)),
            q_spec(A_HEADS), kv_spec(ka), kv_spec(va),
            q_spec(B_HEADS), kv_spec(kb), kv_spec(vb), kv_spec(kn2),
            pl.BlockSpec((1, N_MAPS, t), lambda b, i, flags: (b, 0, i)),
        ],
        out_specs=pl.BlockSpec((1, t, A_WIDTH + B_WIDTH), lambda b, i, flags: (b, i, 0)),
    )
    return pl.pallas_call(
        _attend_kernel,
        grid_spec=grid_spec,
        out_shape=jax.ShapeDtypeStruct((nb, seq, A_WIDTH + B_WIDTH), _BF16),
        compiler_params=pltpu.CompilerParams(
            dimension_semantics=("arbitrary", "arbitrary"), vmem_limit_bytes=VMEM_LIMIT),
        name="attend",
    )(_flags(kn2, qn2), lam_p, subln_col, qa, ka, va, qb, kb, vb, kn2, qn2)


def _post_kernel(cat_ref, x_ref, mod_ref, wo_ref, wgu_ref, wd_ref, fn_ref, o_ref):
    mod = mod_ref[0]
    g_a = mod[:, 2 * D_MODEL:3 * D_MODEL]
    sh_f = mod[:, 3 * D_MODEL:4 * D_MODEL]
    sc_f = mod[:, 4 * D_MODEL:5 * D_MODEL]
    g_f = mod[:, 5 * D_MODEL:6 * D_MODEL]
    rows = [slice(r, r + POST_SUB) for r in range(0, POST_TILE, POST_SUB)]
    n = len(rows)
    lats, acts = {}, {}

    def attn_stage(i):
        lats[i] = x_ref[0, rows[i], :] + g_a * _dot(cat_ref[0, rows[i], :], wo_ref[...])

    def gate_up_stage(i):
        h = (_rms(lats[i]) * (1.0 + sc_f) + sh_f).astype(_BF16)
        g, u = _dot(h, wgu_ref[:, :FFN_HIDDEN]), _dot(h, wgu_ref[:, FFN_HIDDEN:])
        acts[i] = (g / (1.0 + jnp.exp(-g)) * u).astype(_BF16)

    def down_stage(i):
        lat = lats.pop(i) + g_f * _dot(acts.pop(i), wd_ref[...])
        o_ref[0, rows[i], :] = _rms(lat) * fn_ref[...]

    for i in range(n + 2):
        if i < n:
            attn_stage(i)
        if 0 <= i - 1 < n:
            gate_up_stage(i - 1)
        if 0 <= i - 2 < n:
            down_stage(i - 2)


def _post(cat, x, mod3, wo, wgu, wd, fn):
    nb, seq, _ = x.shape
    t = POST_TILE

    def full(a):
        return pl.BlockSpec(a.shape, lambda b, i: (0,) * a.ndim, pipeline_mode=pl.Buffered(1))

    tok = pl.BlockSpec((1, t, D_MODEL), lambda b, i: (b, i, 0))
    return pl.pallas_call(
        _post_kernel,
        grid=(nb, seq // t),
        in_specs=[tok, tok, pl.BlockSpec((1, 1, 6 * D_MODEL), lambda b, i: (b, 0, 0)),
                  full(wo), full(wgu), full(wd), full(fn)],
        out_specs=tok,
        out_shape=jax.ShapeDtypeStruct((nb, seq, D_MODEL), _F32),
        compiler_params=pltpu.CompilerParams(
            dimension_semantics=("arbitrary", "arbitrary"), vmem_limit_bytes=VMEM_LIMIT),
        name="post",
    )(cat, x, mod3, wo, wgu, wd, fn)


def _rope_tables(seq):
    pos = np.arange(seq)
    row = (pos // GRID_W).astype(np.float32)
    col = (pos % GRID_W).astype(np.float32)
    lane = np.arange(LANES)

    def build(d, active, half, q_scale):
        quarter = half // 2
        inv = (ROPE_THETA ** (-(np.arange(quarter, dtype=np.float32)) / quarter)).astype(np.float32)
        p = np.where((d < half)[None, :], row[:, None], col[:, None]).astype(np.float32)
        ang = (p * inv[d % quarter][None, :]).astype(np.float32)
        cos = np.where(active[None, :], np.cos(ang), 1.0).astype(np.float32)
        sin = np.where(active[None, :], np.sin(ang), 0.0).astype(np.float32)
        upper = ((d % half) >= quarter)[None, :]
        s_up = np.where(upper, sin, 0.0).astype(np.float32)
        s_dn = np.where(upper, 0.0, -sin).astype(np.float32)
        ident = np.zeros((CTX_LEN, LANES), np.float32)
        nat = [jnp.asarray(np.concatenate([ident + fill, tab], 0))
               for tab, fill in ((cos, 1.0), (s_up, 0.0), (s_dn, 0.0))]
        trans = [jnp.asarray(np.ascontiguousarray(cos.T * np.float32(q_scale))),
                 jnp.asarray(np.ascontiguousarray((s_up + s_dn).T * np.float32(q_scale)))]
        return nat, trans

    nat_a, trans_a = build(lane % A_HEAD_DIM, np.ones(LANES, bool), A_HEAD_DIM // 2, A_QSCALE)
    in_rope = (lane >= B_NOPE) & (lane < B_NOPE + B_ROPE)
    nat_b, trans_b = build(np.where(in_rope, lane - B_NOPE, 0), in_rope, B_ROPE // 2, B_QSCALE)
    return nat_a + nat_b, trans_a + trans_b


def kernel(x, c, ctx, c_ctx, w_ada, b_ada, w_in, q_a_norm, kv_a_norm, w_q_up, w_kv_up,
           diff_lambda, diff_subln, w_out, w_ffn_in, w_ffn_out, final_norm):
    nb, seq, _ = x.shape
    assert w_ada.shape[0] == 1 and seq % ATT_TILE == 0 and ctx.shape[1] == CTX_LEN == TOKEN_TILE

    pad = (-(nb + 1)) % SUBLANES
    c_all = jnp.concatenate([c, c_ctx[None, :], jnp.zeros((pad, D_MODEL), _F32)], axis=0)
    mod = _adaln(c_all, w_ada[0], b_ada[0][None, :])
    mod3 = mod[:, None, :]

    w = w_in[0]
    o_k, o_v, o_cq = A_WIDTH, 2 * A_WIDTH, 3 * A_WIDTH
    o_ckv, o_kr = o_cq + B_Q_RANK, o_cq + B_Q_RANK + B_KV_RANK
    zeros = lambda n: jnp.zeros((D_MODEL, n), w.dtype)
    w_c = w[:, o_cq:o_kr].T
    w_qv = jnp.concatenate([w[:, :o_k], w[:, o_v:o_cq]], axis=1).T
    w_n = jnp.concatenate([w[:, o_k:o_v], w[:, o_ckv:o_kr], zeros(B_NOPE), w[:, o_kr:],
                           zeros(LANES - B_NOPE - B_ROPE)], axis=1)
    wq = jnp.pad(w_q_up[0].reshape(B_Q_RANK, B_HEADS, B_NOPE + B_ROPE),
                 ((0, 0), (0, 0), (0, LANES - B_NOPE - B_ROPE))).reshape(B_Q_RANK, B_HEADS * LANES)
    wkv3 = w_kv_up[0].reshape(B_KV_RANK, B_HEADS, B_NOPE + B_VDIM)
    wk = jnp.pad(wkv3[:, :, :B_NOPE], ((0, 0), (0, 0), (0, LANES - B_NOPE))).reshape(B_KV_RANK, B_HEADS * LANES)
    wv = wkv3[:, :, B_NOPE:].reshape(B_KV_RANK, B_WIDTH)
    sel = np.zeros((N_MAPS, (A_HEADS + B_HEADS) * LANES), np.float32)
    for hd in range(A_HEADS):
        sel[2 * hd, hd * LANES:hd * LANES + A_HEAD_DIM] = 1.0
        sel[2 * hd + 1, hd * LANES + A_HEAD_DIM:(hd + 1) * LANES] = 1.0
    for hd in range(B_HEADS):
        sel[2 * A_HEADS + hd, (A_HEADS + hd) * LANES:(A_HEADS + hd + 1) * LANES] = 1.0
    weights = (w_c.astype(_BF16), w_n.astype(_BF16), w_qv.astype(_BF16),
               q_a_norm.reshape(B_Q_RANK, 1), kv_a_norm.reshape(B_KV_RANK, 1),
               kv_a_norm.reshape(1, B_KV_RANK),
               wq.T.astype(_BF16), wv.T.astype(_BF16), wk.astype(_BF16), jnp.asarray(sel, _BF16))

    tables_nat, tables_t = _rope_tables(seq)
    qa, ka, va, qb, kb, vb, kn2, qn2 = _proj(x, ctx, mod3, weights, tables_nat, tables_t)

    cat = _attend(diff_lambda[0], diff_subln.reshape(LANES, 1), qa, ka, va, qb, kb, vb, kn2, qn2)

    return _post(cat, x, mod3, w_out[0].astype(_BF16), w_ffn_in[0].astype(_BF16),
                 w_ffn_out[0].astype(_BF16), final_norm[None, :])
```

```python
import math

import numpy as np
import jax
import jax.numpy as jnp
from jax import lax
from jax.experimental import pallas as pl
from jax.experimental.pallas import tpu as pltpu

D_MODEL = 1024
CTX_LEN = 256
GRID_W = 64
ROPE_THETA = 10000.0
NORM_EPS = 1e-6

A_HEADS = 4
A_HEAD_DIM = 64
A_WIDTH = A_HEADS * 2 * A_HEAD_DIM
B_HEADS = 8
B_NOPE = 64
B_ROPE = 32
B_VDIM = 64
B_Q_RANK = 256
B_KV_RANK = 128
B_WIDTH = B_HEADS * B_VDIM
FFN_HIDDEN = 2816

LANES = 128
SUBLANES = 8
TOKEN_TILE = 256
ATT_TILE = 256
ADA_TILE = 2048
PROJ_ROWS = 4
PROJ_LAG = 2
POST_TILE = 1024
POST_SUB = 256
V_PAD = 16
N_MAPS = 2 * A_HEADS + B_HEADS
V7X_VMEM_BYTES = 64 * 1024 * 1024
VMEM_LIMIT = V7X_VMEM_BYTES - 8 * 1024 * 1024

LAM_INIT = 0.8 - 0.6 * math.exp(-0.3 * 0)
LOG2E = 1.4426950408889634
A_QSCALE = A_HEAD_DIM ** -0.5 * LOG2E
B_QSCALE = (B_NOPE + B_ROPE) ** -0.5 * LOG2E

FAST_LIMIT = 50.0
BOUND_MARGIN = 1.02
FAST_DEPTH = 2
EXACT_DEPTH = 2

_F32 = jnp.float32
_BF16 = jnp.bfloat16


def _dot(a, b):
    return jnp.dot(a, b, preferred_element_type=_F32)


def _dot_nt(a, b):
    return lax.dot_general(a, b, (((1,), (1,)), ((), ())), preferred_element_type=_F32)


def _rms(x, axis=-1):
    return x * lax.rsqrt(jnp.mean(x * x, axis=axis, keepdims=True) + NORM_EPS)


def _adaln_kernel(c_ref, w_ref, b_ref, o_ref):
    c = c_ref[...]
    s = c / (1.0 + jnp.exp(-c))
    o_ref[...] = _dot(s.astype(_BF16), w_ref[...].astype(_BF16)) + b_ref[...]


def _adaln(c_all, w_ada, b_ada):
    rows = c_all.shape[0]
    n = w_ada.shape[1]
    return pl.pallas_call(
        _adaln_kernel,
        grid=(n // ADA_TILE,),
        in_specs=[
            pl.BlockSpec((rows, D_MODEL), lambda i: (0, 0)),
            pl.BlockSpec((D_MODEL, ADA_TILE), lambda i: (0, i)),
            pl.BlockSpec((1, ADA_TILE), lambda i: (0, i)),
        ],
        out_specs=pl.BlockSpec((rows, ADA_TILE), lambda i: (0, i)),
        out_shape=jax.ShapeDtypeStruct((rows, n), _F32),
        name="adaln",
    )(c_all, w_ada, b_ada)


def _rope(x, cos, s_up, s_dn, quarter):
    return (x * cos + pltpu.roll(x, quarter, 1) * s_up
            + pltpu.roll(x, LANES - quarter, 1) * s_dn)


def _rope_t(x, cos_t, sin_t, lo, hi, quarter):
    parts = [x[:lo]] if lo else []
    for r in range(lo, hi, 2 * quarter):
        parts += [x[r + quarter:r + 2 * quarter], x[r:r + quarter]]
    if hi < x.shape[0]:
        parts.append(x[hi:])
    return x * cos_t + jnp.concatenate(parts, axis=0) * sin_t


def _proj_kernel(x_ref, ctx_ref, mod_ref, modc_ref, wc_ref, wn_ref, wqv_ref, qn_ref, kvn_col_ref,
                 kvn_row_ref, wqt_ref, wvt_ref, wk_ref, sel_ref,
                 ca_ref, ua_ref, da_ref, cb_ref, ub_ref, db_ref,
                 cat_ref, sat_ref, cbt_ref, sbt_ref,
                 qa_ref, ka_ref, va_ref, qb_ref, kb_ref, vb_ref, kn2_ref, qn2_ref):
    t = TOKEN_TILE
    chains = range(PROJ_ROWS)
    is_ctx = pl.program_id(1) == 0

    hs, hts = [], []
    for i in chains:
        xin = jnp.where(is_ctx, ctx_ref[i], x_ref[i])
        mod = jnp.where(is_ctx, modc_ref[0], mod_ref[i])
        shift = mod[:, 0:D_MODEL]
        scale = mod[:, D_MODEL:2 * D_MODEL]
        h32 = _rms(xin) * (1.0 + scale) + shift
        hs.append(h32.astype(_BF16))
        hts.append(hs[-1].T)

    ycs, ys, yqvs, qbs, vbs, kns = [], [], [], [], [], []

    def first_level(i):
        ycs.append(_dot(wc_ref[...], hts[i]))
        ys.append(_dot(hs[i], wn_ref[...]))
        yqvs.append(_dot(wqv_ref[...], hts[i]))

    def second_level(i):
        yc, y = ycs[i], ys[i]
        qbs.append(_dot(wqt_ref[...], (_rms(yc[:B_Q_RANK], 0) * qn_ref[...]).astype(_BF16)))
        vbs.append(_dot(wvt_ref[...], (_rms(yc[B_Q_RANK:], 0) * kvn_col_ref[...]).astype(_BF16)))
        kns.append(_dot((_rms(y[:, A_WIDTH:A_WIDTH + B_KV_RANK]) * kvn_row_ref[...]).astype(_BF16),
                        wk_ref[...]))

    for i in chains:
        first_level(i)
        if i >= PROJ_LAG:
            second_level(i - PROJ_LAG)
    for i in range(PROJ_ROWS - PROJ_LAG, PROJ_ROWS):
        second_level(i)

    row16 = lax.broadcasted_iota(jnp.int32, (V_PAD, t), 0)
    ones_rows = jnp.where(row16 == 0, 1.0, 0.0).astype(_BF16)
    lane = lax.broadcasted_iota(jnp.int32, (t, LANES), 1)
    cat, sat = cat_ref[...], sat_ref[...]
    cbt, sbt = cbt_ref[...], sbt_ref[...]
    ca, ua, da = ca_ref[...], ua_ref[...], da_ref[...]
    cb, ub, db = cb_ref[...], ub_ref[...], db_ref[...]

    def col_sq_norm(q):
        return jnp.sum(q * q, axis=0, keepdims=True)

    def squares(k):
        return (k * k).astype(_BF16)

    for i in chains:
        yqv_t, qb_t, vb_t, y, kn = yqvs[i], qbs[i], vbs[i], ys[i], kns[i]
        qn2_rows = []
        for hd in range(A_HEADS):
            q_t = yqv_t[hd * LANES:(hd + 1) * LANES]
            q = _rope_t(q_t, cat, sat, 0, LANES, A_HEAD_DIM // 4)
            qa_ref[i, hd] = q.astype(_BF16)
            qn2_rows += [col_sq_norm(q[:A_HEAD_DIM]), col_sq_norm(q[A_HEAD_DIM:])]
            va_ref[i, hd] = yqv_t[A_WIDTH + hd * LANES:A_WIDTH + (hd + 1) * LANES].astype(_BF16)
        for hd in range(B_HEADS):
            blk = qb_t[hd * LANES:(hd + 1) * LANES]
            q = _rope_t(blk, cbt, sbt, B_NOPE, B_NOPE + B_ROPE, B_ROPE // 4)
            qb_ref[i, hd] = q.astype(_BF16)
            qn2_rows.append(col_sq_norm(q))
            vb_ref[i, hd, :B_VDIM, :] = vb_t[hd * B_VDIM:(hd + 1) * B_VDIM].astype(_BF16)
            vb_ref[i, hd, B_VDIM:, :] = ones_rows
        qn2_ref[i] = jnp.concatenate(qn2_rows, axis=0)

        ksq = []
        for hd in range(A_HEADS):
            k = _rope(y[:, hd * LANES:(hd + 1) * LANES], ca, ua, da, A_HEAD_DIM // 4)
            ka_ref[i, hd] = k.astype(_BF16)
            ksq.append(squares(k))
        kr_rot = _rope(y[:, A_WIDTH + B_KV_RANK:], cb, ub, db, B_ROPE // 4)
        for hd in range(B_HEADS):
            k = jnp.where(lane < B_NOPE, kn[:, hd * LANES:(hd + 1) * LANES], kr_rot)
            kb_ref[i, hd] = k.astype(_BF16)
            ksq.append(squares(k))
        kn2_ref[i] = _dot_nt(sel_ref[...], jnp.concatenate(ksq, axis=1))


def _proj(x, ctx, mod3, weights, tables_nat, tables_t):
    nb, seq, _ = x.shape
    n_tok = CTX_LEN + seq
    n_tiles = n_tok // TOKEN_TILE
    t = TOKEN_TILE

    def full(a):
        return pl.BlockSpec(a.shape, lambda b, j: (0,) * a.ndim)

    lat = lambda j: jnp.maximum(j - 1, 0)
    out_shapes = (
        jax.ShapeDtypeStruct((nb, A_HEADS, LANES, seq), _BF16),
        jax.ShapeDtypeStruct((nb, A_HEADS, n_tok, LANES), _BF16),
        jax.ShapeDtypeStruct((nb, A_HEADS, LANES, n_tok), _BF16),
        jax.ShapeDtypeStruct((nb, B_HEADS, LANES, seq), _BF16),
        jax.ShapeDtypeStruct((nb, B_HEADS, n_tok, LANES), _BF16),
        jax.ShapeDtypeStruct((nb, B_HEADS, B_VDIM + V_PAD, n_tok), _BF16),
        jax.ShapeDtypeStruct((nb, N_MAPS, n_tok), _F32),
        jax.ShapeDtypeStruct((nb, N_MAPS, seq), _F32),
    )
    r = PROJ_ROWS
    out_specs = (
        pl.BlockSpec((r, A_HEADS, LANES, t), lambda b, j: (b, 0, 0, lat(j))),
        pl.BlockSpec((r, A_HEADS, t, LANES), lambda b, j: (b, 0, j, 0)),
        pl.BlockSpec((r, A_HEADS, LANES, t), lambda b, j: (b, 0, 0, j)),
        pl.BlockSpec((r, B_HEADS, LANES, t), lambda b, j: (b, 0, 0, lat(j))),
        pl.BlockSpec((r, B_HEADS, t, LANES), lambda b, j: (b, 0, j, 0)),
        pl.BlockSpec((r, B_HEADS, B_VDIM + V_PAD, t), lambda b, j: (b, 0, 0, j)),
        pl.BlockSpec((r, N_MAPS, t), lambda b, j: (b, 0, j)),
        pl.BlockSpec((r, N_MAPS, t), lambda b, j: (b, 0, lat(j))),
    )
    return pl.pallas_call(
        _proj_kernel,
        grid=(nb // r, n_tiles),
        in_specs=[
            pl.BlockSpec((r, t, D_MODEL), lambda b, j: (b, lat(j), 0)),
            pl.BlockSpec((r, t, D_MODEL), lambda b, j: (b, 0, 0)),
            pl.BlockSpec((r, 1, 6 * D_MODEL), lambda b, j: (b, 0, 0)),
            pl.BlockSpec((1, 1, 6 * D_MODEL), lambda b, j: (nb, 0, 0)),
        ] + [full(w) for w in weights]
          + [pl.BlockSpec((t, LANES), lambda b, j: (j, 0))] * len(tables_nat)
          + [pl.BlockSpec((LANES, t), lambda b, j: (0, lat(j)))] * len(tables_t),
        out_specs=out_specs,
        out_shape=out_shapes,
        compiler_params=pltpu.CompilerParams(
            dimension_semantics=("arbitrary", "arbitrary"), vmem_limit_bytes=VMEM_LIMIT),
        name="proj",
    )(x, ctx, mod3, mod3, *weights, *tables_nat, *tables_t)


def _score_bound(qn2, kn2):
    return jnp.sqrt(qn2 * jnp.max(kn2, axis=-1, keepdims=True)) * BOUND_MARGIN


def _flags_kernel(kn2_ref, qn2_ref, o_ref):
    kn2 = kn2_ref[...]
    cols = []
    for i in range(qn2_ref.shape[2] // ATT_TILE):
        bound = _score_bound(qn2_ref[:, :, i * ATT_TILE:(i + 1) * ATT_TILE], kn2)
        worst = jnp.max(jnp.max(bound, axis=2), axis=1, keepdims=True)
        cols.append(jnp.where(worst <= FAST_LIMIT, 1, 0).astype(jnp.int32))
    o_ref[...] = jnp.concatenate(cols, axis=1)


def _flags(kn2, qn2):
    nb, _, seq = qn2.shape
    return pl.pallas_call(
        _flags_kernel,
        out_shape=jax.ShapeDtypeStruct((nb, seq // ATT_TILE), jnp.int32),
        name="flags",
    )(kn2, qn2)


def _attend_body(use_bound, depth, lam_ref, subln_ref, qa_ref, ka_ref, va_ref, qb_ref, kb_ref, vb_ref,
                 kn2_ref, qn2_ref, o_ref):
    t = ATT_TILE
    lp = lam_ref[...]
    lam = (jnp.exp(jnp.sum(lp[0:1] * lp[1:2], axis=-1, keepdims=True))
           - jnp.exp(jnp.sum(lp[2:3] * lp[3:4], axis=-1, keepdims=True)) + LAM_INIT)
    gain = subln_ref[...] * (1.0 - LAM_INIT)
    row = lax.broadcasted_iota(jnp.int32, (LANES, t), 0)
    lo = row < A_HEAD_DIM

    maps = []
    for hd in range(A_HEADS):
        q_t = qa_ref[0, hd]
        zero = jnp.zeros_like(q_t)
        for half in range(2):
            q_m = jnp.where(lo, q_t, zero) if half == 0 else jnp.where(lo, zero, q_t)
            maps.append((q_m, ka_ref.at[0, hd], va_ref.at[0, hd], LANES))
    for hd in range(B_HEADS):
        maps.append((qb_ref[0, hd], kb_ref.at[0, hd], vb_ref.at[0, hd], B_VDIM))

    def scores_t(i):
        q_m, k_ref, _, _ = maps[i]
        return _dot(k_ref[...], q_m)

    bounds = _score_bound(qn2_ref[0], kn2_ref[0]) if use_bound else None

    def finish(i, o_t, acc):
        if i < 2 * A_HEADS:
            hd, half = divmod(i, 2)
            if half == 0:
                acc[hd] = o_t
                return
            o = acc.pop(hd) - lam * o_t
            o = _rms(o, 0) * gain
            o_ref[0, :, hd * LANES:(hd + 1) * LANES] = o.T.astype(o_ref.dtype)
        else:
            hd = i - 2 * A_HEADS
            if hd % 2 == 0:
                acc[hd] = o_t
                return
            o = jnp.concatenate([acc.pop(hd - 1), o_t], axis=0)
            p = hd // 2
            o_ref[0, :, A_WIDTH + p * LANES:A_WIDTH + (p + 1) * LANES] = o.T.astype(o_ref.dtype)

    acc = {}

    def softmax_pv(i, s_t):
        _, _, v_ref, dv = maps[i]
        m = bounds[i:i + 1] if use_bound else jnp.max(s_t, axis=0, keepdims=True)
        p = jnp.exp2(s_t - m)
        if v_ref.shape[0] == dv:
            o = _dot(v_ref[...], p.astype(_BF16)) / jnp.sum(p, axis=0, keepdims=True)
        else:
            o = _dot(v_ref[...], p.astype(_BF16))
            o = o[:dv] / o[dv:dv + 1]
        finish(i, o, acc)

    pending = []
    for i in range(len(maps)):
        pending.append((i, scores_t(i)))
        if len(pending) > depth:
            softmax_pv(*pending.pop(0))
    for item in pending:
        softmax_pv(*item)


def _attend_kernel(flags_ref, *refs):
    fast_ok = flags_ref[pl.program_id(0), pl.program_id(1)] != 0

    @pl.when(fast_ok)
    def _():
        _attend_body(True, FAST_DEPTH, *refs)

    @pl.when(jnp.logical_not(fast_ok))
    def _():
        _attend_body(False, EXACT_DEPTH, *refs)


def _attend(lam_p, subln_col, qa, ka, va, qb, kb, vb, kn2, qn2):
    nb, _, _, seq = qa.shape
    t = ATT_TILE

    def q_spec(nh):
        return pl.BlockSpec((1, nh, LANES, t), lambda b, i, flags: (b, 0, 0, i))

    def kv_spec(a):
        return pl.BlockSpec((1,) + a.shape[1:], lambda b, i, flags: (b,) + (0,) * (a.ndim - 1))

    grid_spec = pltpu.PrefetchScalarGridSpec(
        num_scalar_prefetch=1,
        grid=(nb, seq // t),
        in_specs=[
            pl.BlockSpec(lam_p.shape, lambda b, i, flags: (0, 0)),
            pl.BlockSpec(subln_col.shape, lambda b, i, flags: (0, 0)),
            q_spec(A_HEADS), kv_spec(ka), kv_spec(va),
            q_spec(B_HEADS), kv_spec(kb), kv_spec(vb), kv_spec(kn2),
            pl.BlockSpec((1, N_MAPS, t), lambda b, i, flags: (b, 0, i)),
        ],
        out_specs=pl.BlockSpec((1, t, A_WIDTH + B_WIDTH), lambda b, i, flags: (b, i, 0)),
    )
    return pl.pallas_call(
        _attend_kernel,
        grid_spec=grid_spec,
        out_shape=jax.ShapeDtypeStruct((nb, seq, A_WIDTH + B_WIDTH), _BF16),
        compiler_params=pltpu.CompilerParams(
            dimension_semantics=("arbitrary", "arbitrary"), vmem_limit_bytes=VMEM_LIMIT),
        name="attend",
    )(_flags(kn2, qn2), lam_p, subln_col, qa, ka, va, qb, kb, vb, kn2, qn2)


def _post_kernel(cat_ref, x_ref, mod_ref, wo_ref, wgu_ref, wd_ref, fn_ref, o_ref):
    mod = mod_ref[0]
    g_a = mod[:, 2 * D_MODEL:3 * D_MODEL]
    sh_f = mod[:, 3 * D_MODEL:4 * D_MODEL]
    sc_f = mod[:, 4 * D_MODEL:5 * D_MODEL]
    g_f = mod[:, 5 * D_MODEL:6 * D_MODEL]
    rows = [slice(r, r + POST_SUB) for r in range(0, POST_TILE, POST_SUB)]
    n = len(rows)
    lats, acts = {}, {}

    def attn_stage(i):
        lats[i] = x_ref[0, rows[i], :] + g_a * _dot(cat_ref[0, rows[i], :], wo_ref[...])

    def gate_up_stage(i):
        h = (_rms(lats[i]) * (1.0 + sc_f) + sh_f).astype(_BF16)
        g, u = _dot(h, wgu_ref[:, :FFN_HIDDEN]), _dot(h, wgu_ref[:, FFN_HIDDEN:])
        acts[i] = (g / (1.0 + jnp.exp(-g)) * u).astype(_BF16)

    def down_stage(i):
        lat = lats.pop(i) + g_f * _dot(acts.pop(i), wd_ref[...])
        o_ref[0, rows[i], :] = _rms(lat) * fn_ref[...]

    for i in range(n + 2):
        if i < n:
            attn_stage(i)
        if 0 <= i - 1 < n:
            gate_up_stage(i - 1)
        if 0 <= i - 2 < n:
            down_stage(i - 2)


def _post(cat, x, mod3, wo, wgu, wd, fn):
    nb, seq, _ = x.shape
    t = POST_TILE

    def full(a):
        return pl.BlockSpec(a.shape, lambda b, i: (0,) * a.ndim, pipeline_mode=pl.Buffered(1))

    tok = pl.BlockSpec((1, t, D_MODEL), lambda b, i: (b, i, 0))
    return pl.pallas_call(
        _post_kernel,
        grid=(nb, seq // t),
        in_specs=[tok, tok, pl.BlockSpec((1, 1, 6 * D_MODEL), lambda b, i: (b, 0, 0)),
                  full(wo), full(wgu), full(wd), full(fn)],
        out_specs=tok,
        out_shape=jax.ShapeDtypeStruct((nb, seq, D_MODEL), _F32),
        compiler_params=pltpu.CompilerParams(
            dimension_semantics=("arbitrary", "arbitrary"), vmem_limit_bytes=VMEM_LIMIT),
        name="post",
    )(cat, x, mod3, wo, wgu, wd, fn)


def _rope_tables(seq):
    pos = np.arange(seq)
    row = (pos // GRID_W).astype(np.float32)
    col = (pos % GRID_W).astype(np.float32)
    lane = np.arange(LANES)

    def build(d, active, half, q_scale):
        quarter = half // 2
        inv = (ROPE_THETA ** (-(np.arange(quarter, dtype=np.float32)) / quarter)).astype(np.float32)
        p = np.where((d < half)[None, :], row[:, None], col[:, None]).astype(np.float32)
        ang = (p * inv[d % quarter][None, :]).astype(np.float32)
        cos = np.where(active[None, :], np.cos(ang), 1.0).astype(np.float32)
        sin = np.where(active[None, :], np.sin(ang), 0.0).astype(np.float32)
        upper = ((d % half) >= quarter)[None, :]
        s_up = np.where(upper, sin, 0.0).astype(np.float32)
        s_dn = np.where(upper, 0.0, -sin).astype(np.float32)
        ident = np.zeros((CTX_LEN, LANES), np.float32)
        nat = [jnp.asarray(np.concatenate([ident + fill, tab], 0))
               for tab, fill in ((cos, 1.0), (s_up, 0.0), (s_dn, 0.0))]
        trans = [jnp.asarray(np.ascontiguousarray(cos.T * np.float32(q_scale))),
                 jnp.asarray(np.ascontiguousarray((s_up + s_dn).T * np.float32(q_scale)))]
        return nat, trans

    nat_a, trans_a = build(lane % A_HEAD_DIM, np.ones(LANES, bool), A_HEAD_DIM // 2, A_QSCALE)
    in_rope = (lane >= B_NOPE) & (lane < B_NOPE + B_ROPE)
    nat_b, trans_b = build(np.where(in_rope, lane - B_NOPE, 0), in_rope, B_ROPE // 2, B_QSCALE)
    return nat_a + nat_b, trans_a + trans_b


def kernel(x, c, ctx, c_ctx, w_ada, b_ada, w_in, q_a_norm, kv_a_norm, w_q_up, w_kv_up,
           diff_lambda, diff_subln, w_out, w_ffn_in, w_ffn_out, final_norm):
    nb, seq, _ = x.shape
    assert w_ada.shape[0] == 1 and seq % ATT_TILE == 0 and ctx.shape[1] == CTX_LEN == TOKEN_TILE

    pad = (-(nb + 1)) % SUBLANES
    c_all = jnp.concatenate([c, c_ctx[None, :], jnp.zeros((pad, D_MODEL), _F32)], axis=0)
    mod = _adaln(c_all, w_ada[0], b_ada[0][None, :])
    mod3 = mod[:, None, :]

    w = w_in[0]
    o_k, o_v, o_cq = A_WIDTH, 2 * A_WIDTH, 3 * A_WIDTH
    o_ckv, o_kr = o_cq + B_Q_RANK, o_cq + B_Q_RANK + B_KV_RANK
    zeros = lambda n: jnp.zeros((D_MODEL, n), w.dtype)
    w_c = w[:, o_cq:o_kr].T
    w_qv = jnp.concatenate([w[:, :o_k], w[:, o_v:o_cq]], axis=1).T
    w_n = jnp.concatenate([w[:, o_k:o_v], w[:, o_ckv:o_kr], zeros(B_NOPE), w[:, o_kr:],
                           zeros(LANES - B_NOPE - B_ROPE)], axis=1)
    wq = jnp.pad(w_q_up[0].reshape(B_Q_RANK, B_HEADS, B_NOPE + B_ROPE),
                 ((0, 0), (0, 0), (0, LANES - B_NOPE - B_ROPE))).reshape(B_Q_RANK, B_HEADS * LANES)
    wkv3 = w_kv_up[0].reshape(B_KV_RANK, B_HEADS, B_NOPE + B_VDIM)
    wk = jnp.pad(wkv3[:, :, :B_NOPE], ((0, 0), (0, 0), (0, LANES - B_NOPE))).reshape(B_KV_RANK, B_HEADS * LANES)
    wv = wkv3[:, :, B_NOPE:].reshape(B_KV_RANK, B_WIDTH)
    sel = np.zeros((N_MAPS, (A_HEADS + B_HEADS) * LANES), np.float32)
    for hd in range(A_HEADS):
        sel[2 * hd, hd * LANES:hd * LANES + A_HEAD_DIM] = 1.0
        sel[2 * hd + 1, hd * LANES + A_HEAD_DIM:(hd + 1) * LANES] = 1.0
    for hd in range(B_HEADS):
        sel[2 * A_HEADS + hd, (A_HEADS + hd) * LANES:(A_HEADS + hd + 1) * LANES] = 1.0
    weights = (w_c.astype(_BF16), w_n.astype(_BF16), w_qv.astype(_BF16),
               q_a_norm.reshape(B_Q_RANK, 1), kv_a_norm.reshape(B_KV_RANK, 1),
               kv_a_norm.reshape(1, B_KV_RANK),
               wq.T.astype(_BF16), wv.T.astype(_BF16), wk.astype(_BF16), jnp.asarray(sel, _BF16))

    tables_nat, tables_t = _rope_tables(seq)
    qa, ka, va, qb, kb, vb, kn2, qn2 = _proj(x, ctx, mod3, weights, tables_nat, tables_t)

    cat = _attend(diff_lambda[0], diff_subln.reshape(LANES, 1), qa, ka, va, qb, kb, vb, kn2, qn2)

    return _post(cat, x, mod3, w_out[0].astype(_BF16), w_ffn_in[0].astype(_BF16),
                 w_ffn_out[0].astype(_BF16), final_norm[None, :])
```

```python
import math

import numpy as np
import jax
import jax.numpy as jnp
from jax import lax
from jax.experimental import pallas as pl
from jax.experimental.pallas import tpu as pltpu

D_MODEL = 1024
CTX_LEN = 256
GRID_W = 64
ROPE_THETA = 10000.0
NORM_EPS = 1e-6

A_HEADS = 4
A_HEAD_DIM = 64
A_WIDTH = A_HEADS * 2 * A_HEAD_DIM
B_HEADS = 8
B_NOPE = 64
B_ROPE = 32
B_VDIM = 64
B_Q_RANK = 256
B_KV_RANK = 128
B_WIDTH = B_HEADS * B_VDIM
FFN_HIDDEN = 2816

LANES = 128
SUBLANES = 8
TOKEN_TILE = 256
ATT_TILE = 256
ADA_TILE = 2048
PROJ_ROWS = 4
PROJ_LAG = 1
POST_TILE = 1024
POST_SUB = 256
ONE_LANE = B_NOPE + B_ROPE
N_MAPS = 2 * A_HEADS + B_HEADS
V7X_VMEM_BYTES = 64 * 1024 * 1024
VMEM_LIMIT = V7X_VMEM_BYTES - 8 * 1024 * 1024

LAM_INIT = 0.8 - 0.6 * math.exp(-0.3 * 0)
LOG2E = 1.4426950408889634
A_QSCALE = A_HEAD_DIM ** -0.5 * LOG2E
B_QSCALE = (B_NOPE + B_ROPE) ** -0.5 * LOG2E

FAST_LIMIT = 50.0
BOUND_MARGIN = 1.02
FAST_DEPTH = 2
EXACT_DEPTH = 2

_F32 = jnp.float32
_BF16 = jnp.bfloat16


def _dot(a, b):
    return jnp.dot(a, b, preferred_element_type=_F32)


def _dot_nt(a, b):
    return lax.dot_general(a, b, (((1,), (1,)), ((), ())), preferred_element_type=_F32)


def _rms(x, axis=-1):
    return x * lax.rsqrt(jnp.mean(x * x, axis=axis, keepdims=True) + NORM_EPS)


def _adaln_kernel(c_ref, w_ref, b_ref, o_ref):
    c = c_ref[...]
    s = c / (1.0 + jnp.exp(-c))
    o_ref[...] = _dot(s.astype(_BF16), w_ref[...].astype(_BF16)) + b_ref[...]


def _adaln(c_all, w_ada, b_ada):
    rows = c_all.shape[0]
    n = w_ada.shape[1]
    return pl.pallas_call(
        _adaln_kernel,
        grid=(n // ADA_TILE,),
        in_specs=[
            pl.BlockSpec((rows, D_MODEL), lambda i: (0, 0)),
            pl.BlockSpec((D_MODEL, ADA_TILE), lambda i: (0, i)),
            pl.BlockSpec((1, ADA_TILE), lambda i: (0, i)),
        ],
        out_specs=pl.BlockSpec((rows, ADA_TILE), lambda i: (0, i)),
        out_shape=jax.ShapeDtypeStruct((rows, n), _F32),
        name="adaln",
    )(c_all, w_ada, b_ada)


def _rope(x, cos, s_up, s_dn, quarter):
    return (x * cos + pltpu.roll(x, quarter, 1) * s_up
            + pltpu.roll(x, LANES - quarter, 1) * s_dn)


def _rope_t(x, cos_t, sin_t, lo, hi, quarter):
    parts = [x[:lo]] if lo else []
    for r in range(lo, hi, 2 * quarter):
        parts += [x[r + quarter:r + 2 * quarter], x[r:r + quarter]]
    if hi < x.shape[0]:
        parts.append(x[hi:])
    return x * cos_t + jnp.concatenate(parts, axis=0) * sin_t


def _proj_kernel(x_ref, ctx_ref, mod_ref, modc_ref, wc_ref, wn_ref, wqv_ref, qn_ref, kvn_col_ref,
                 kvn_row_ref, wqt_ref, wvt_ref, wk_ref, sel_ref,
                 ca_ref, ua_ref, da_ref, cb_ref, ub_ref, db_ref,
                 cat_ref, sat_ref, cbt_ref, sbt_ref,
                 qa_ref, ka_ref, va_ref, qb_ref, kb_ref, vb_ref, kn2_ref, qn2_ref):
    t = TOKEN_TILE
    chains = range(PROJ_ROWS)
    is_ctx = pl.program_id(1) == 0

    hs, hts = [], []
    for i in chains:
        xin = jnp.where(is_ctx, ctx_ref[i], x_ref[i])
        mod = jnp.where(is_ctx, modc_ref[0], mod_ref[i])
        shift = mod[:, 0:D_MODEL]
        scale = mod[:, D_MODEL:2 * D_MODEL]
        h32 = _rms(xin) * (1.0 + scale) + shift
        hs.append(h32.astype(_BF16))
        hts.append(hs[-1].T)

    ycs, ys, yqvs, qbs, vbs, kns = [], [], [], [], [], []

    def first_level(i):
        ycs.append(_dot(wc_ref[...], hts[i]))
        ys.append(_dot(hs[i], wn_ref[...]))
        yqvs.append(_dot(wqv_ref[...], hts[i]))

    def second_level(i):
        yc, y = ycs[i], ys[i]
        qbs.append(_dot(wqt_ref[...], (_rms(yc[:B_Q_RANK], 0) * qn_ref[...]).astype(_BF16)))
        vbs.append(_dot(wvt_ref[...], (_rms(yc[B_Q_RANK:], 0) * kvn_col_ref[...]).astype(_BF16)))
        kns.append(_dot((_rms(y[:, A_WIDTH:A_WIDTH + B_KV_RANK]) * kvn_row_ref[...]).astype(_BF16),
                        wk_ref[...]))

    for i in chains:
        first_level(i)
        if i >= PROJ_LAG:
            second_level(i - PROJ_LAG)
    for i in range(PROJ_ROWS - PROJ_LAG, PROJ_ROWS):
        second_level(i)

    lane = lax.broadcasted_iota(jnp.int32, (t, LANES), 1)
    one_lane = jnp.where(lax.broadcasted_iota(jnp.int32, (1, LANES), 1) == ONE_LANE, 1.0, 0.0)
    cat, sat = cat_ref[...], sat_ref[...]
    cbt, sbt = cbt_ref[...], sbt_ref[...]
    ca, ua, da = ca_ref[...], ua_ref[...], da_ref[...]
    cb, ub, db = cb_ref[...], ub_ref[...], db_ref[...]

    def col_sq_norm(q):
        return jnp.sum(q * q, axis=0, keepdims=True)

    def squares(k):
        return (k * k).astype(_BF16)

    for i in chains:
        yqv_t, qb_t, vb_t, y, kn = yqvs[i], qbs[i], vbs[i], ys[i], kns[i]
        qn2_rows = []
        for hd in range(A_HEADS):
            q_t = yqv_t[hd * LANES:(hd + 1) * LANES]
            q = _rope_t(q_t, cat, sat, 0, LANES, A_HEAD_DIM // 4)
            qa_ref[i, hd] = q.astype(_BF16)
            qn2_rows += [col_sq_norm(q[:A_HEAD_DIM]), col_sq_norm(q[A_HEAD_DIM:])]
            va_ref[i, hd] = yqv_t[A_WIDTH + hd * LANES:A_WIDTH + (hd + 1) * LANES].astype(_BF16)
        for hd in range(B_HEADS):
            blk = qb_t[hd * LANES:(hd + 1) * LANES]
            q = _rope_t(blk, cbt, sbt, B_NOPE, B_NOPE + B_ROPE, B_ROPE // 4)
            qb_ref[i, hd] = q.astype(_BF16)
            qn2_rows.append(col_sq_norm(q))
            vb_ref[i, hd] = vb_t[hd * B_VDIM:(hd + 1) * B_VDIM].astype(_BF16)
        qn2_ref[i] = jnp.concatenate(qn2_rows, axis=0)

        ksq = []
        for hd in range(A_HEADS):
            k = _rope(y[:, hd * LANES:(hd + 1) * LANES], ca, ua, da, A_HEAD_DIM // 4)
            ka_ref[i, hd] = k.astype(_BF16)
            ksq.append(squares(k))
        kr_rot = _rope(y[:, A_WIDTH + B_KV_RANK:], cb, ub, db, B_ROPE // 4) + one_lane
        for hd in range(B_HEADS):
            k = jnp.where(lane < B_NOPE, kn[:, hd * LANES:(hd + 1) * LANES], kr_rot)
            kb_ref[i, hd] = k.astype(_BF16)
            ksq.append(squares(k))
        kn2_ref[i] = _dot_nt(sel_ref[...], jnp.concatenate(ksq, axis=1))


def _proj(x, ctx, mod3, weights, tables_nat, tables_t):
    nb, seq, _ = x.shape
    n_tok = CTX_LEN + seq
    n_tiles = n_tok // TOKEN_TILE
    t = TOKEN_TILE

    def full(a):
        return pl.BlockSpec(a.shape, lambda b, j: (0,) * a.ndim)

    lat = lambda j: jnp.maximum(j - 1, 0)
    out_shapes = (
        jax.ShapeDtypeStruct((nb, A_HEADS, LANES, seq), _BF16),
        jax.ShapeDtypeStruct((nb, A_HEADS, n_tok, LANES), _BF16),
        jax.ShapeDtypeStruct((nb, A_HEADS, LANES, n_tok), _BF16),
        jax.ShapeDtypeStruct((nb, B_HEADS, LANES, seq), _BF16),
        jax.ShapeDtypeStruct((nb, B_HEADS, n_tok, LANES), _BF16),
        jax.ShapeDtypeStruct((nb, B_HEADS, B_VDIM, n_tok), _BF16),
        jax.ShapeDtypeStruct((nb, N_MAPS, n_tok), _F32),
        jax.ShapeDtypeStruct((nb, N_MAPS, seq), _F32),
    )
    r = PROJ_ROWS
    out_specs = (
        pl.BlockSpec((r, A_HEADS, LANES, t), lambda b, j: (b, 0, 0, lat(j))),
        pl.BlockSpec((r, A_HEADS, t, LANES), lambda b, j: (b, 0, j, 0)),
        pl.BlockSpec((r, A_HEADS, LANES, t), lambda b, j: (b, 0, 0, j)),
        pl.BlockSpec((r, B_HEADS, LANES, t), lambda b, j: (b, 0, 0, lat(j))),
        pl.BlockSpec((r, B_HEADS, t, LANES), lambda b, j: (b, 0, j, 0)),
        pl.BlockSpec((r, B_HEADS, B_VDIM, t), lambda b, j: (b, 0, 0, j)),
        pl.BlockSpec((r, N_MAPS, t), lambda b, j: (b, 0, j)),
        pl.BlockSpec((r, N_MAPS, t), lambda b, j: (b, 0, lat(j))),
    )
    return pl.pallas_call(
        _proj_kernel,
        grid=(nb // r, n_tiles),
        in_specs=[
            pl.BlockSpec((r, t, D_MODEL), lambda b, j: (b, lat(j), 0)),
            pl.BlockSpec((r, t, D_MODEL), lambda b, j: (b, 0, 0)),
            pl.BlockSpec((r, 1, 6 * D_MODEL), lambda b, j: (b, 0, 0)),
            pl.BlockSpec((1, 1, 6 * D_MODEL), lambda b, j: (nb, 0, 0)),
        ] + [full(w) for w in weights]
          + [pl.BlockSpec((t, LANES), lambda b, j: (j, 0))] * len(tables_nat)
          + [pl.BlockSpec((LANES, t), lambda b, j: (0, lat(j)))] * len(tables_t),
        out_specs=out_specs,
        out_shape=out_shapes,
        compiler_params=pltpu.CompilerParams(
            dimension_semantics=("arbitrary", "arbitrary"), vmem_limit_bytes=VMEM_LIMIT),
        name="proj",
    )(x, ctx, mod3, mod3, *weights, *tables_nat, *tables_t)


def _score_bound(qn2, kn2):
    return jnp.sqrt(qn2 * jnp.max(kn2, axis=-1, keepdims=True)) * BOUND_MARGIN


def _flags_kernel(kn2_ref, qn2_ref, o_ref):
    kn2 = kn2_ref[...]
    cols = []
    for i in range(qn2_ref.shape[2] // ATT_TILE):
        bound = _score_bound(qn2_ref[:, :, i * ATT_TILE:(i + 1) * ATT_TILE], kn2)
        worst = jnp.max(jnp.max(bound, axis=2), axis=1, keepdims=True)
        cols.append(jnp.where(worst <= FAST_LIMIT, 1, 0).astype(jnp.int32))
    o_ref[...] = jnp.concatenate(cols, axis=1)


def _flags(kn2, qn2):
    nb, _, seq = qn2.shape
    return pl.pallas_call(
        _flags_kernel,
        out_shape=jax.ShapeDtypeStruct((nb, seq // ATT_TILE), jnp.int32),
        name="flags",
    )(kn2, qn2)


def _attend_body(use_bound, depth, lam_ref, subln_ref, qa_ref, ka_ref, va_ref, qb_ref, kb_ref, vb_ref,
                 kn2_ref, qn2_ref, o_ref):
    t = ATT_TILE
    lp = lam_ref[...]
    lam = (jnp.exp(jnp.sum(lp[0:1] * lp[1:2], axis=-1, keepdims=True))
           - jnp.exp(jnp.sum(lp[2:3] * lp[3:4], axis=-1, keepdims=True)) + LAM_INIT)
    gain = subln_ref[...] * (1.0 - LAM_INIT)
    row = lax.broadcasted_iota(jnp.int32, (LANES, t), 0)
    lo = row < A_HEAD_DIM

    maps = []
    for hd in range(A_HEADS):
        q_t = qa_ref[0, hd]
        zero = jnp.zeros_like(q_t)
        for half in range(2):
            q_m = jnp.where(lo, q_t, zero) if half == 0 else jnp.where(lo, zero, q_t)
            maps.append((q_m, ka_ref.at[0, hd], va_ref.at[0, hd], LANES))
    for hd in range(B_HEADS):
        maps.append((qb_ref[0, hd], kb_ref.at[0, hd], vb_ref.at[0, hd], B_VDIM))

    bounds = _score_bound(qn2_ref[0], kn2_ref[0]) if use_bound else None

    def shift_in_matmul(i):
        return use_bound and i >= 2 * A_HEADS

    def scores_t(i):
        q_m, k_ref, _, _ = maps[i]
        if shift_in_matmul(i):
            q_m = jnp.where(row == ONE_LANE, (-bounds[i:i + 1]).astype(_BF16), q_m)
        return _dot(k_ref[...], q_m)

    def finish(i, o_t, acc):
        if i < 2 * A_HEADS:
            hd, half = divmod(i, 2)
            if half == 0:
                acc[hd] = o_t
                return
            o = acc.pop(hd) - lam * o_t
            o = _rms(o, 0) * gain
            o_ref[0, :, hd * LANES:(hd + 1) * LANES] = o.T.astype(o_ref.dtype)
        else:
            hd = i - 2 * A_HEADS
            if hd % 2 == 0:
                acc[hd] = o_t
                return
            o = jnp.concatenate([acc.pop(hd - 1), o_t], axis=0)
            p = hd // 2
            o_ref[0, :, A_WIDTH + p * LANES:A_WIDTH + (p + 1) * LANES] = o.T.astype(o_ref.dtype)

    acc = {}

    def softmax_pv(i, s_t):
        v_ref = maps[i][2]
        if shift_in_matmul(i):
            p = jnp.exp2(s_t)
        else:
            m = bounds[i:i + 1] if use_bound else jnp.max(s_t, axis=0, keepdims=True)
            p = jnp.exp2(s_t - m)
        o = _dot(v_ref[...], p.astype(_BF16)) / jnp.sum(p, axis=0, keepdims=True)
        finish(i, o, acc)

    pending = []
    for i in range(len(maps)):
        pending.append((i, scores_t(i)))
        if len(pending) > depth:
            softmax_pv(*pending.pop(0))
    for item in pending:
        softmax_pv(*item)


def _attend_kernel(flags_ref, *refs):
    fast_ok = flags_ref[pl.program_id(0), pl.program_id(1)] != 0

    @pl.when(fast_ok)
    def _():
        _attend_body(True, FAST_DEPTH, *refs)

    @pl.when(jnp.logical_not(fast_ok))
    def _():
        _attend_body(False, EXACT_DEPTH, *refs)


def _attend(lam_p, subln_col, qa, ka, va, qb, kb, vb, kn2, qn2):
    nb, _, _, seq = qa.shape
    t = ATT_TILE

    def q_spec(nh):
        return pl.BlockSpec((1, nh, LANES, t), lambda b, i, flags: (b, 0, 0, i))

    def kv_spec(a):
        return pl.BlockSpec((1,) + a.shape[1:], lambda b, i, flags: (b,) + (0,) * (a.ndim - 1))

    grid_spec = pltpu.PrefetchScalarGridSpec(
        num_scalar_prefetch=1,
        grid=(nb, seq // t),
        in_specs=[
            pl.BlockSpec(lam_p.shape, lambda b, i, flags: (0, 0)),
            pl.BlockSpec(subln_col.shape, lambda b, i, flags: (0, 0)),
            q_spec(A_HEADS), kv_spec(ka), kv_spec(va),
            q_spec(B_HEADS), kv_spec(kb), kv_spec(vb), kv_spec(kn2),
            pl.BlockSpec((1, N_MAPS, t), lambda b, i, flags: (b, 0, i)),
        ],
        out_specs=pl.BlockSpec((1, t, A_WIDTH + B_WIDTH), lambda b, i, flags: (b, i, 0)),
    )
    return pl.pallas_call(
        _attend_kernel,
        grid_spec=grid_spec,
        out_shape=jax.ShapeDtypeStruct((nb, seq, A_WIDTH + B_WIDTH), _BF16),
        compiler_params=pltpu.CompilerParams(
            dimension_semantics=("arbitrary", "arbitrary"), vmem_limit_bytes=VMEM_LIMIT),
        name="attend",
    )(_flags(kn2, qn2), lam_p, subln_col, qa, ka, va, qb, kb, vb, kn2, qn2)


def _post_kernel(cat_ref, x_ref, mod_ref, wo_ref, wgu_ref, wd_ref, fn_ref, o_ref):
    mod = mod_ref[0]
    g_a = mod[:, 2 * D_MODEL:3 * D_MODEL]
    sh_f = mod[:, 3 * D_MODEL:4 * D_MODEL]
    sc_f = mod[:, 4 * D_MODEL:5 * D_MODEL]
    g_f = mod[:, 5 * D_MODEL:6 * D_MODEL]
    rows = [slice(r, r + POST_SUB) for r in range(0, POST_TILE, POST_SUB)]
    n = len(rows)
    lats, acts = {}, {}

    def attn_stage(i):
        lats[i] = x_ref[0, rows[i], :] + g_a * _dot(cat_ref[0, rows[i], :], wo_ref[...])

    def gate_up_stage(i):
        h = (_rms(lats[i]) * (1.0 + sc_f) + sh_f).astype(_BF16)
        g, u = _dot(h, wgu_ref[:, :FFN_HIDDEN]), _dot(h, wgu_ref[:, FFN_HIDDEN:])
        acts[i] = (g / (1.0 + jnp.exp(-g)) * u).astype(_BF16)

    def down_stage(i):
        lat = lats.pop(i) + g_f * _dot(acts.pop(i), wd_ref[...])
        o_ref[0, rows[i], :] = _rms(lat) * fn_ref[...]

    for i in range(n + 2):
        if i < n:
            attn_stage(i)
        if 0 <= i - 1 < n:
            gate_up_stage(i - 1)
        if 0 <= i - 2 < n:
            down_stage(i - 2)


def _post(cat, x, mod3, wo, wgu, wd, fn):
    nb, seq, _ = x.shape
    t = POST_TILE

    def full(a):
        return pl.BlockSpec(a.shape, lambda b, i: (0,) * a.ndim, pipeline_mode=pl.Buffered(1))

    tok = pl.BlockSpec((1, t, D_MODEL), lambda b, i: (b, i, 0))
    return pl.pallas_call(
        _post_kernel,
        grid=(nb, seq // t),
        in_specs=[tok, tok, pl.BlockSpec((1, 1, 6 * D_MODEL), lambda b, i: (b, 0, 0)),
                  full(wo), full(wgu), full(wd), full(fn)],
        out_specs=tok,
        out_shape=jax.ShapeDtypeStruct((nb, seq, D_MODEL), _F32),
        compiler_params=pltpu.CompilerParams(
            dimension_semantics=("arbitrary", "arbitrary"), vmem_limit_bytes=VMEM_LIMIT),
        name="post",
    )(cat, x, mod3, wo, wgu, wd, fn)


def _rope_tables(seq):
    pos = np.arange(seq)
    row = (pos // GRID_W).astype(np.float32)
    col = (pos % GRID_W).astype(np.float32)
    lane = np.arange(LANES)

    def build(d, active, half, q_scale):
        quarter = half // 2
        inv = (ROPE_THETA ** (-(np.arange(quarter, dtype=np.float32)) / quarter)).astype(np.float32)
        p = np.where((d < half)[None, :], row[:, None], col[:, None]).astype(np.float32)
        ang = (p * inv[d % quarter][None, :]).astype(np.float32)
        cos = np.where(active[None, :], np.cos(ang), 1.0).astype(np.float32)
        sin = np.where(active[None, :], np.sin(ang), 0.0).astype(np.float32)
        upper = ((d % half) >= quarter)[None, :]
        s_up = np.where(upper, sin, 0.0).astype(np.float32)
        s_dn = np.where(upper, 0.0, -sin).astype(np.float32)
        ident = np.zeros((CTX_LEN, LANES), np.float32)
        nat = [jnp.asarray(np.concatenate([ident + fill, tab], 0))
               for tab, fill in ((cos, 1.0), (s_up, 0.0), (s_dn, 0.0))]
        trans = [jnp.asarray(np.ascontiguousarray(cos.T * np.float32(q_scale))),
                 jnp.asarray(np.ascontiguousarray((s_up + s_dn).T * np.float32(q_scale)))]
        return nat, trans

    nat_a, trans_a = build(lane % A_HEAD_DIM, np.ones(LANES, bool), A_HEAD_DIM // 2, A_QSCALE)
    in_rope = (lane >= B_NOPE) & (lane < B_NOPE + B_ROPE)
    nat_b, trans_b = build(np.where(in_rope, lane - B_NOPE, 0), in_rope, B_ROPE // 2, B_QSCALE)
    return nat_a + nat_b, trans_a + trans_b


def kernel(x, c, ctx, c_ctx, w_ada, b_ada, w_in, q_a_norm, kv_a_norm, w_q_up, w_kv_up,
           diff_lambda, diff_subln, w_out, w_ffn_in, w_ffn_out, final_norm):
    nb, seq, _ = x.shape
    assert w_ada.shape[0] == 1 and seq % ATT_TILE == 0 and ctx.shape[1] == CTX_LEN == TOKEN_TILE

    pad = (-(nb + 1)) % SUBLANES
    c_all = jnp.concatenate([c, c_ctx[None, :], jnp.zeros((pad, D_MODEL), _F32)], axis=0)
    mod = _adaln(c_all, w_ada[0], b_ada[0][None, :])
    mod3 = mod[:, None, :]

    w = w_in[0]
    o_k, o_v, o_cq = A_WIDTH, 2 * A_WIDTH, 3 * A_WIDTH
    o_ckv, o_kr = o_cq + B_Q_RANK, o_cq + B_Q_RANK + B_KV_RANK
    zeros = lambda n: jnp.zeros((D_MODEL, n), w.dtype)
    w_c = w[:, o_cq:o_kr].T
    w_qv = jnp.concatenate([w[:, :o_k], w[:, o_v:o_cq]], axis=1).T
    w_n = jnp.concatenate([w[:, o_k:o_v], w[:, o_ckv:o_kr], zeros(B_NOPE), w[:, o_kr:],
                           zeros(LANES - B_NOPE - B_ROPE)], axis=1)
    wq = jnp.pad(w_q_up[0].reshape(B_Q_RANK, B_HEADS, B_NOPE + B_ROPE),
                 ((0, 0), (0, 0), (0, LANES - B_NOPE - B_ROPE))).reshape(B_Q_RANK, B_HEADS * LANES)
    wkv3 = w_kv_up[0].reshape(B_KV_RANK, B_HEADS, B_NOPE + B_VDIM)
    wk = jnp.pad(wkv3[:, :, :B_NOPE], ((0, 0), (0, 0), (0, LANES - B_NOPE))).reshape(B_KV_RANK, B_HEADS * LANES)
    wv = wkv3[:, :, B_NOPE:].reshape(B_KV_RANK, B_WIDTH)
    sel = np.zeros((N_MAPS, (A_HEADS + B_HEADS) * LANES), np.float32)
    for hd in range(A_HEADS):
        sel[2 * hd, hd * LANES:hd * LANES + A_HEAD_DIM] = 1.0
        sel[2 * hd + 1, hd * LANES + A_HEAD_DIM:(hd + 1) * LANES] = 1.0
    for hd in range(B_HEADS):
        sel[2 * A_HEADS + hd, (A_HEADS + hd) * LANES:(A_HEADS + hd) * LANES + ONE_LANE] = 1.0
    weights = (w_c.astype(_BF16), w_n.astype(_BF16), w_qv.astype(_BF16),
               q_a_norm.reshape(B_Q_RANK, 1), kv_a_norm.reshape(B_KV_RANK, 1),
               kv_a_norm.reshape(1, B_KV_RANK),
               wq.T.astype(_BF16), wv.T.astype(_BF16), wk.astype(_BF16), jnp.asarray(sel, _BF16))

    tables_nat, tables_t = _rope_tables(seq)
    qa, ka, va, qb, kb, vb, kn2, qn2 = _proj(x, ctx, mod3, weights, tables_nat, tables_t)

    cat = _attend(diff_lambda[0], diff_subln.reshape(LANES, 1), qa, ka, va, qb, kb, vb, kn2, qn2)

    return _post(cat, x, mod3, w_out[0].astype(_BF16), w_ffn_in[0].astype(_BF16),
                 w_ffn_out[0].astype(_BF16), final_norm[None, :])
```
